```python
import math, functools
import jax, jax.numpy as jnp
from jax import lax
import numpy as np

D_MODEL = 1024
BATCH = 16
SEQ = 2048
DEPTH = 2
DEC_BATCH = 128
DEC_SEQ = 4
PAST_LEN = 16384
PAGE_SIZE = 128

HEAD_DIM = 64
N_Q_HEADS = 8
N_KV_HEADS = 2
GQA = N_Q_HEADS // N_KV_HEADS
ATT_W = N_Q_HEADS * HEAD_DIM
KV_W = N_KV_HEADS * HEAD_DIM
WINDOW = 128
BLOCK = 128
NUM_BUCKETS = 32
MAX_DISTANCE = 128
RW_HEADS = 8
RW_W = RW_HEADS * HEAD_DIM
DECAY_LORA = 64
ICLR_LORA = 64
GATE_LORA = 128
RW_IN_W = 3 * RW_W + DECAY_LORA + ICLR_LORA + GATE_LORA
ATT_IN_W = ATT_W + 2 * KV_W
IN_W = ATT_IN_W + RW_IN_W
MIX_W = ATT_W + RW_W
D_FF = -(-8 * D_MODEL // (3 * 256)) * 256
NORM_EPS = 1e-6
GN_EPS = 64e-5
NEG = -1e30

kernel_name = "hymba_swa_sink_rwkv7_decode_step"


def rmsnorm(x, g):
    xf = x.astype(jnp.float32)
    y = xf * lax.rsqrt(jnp.mean(xf * xf, -1, keepdims=True) + NORM_EPS)
    return (y * g.astype(jnp.float32)).astype(x.dtype)


def t5_bucket(d):
    max_exact = NUM_BUCKETS // 2
    n = np.maximum(d, 0)
    large = max_exact + (np.log(np.maximum(n, 1) / max_exact) / np.log(MAX_DISTANCE / max_exact)
                         * (NUM_BUCKETS - max_exact)).astype(np.int32)
    large = np.minimum(large, NUM_BUCKETS - 1)
    return np.where(n < max_exact, n, large).astype(np.int32)


def rel_bias(rel_table, d):
    b = rel_table[t5_bucket(d)].astype(jnp.float32)
    return jnp.transpose(b, (2, 0, 1)).reshape((N_KV_HEADS, GQA) + d.shape)


def sink_attention(q, k, v, bias, mask, sink):
    s = jnp.einsum('...qkgd,...skd->...kgqs', q, k).astype(jnp.float32) * (HEAD_DIM ** -0.5)
    s = jnp.where(mask, s + bias, NEG)
    sink_col = jnp.broadcast_to(sink.astype(jnp.float32).reshape(N_KV_HEADS, GQA, 1, 1), s.shape[:-1] + (1,))
    p = jax.nn.softmax(jnp.concatenate([s, sink_col], -1), axis=-1)[..., :-1]
    o = jnp.einsum('...kgqs,...skd->...qkgd', p.astype(v.dtype), v)
    return o.reshape(o.shape[:-3] + (ATT_W,))


def swa_prompt(q, k, v, rel_table, sink):
    B, T = q.shape[:2]
    nb = T // BLOCK
    qb = q.reshape(B, nb, BLOCK, N_KV_HEADS, GQA, HEAD_DIM)

    def band(t):
        tb = t.reshape(B, nb, BLOCK, N_KV_HEADS, HEAD_DIM)
        prev = jnp.concatenate([jnp.zeros_like(tb[:, :1]), tb[:, :-1]], 1)
        return jnp.concatenate([prev, tb], 2)

    qi = np.arange(BLOCK)[:, None]
    sj = np.arange(2 * BLOCK)[None, :]
    d = qi + BLOCK - sj
    inside = (d >= 0) & (d <= WINDOW)
    exists = (np.arange(nb)[:, None, None] > 0) | (sj[None] >= BLOCK)
    mask = (inside[None] & exists)[:, None, None]
    o = sink_attention(qb, band(k), band(v), rel_bias(rel_table, d), mask, sink)
    w = min(WINDOW, T)
    return o.reshape(B, T, ATT_W), k[:, -w:], v[:, -w:]


def swa_sample(q, k, v, k_cache, v_cache, rel_table, sink):
    S = q.shape[1]
    wc = k_cache.shape[1]
    kf = jnp.concatenate([k_cache.astype(k.dtype), k], 1)
    vf = jnp.concatenate([v_cache.astype(v.dtype), v], 1)
    d = (wc + np.arange(S))[:, None] - np.arange(wc + S)[None, :]
    mask = (d >= 0) & (d <= WINDOW)
    o = sink_attention(q, kf, vf, rel_bias(rel_table, d), mask, sink)
    return o, kf[:, -wc:], vf[:, -wc:]


def rwkv7_group(zc, zp, mu, w0, w2, a0, a2, g2, k_k, k_a, r_k, gn_w, gn_b, wkv0):
    f32 = jnp.float32
    B, T = zc.shape[:2]
    zm = (zc + (zp - zc) * mu).astype(f32)
    r, k, v, zw, za, zg = jnp.split(zm, [RW_W, 2 * RW_W, 3 * RW_W, 3 * RW_W + DECAY_LORA,
                                         3 * RW_W + DECAY_LORA + ICLR_LORA], -1)
    w = -jax.nn.softplus(-(w0 + jnp.tanh(zw) @ w2)) - 0.5
    decay = jnp.exp(-jnp.exp(w))
    a = jax.nn.sigmoid(a0 + za @ a2)
    g = jax.nn.sigmoid(zg) @ g2
    heads = lambda t: t.reshape(B, T, RW_HEADS, HEAD_DIM)
    kk = heads(k * k_k)
    kk = kk / jnp.maximum(jnp.sqrt(jnp.sum(kk * kk, -1, keepdims=True)), 1e-12)
    k = k * (1.0 + (a - 1.0) * k_a)
    r, k, v, decay, a = heads(r), heads(k), heads(v), heads(decay), heads(a)

    def step(S, inp):
        r_t, w_t, k_t, v_t, kk_t, a_t = inp
        Skk = jnp.einsum('bhvi,bhi->bhv', S, kk_t)
        S = (S * w_t[:, :, None, :] - Skk[..., None] * (kk_t * a_t)[:, :, None, :]
             + v_t[..., None] * k_t[:, :, None, :])
        return S, jnp.einsum('bhvk,bhk->bhv', S, r_t)

    xs = tuple(jnp.swapaxes(t, 0, 1) for t in (r, decay, k, v, kk, a))
    S_fin, o = lax.scan(step, wkv0.astype(f32), xs)
    o = jnp.swapaxes(o, 0, 1)
    mean = jnp.mean(o, -1, keepdims=True)
    var = jnp.mean(jnp.square(o - mean), -1, keepdims=True)
    o = ((o - mean) * lax.rsqrt(var + GN_EPS)).reshape(B, T, RW_W) * gn_w + gn_b
    bonus = jnp.sum(r * k * r_k.reshape(RW_HEADS, HEAD_DIM), -1, keepdims=True) * v
    o = (o + bonus.reshape(B, T, RW_W)) * g
    return o, S_fin


def decoder_layer(x, h_prev, wkv0, attn_fn, n1, w_in, w_out, mu, w0, w2, a0, a2, g2, k_k, k_a, r_k,
                  gn_w, gn_b, n2, w_gate, w_up, w_down):
    B, T = x.shape[:2]
    h = rmsnorm(x, n1)
    z = h @ w_in
    q, k, v = jnp.split(z[..., :ATT_IN_W], [ATT_W, ATT_W + KV_W], -1)
    zc = z[..., ATT_IN_W:]
    zp_first = (h_prev.astype(h.dtype) @ w_in[:, ATT_IN_W:]).astype(zc.dtype)
    zp = jnp.concatenate([zp_first[:, None], zc[:, :-1]], 1)
    att, k_state, v_state = attn_fn(q.reshape(B, T, N_KV_HEADS, GQA, HEAD_DIM),
                                    k.reshape(B, T, N_KV_HEADS, HEAD_DIM),
                                    v.reshape(B, T, N_KV_HEADS, HEAD_DIM))
    rw, wkv = rwkv7_group(zc, zp, mu, w0, w2, a0, a2, g2, k_k, k_a, r_k, gn_w, gn_b, wkv0)
    x = x + (jnp.concatenate([att, rw.astype(att.dtype)], -1) @ w_out).astype(x.dtype)
    hf = rmsnorm(x, n2)
    x = x + ((jax.nn.silu(hf @ w_gate) * (hf @ w_up)) @ w_down).astype(x.dtype)
    return x, h[:, -1], k_state, v_state, wkv


def setup_inputs(seed: int = 0) -> dict:
    key = jax.random.key(seed)
    ks = jax.random.split(key, 32)
    nrm = lambda i, shape, s: jax.random.normal(ks[i], shape, jnp.float32) * s
    wc = min(WINDOW, PAST_LEN)
    return {
        'x_prompt': nrm(0, (BATCH, SEQ, D_MODEL), 1.0),
        'x_sample': nrm(1, (DEC_BATCH, DEC_SEQ, D_MODEL), 1.0),
        'cache_k': nrm(2, (DEPTH, DEC_BATCH, wc, N_KV_HEADS, HEAD_DIM), 1.0),
        'cache_v': nrm(3, (DEPTH, DEC_BATCH, wc, N_KV_HEADS, HEAD_DIM), 1.0),
        'state_shift': nrm(4, (DEPTH, DEC_BATCH, D_MODEL), 1.0),
        'state_wkv': nrm(5, (DEPTH, DEC_BATCH, RW_HEADS, HEAD_DIM, HEAD_DIM), 0.5),
        'norm1': 1.0 + nrm(6, (DEPTH, D_MODEL), 0.02),
        'w_in': nrm(7, (DEPTH, D_MODEL, IN_W), D_MODEL ** -0.5),
        'w_out': nrm(8, (DEPTH, MIX_W, D_MODEL), MIX_W ** -0.5),
        'sink': nrm(9, (DEPTH, N_Q_HEADS), 0.5),
        'rel_table': nrm(10, (NUM_BUCKETS, N_Q_HEADS), 0.5),
        'mu': jax.random.uniform(ks[11], (DEPTH, RW_IN_W), jnp.float32),
        'w0': jax.random.uniform(ks[12], (DEPTH, RW_W), jnp.float32, -6.0, -1.0),
        'w2': nrm(13, (DEPTH, DECAY_LORA, RW_W), 0.1),
        'a0': nrm(14, (DEPTH, RW_W), 0.1),
        'a2': nrm(15, (DEPTH, ICLR_LORA, RW_W), 0.1),
        'g2': nrm(16, (DEPTH, GATE_LORA, RW_W), GATE_LORA ** -0.5),
        'k_k': 0.85 + nrm(17, (DEPTH, RW_W), 0.05),
        'k_a': 1.0 + nrm(18, (DEPTH, RW_W), 0.05),
        'r_k': nrm(19, (DEPTH, RW_W), 0.1),
        'gn_w': 1.0 + nrm(20, (DEPTH, RW_W), 0.02),
        'gn_b': nrm(21, (DEPTH, RW_W), 0.02),
        'norm2': 1.0 + nrm(22, (DEPTH, D_MODEL), 0.02),
        'w_gate': nrm(23, (DEPTH, D_MODEL, D_FF), D_MODEL ** -0.5),
        'w_up': nrm(24, (DEPTH, D_MODEL, D_FF), D_MODEL ** -0.5),
        'w_down': nrm(25, (DEPTH, D_FF, D_MODEL), D_FF ** -0.5),
        'norm_f': 1.0 + nrm(26, (D_MODEL,), 0.02),
    }


def reference(x_prompt, x_sample, cache_k, cache_v, state_shift, state_wkv, norm1, w_in, w_out, sink,
              rel_table, mu, w0, w2, a0, a2, g2, k_k, k_a, r_k, gn_w, gn_b, norm2, w_gate, w_up,
              w_down, norm_f):
    xp, xs = x_prompt, x_sample
    kp, vp, sp, wp, ksl, vsl, ssl, wsl = [], [], [], [], [], [], [], []
    for l in range(DEPTH):
        lw = (norm1[l], w_in[l], w_out[l], mu[l], w0[l], w2[l], a0[l], a2[l], g2[l], k_k[l], k_a[l],
              r_k[l], gn_w[l], gn_b[l], norm2[l], w_gate[l], w_up[l], w_down[l])
        attn_p = functools.partial(swa_prompt, rel_table=rel_table, sink=sink[l])
        wkv_zero = jnp.zeros((xp.shape[0], RW_HEADS, HEAD_DIM, HEAD_DIM), jnp.float32)
        xp, h_last, k_new, v_new, wkv = decoder_layer(xp, jnp.zeros_like(xp[:, 0]), wkv_zero, attn_p, *lw)
        kp.append(k_new); vp.append(v_new); sp.append(h_last); wp.append(wkv)
        attn_s = functools.partial(swa_sample, k_cache=cache_k[l], v_cache=cache_v[l],
                                   rel_table=rel_table, sink=sink[l])
        xs, h_last_s, k_buf, v_buf, wkv_s = decoder_layer(xs, state_shift[l], state_wkv[l], attn_s, *lw)
        ksl.append(k_buf); vsl.append(v_buf); ssl.append(h_last_s); wsl.append(wkv_s)
    y_prompt = rmsnorm(xp, norm_f)
    y_sample = rmsnorm(xs, norm_f)
    k_prompt, v_prompt = jnp.stack(kp), jnp.stack(vp)
    shift_prompt, wkv_prompt = jnp.stack(sp), jnp.stack(wp)
    k_sample, v_sample = jnp.stack(ksl), jnp.stack(vsl)
    shift_sample, wkv_sample = jnp.stack(ssl), jnp.stack(wsl)
    return (y_prompt, y_sample, k_prompt, v_prompt, shift_prompt, wkv_prompt,
            k_sample, v_sample, shift_sample, wkv_sample)
```

```python
import functools

import numpy as np
import jax
import jax.numpy as jnp
from jax import lax
from jax.experimental import pallas as pl
from jax.experimental.pallas import tpu as pltpu

D_MODEL = 1024
HEAD_DIM = 64
N_Q_HEADS = 8
N_KV_HEADS = 2
GQA = N_Q_HEADS // N_KV_HEADS
ATT_W = N_Q_HEADS * HEAD_DIM
KV_W = N_KV_HEADS * HEAD_DIM
WINDOW = 128
BLOCK = 128
NUM_BUCKETS = 32
MAX_DISTANCE = 128
RW_HEADS = 8
RW_W = RW_HEADS * HEAD_DIM
DECAY_LORA = 64
ICLR_LORA = 64
GATE_LORA = 128
RW_IN_W = 3 * RW_W + DECAY_LORA + ICLR_LORA + GATE_LORA
ATT_IN_W = ATT_W + 2 * KV_W
NORM_EPS = 1e-6
GN_EPS = 64e-5
NEG = -1e30
SCALE = HEAD_DIM ** -0.5

V7X_VMEM_LIMIT_BYTES = 56 * 1024 * 1024

PROMPT_CHUNK = 64
SAMPLE_CHUNK = 16
SAMPLE_PAD = 8

BF16 = jnp.bfloat16
F32 = jnp.float32


def _mm(a, b):
    return jnp.dot(a.astype(BF16), b.astype(BF16), preferred_element_type=F32)


def _mm_nt(a, b):
    return lax.dot_general(a.astype(BF16), b.astype(BF16), (((1,), (1,)), ((), ())),
                           preferred_element_type=F32)


def _mm_tn(a, b):
    return lax.dot_general(a.astype(BF16), b.astype(BF16), (((0,), (0,)), ((), ())),
                           preferred_element_type=F32)


def _rmsnorm(x, g):
    return x * lax.rsqrt(jnp.mean(x * x, -1, keepdims=True) + NORM_EPS) * g


def _params(n_axes):
    return pltpu.CompilerParams(dimension_semantics=("arbitrary",) * n_axes,
                                vmem_limit_bytes=V7X_VMEM_LIMIT_BYTES)


def _inproj_body(x_ref, g_ref, w_ref, za_ref, zr_ref, h_ref, *, norm):
    x = x_ref[...]
    h = _rmsnorm(x, g_ref[...]) if norm else x
    h_ref[...] = h
    z = jnp.dot(h.astype(BF16), w_ref[...], preferred_element_type=F32)
    za_ref[...] = z[:, :ATT_IN_W]
    zr_ref[...] = z[:, ATT_IN_W:]


def _inproj(x, g, w, *, norm, tm):
    n = x.shape[0]
    in_w = w.shape[1]
    return pl.pallas_call(
        functools.partial(_inproj_body, norm=norm),
        grid=(n // tm,),
        in_specs=[pl.BlockSpec((tm, D_MODEL), lambda i: (i, 0)),
                  pl.BlockSpec((1, D_MODEL), lambda i: (0, 0)),
                  pl.BlockSpec((D_MODEL, in_w), lambda i: (0, 0))],
        out_specs=[pl.BlockSpec((tm, ATT_IN_W), lambda i: (i, 0)),
                   pl.BlockSpec((tm, RW_IN_W), lambda i: (i, 0)),
                   pl.BlockSpec((tm, D_MODEL), lambda i: (i, 0))],
        out_shape=[jax.ShapeDtypeStruct((n, ATT_IN_W), F32),
                   jax.ShapeDtypeStruct((n, RW_IN_W), F32),
                   jax.ShapeDtypeStruct((n, D_MODEL), F32)],
        compiler_params=_params(1),
        name="inproj_norm" if norm else "inproj_plain",
    )(x, g, w)


def _sink_softmax(score_parts, sink_col):
    m = sink_col
    for s in score_parts:
        m = jnp.maximum(m, jnp.max(s, -1, keepdims=True))
    es = [jnp.exp(s - m) for s in score_parts]
    den = jnp.exp(sink_col - m)
    for e in es:
        den = den + jnp.sum(e, -1, keepdims=True)
    return [e / den for e in es]


def _attn_prompt_body(q_ref, kc_ref, vc_ref, kp_ref, vp_ref, bias_ref, sink_ref, o_ref):
    j = pl.program_id(1)
    q = q_ref[0]
    kc, vc, kp, vp = kc_ref[0], vc_ref[0], kp_ref[0], vp_ref[0]
    col = lax.broadcasted_iota(jnp.int32, (GQA * BLOCK, 2 * BLOCK), 1)
    exists = (j > 0) | (col >= BLOCK)
    outs = []
    for kv in range(N_KV_HEADS):
        sl = slice(kv * HEAD_DIM, (kv + 1) * HEAD_DIM)
        qg = jnp.concatenate(
            [q[:, (kv * GQA + g) * HEAD_DIM:(kv * GQA + g + 1) * HEAD_DIM] for g in range(GQA)], axis=0)
        kcat = jnp.concatenate([kp[:, sl], kc[:, sl]], axis=0)
        vcat = jnp.concatenate([vp[:, sl], vc[:, sl]], axis=0)
        s = _mm_nt(qg, kcat) * SCALE
        b = bias_ref[kv]
        s = jnp.where((b > 0.5 * NEG) & exists, s + b, NEG)
        (p,) = _sink_softmax([s], sink_ref[kv])
        o = _mm(p, vcat)
        outs += [o[g * BLOCK:(g + 1) * BLOCK] for g in range(GQA)]
    o_ref[0] = jnp.concatenate(outs, axis=1)


def _attn_prompt(za, bias, sink_rows):
    b, t, _ = za.shape
    nb = t // BLOCK
    kcol, vcol = ATT_W // KV_W, ATT_W // KV_W + 1
    return pl.pallas_call(
        _attn_prompt_body,
        grid=(b, nb),
        in_specs=[pl.BlockSpec((1, BLOCK, ATT_W), lambda i, j: (i, j, 0)),
                  pl.BlockSpec((1, BLOCK, KV_W), lambda i, j: (i, j, kcol)),
                  pl.BlockSpec((1, BLOCK, KV_W), lambda i, j: (i, j, vcol)),
                  pl.BlockSpec((1, BLOCK, KV_W), lambda i, j: (i, jnp.maximum(j - 1, 0), kcol)),
                  pl.BlockSpec((1, BLOCK, KV_W), lambda i, j: (i, jnp.maximum(j - 1, 0), vcol)),
                  pl.BlockSpec(bias.shape, lambda i, j: (0, 0, 0)),
                  pl.BlockSpec(sink_rows.shape, lambda i, j: (0, 0, 0))],
        out_specs=pl.BlockSpec((1, BLOCK, ATT_W), lambda i, j: (i, j, 0)),
        out_shape=jax.ShapeDtypeStruct((b, t, ATT_W), F32),
        compiler_params=_params(2),
        name="attn_prompt",
    )(za, za, za, za, za, bias, sink_rows)


def _attn_sample_body(z_ref, ck_ref, cv_ref, bias_c_ref, bias_n_ref, sink_ref,
                      o_ref, kb_ref, vb_ref, *, n_new):
    z = z_ref[0]
    q = z[:, :ATT_W]
    kn = z[:, ATT_W:ATT_W + KV_W]
    vn = z[:, ATT_W + KV_W:]
    ck, cv = ck_ref[0], cv_ref[0]
    outs = []
    for kv in range(N_KV_HEADS):
        sl = slice(kv * HEAD_DIM, (kv + 1) * HEAD_DIM)
        qg = jnp.concatenate(
            [q[:, (kv * GQA + g) * HEAD_DIM:(kv * GQA + g + 1) * HEAD_DIM] for g in range(GQA)], axis=0)
        bc, bn = bias_c_ref[kv], bias_n_ref[kv]
        sc = _mm_nt(qg, ck[:, sl]) * SCALE
        sn = _mm_nt(qg, kn[:, sl]) * SCALE
        sc = jnp.where(bc > 0.5 * NEG, sc + bc, NEG)
        sn = jnp.where(bn > 0.5 * NEG, sn + bn, NEG)
        pc, pn = _sink_softmax([sc, sn], sink_ref[kv])
        o = _mm(pc, cv[:, sl]) + _mm(pn, vn[:, sl])
        outs += [o[g * SAMPLE_PAD:(g + 1) * SAMPLE_PAD] for g in range(GQA)]
    o_ref[0] = jnp.concatenate(outs, axis=1)
    keep = WINDOW - n_new
    kb_ref[0, 0:keep, :] = ck_ref[0, n_new:WINDOW, :]
    kb_ref[0, keep:WINDOW, :] = kn[0:n_new]
    vb_ref[0, 0:keep, :] = cv_ref[0, n_new:WINDOW, :]
    vb_ref[0, keep:WINDOW, :] = vn[0:n_new]


def _attn_sample(za_pad, ck, cv, bias_c, bias_n, sink_rows, *, n_new):
    db = za_pad.shape[0]
    full = lambda a: pl.BlockSpec(a.shape, lambda i: (0,) * a.ndim)
    return pl.pallas_call(
        functools.partial(_attn_sample_body, n_new=n_new),
        grid=(db,),
        in_specs=[pl.BlockSpec((1, SAMPLE_PAD, ATT_IN_W), lambda i: (i, 0, 0)),
                  pl.BlockSpec((1, WINDOW, KV_W), lambda i: (i, 0, 0)),
                  pl.BlockSpec((1, WINDOW, KV_W), lambda i: (i, 0, 0)),
                  full(bias_c), full(bias_n), full(sink_rows)],
        out_specs=[pl.BlockSpec((1, SAMPLE_PAD, ATT_W), lambda i: (i, 0, 0)),
                   pl.BlockSpec((1, WINDOW, KV_W), lambda i: (i, 0, 0)),
                   pl.BlockSpec((1, WINDOW, KV_W), lambda i: (i, 0, 0))],
        out_shape=[jax.ShapeDtypeStruct((db, SAMPLE_PAD, ATT_W), F32),
                   jax.ShapeDtypeStruct((db, WINDOW, KV_W), F32),
                   jax.ShapeDtypeStruct((db, WINDOW, KV_W), F32)],
        compiler_params=_params(1),
        name="attn_sample",
    )(za_pad, ck, cv, bias_c, bias_n, sink_rows)


def _split3(x):
    hi = x.astype(BF16)
    r1 = x - hi.astype(F32)
    mid = r1.astype(BF16)
    lo = (r1 - mid.astype(F32)).astype(BF16)
    return hi, mid, lo


def _rwkv_body(z_ref, zpf_ref, s0_ref, mu_ref, vec_ref, w2_ref, a2_ref, g2_ref,
               o_ref, sout_ref, state, zbuf, *, chunk, n_valid, solve_steps):
    C = chunk
    c = pl.program_id(1)

    @pl.when(c == 0)
    def _():
        state[...] = s0_ref[0]
        zbuf[7:8, :] = zpf_ref[0]

    @pl.when(c > 0)
    def _():
        zbuf[7:8, :] = zbuf[C + 7:C + 8, :]

    zc = z_ref[0]
    zbuf[8:8 + C, :] = zc
    zp = zbuf[7:7 + C, :]
    zm = zc + (zp - zc) * mu_ref[...]

    r = zm[:, 0:RW_W]
    k = zm[:, RW_W:2 * RW_W]
    v = zm[:, 2 * RW_W:3 * RW_W]
    o1 = 3 * RW_W
    zw = zm[:, o1:o1 + DECAY_LORA]
    za = zm[:, o1 + DECAY_LORA:o1 + DECAY_LORA + ICLR_LORA]
    zg = zm[:, o1 + DECAY_LORA + ICLR_LORA:]
    w0, a0, k_k, k_a = vec_ref[0:1, :], vec_ref[1:2, :], vec_ref[2:3, :], vec_ref[3:4, :]
    r_k, gn_w, gn_b = vec_ref[4:5, :], vec_ref[5:6, :], vec_ref[6:7, :]

    y = -(w0 + _mm(jnp.tanh(zw), w2_ref[...]))
    softplus = jnp.maximum(y, 0.0) + jnp.log1p(jnp.exp(-jnp.abs(y)))
    logdecay = -jnp.exp(-softplus - 0.5)
    a = jax.nn.sigmoid(a0 + _mm(za, a2_ref[...]))
    g = _mm(jax.nn.sigmoid(zg), g2_ref[...])
    kk_raw = k * k_k
    kmod = k * (1.0 + (a - 1.0) * k_a)
    vv = v
    if n_valid < C:
        rows = lax.broadcasted_iota(jnp.int32, (C, 1), 0) < n_valid
        logdecay = jnp.where(rows, logdecay, 0.0)
        kk_raw = jnp.where(rows, kk_raw, 0.0)
        kmod = jnp.where(rows, kmod, 0.0)
        vv = jnp.where(rows, v, 0.0)

    ri = lax.broadcasted_iota(jnp.int32, (C, C), 0)
    ci = lax.broadcasted_iota(jnp.int32, (C, C), 1)
    incl = ri >= ci
    strict = ri > ci
    tri = incl.astype(BF16)
    hi, mid, lo = _split3(logdecay)
    dotf = lambda x: jnp.dot(tri, x, preferred_element_type=F32)
    cum = dotf(hi) + dotf(mid) + dotf(lo)
    gam = jnp.exp(cum)
    gam_inv = jnp.exp(-cum)
    gam_ex = jnp.exp(cum - logdecay)

    outs = []
    for h in range(RW_HEADS):
        sl = slice(h * HEAD_DIM, (h + 1) * HEAD_DIM)
        kk = kk_raw[:, sl]
        kk = kk / jnp.maximum(jnp.sqrt(jnp.sum(kk * kk, -1, keepdims=True)), 1e-12)
        At = -kk * gam_ex[:, sl]
        Bt = kk * a[:, sl] * gam_inv[:, sl]
        Kt = kmod[:, sl] * gam_inv[:, sl]
        Rt = r[:, sl] * gam[:, sl]
        V = vv[:, sl]
        S = state[h]
        L = jnp.where(strict, _mm_nt(At, Bt), 0.0)
        Lak = jnp.where(strict, _mm_nt(At, Kt), 0.0)
        Mrb = jnp.where(incl, _mm_nt(Rt, Bt), 0.0)
        Mrk = jnp.where(incl, _mm_nt(Rt, Kt), 0.0)
        X = _mm_nt(At, S) + _mm(Lak, V)
        P = L
        for i in range(solve_steps):
            X = X + _mm(P, X)
            if i + 1 < solve_steps:
                P = _mm(P, P)
        U = X
        O = _mm_nt(Rt, S) + _mm(Mrb, U) + _mm(Mrk, V)
        state[h] = (S + _mm_tn(U, Bt) + _mm_tn(V, Kt)) * gam[C - 1:C, sl]

        mean = jnp.mean(O, -1, keepdims=True)
        d = O - mean
        var = jnp.mean(d * d, -1, keepdims=True)
        on = d * lax.rsqrt(var + GN_EPS) * gn_w[:, sl] + gn_b[:, sl]
        bonus = jnp.sum(r[:, sl] * kmod[:, sl] * r_k[:, sl], -1, keepdims=True) * v[:, sl]
        outs.append((on + bonus) * g[:, sl])
    o_ref[0] = jnp.concatenate(outs, axis=1)

    @pl.when(c == pl.num_programs(1) - 1)
    def _():
        sout_ref[0] = state[...]


def _rwkv(zr, zp_first, s0, mu, vecs, w2, a2, g2, *, chunk, n_valid):
    b, t, _ = zr.shape
    solve_steps = max(1, (n_valid - 1).bit_length())
    full = lambda a: pl.BlockSpec(a.shape, lambda i, j: (0,) * a.ndim)
    return pl.pallas_call(
        functools.partial(_rwkv_body, chunk=chunk, n_valid=n_valid, solve_steps=solve_steps),
        grid=(b, t // chunk),
        in_specs=[pl.BlockSpec((1, chunk, RW_IN_W), lambda i, j: (i, j, 0)),
                  pl.BlockSpec((1, 1, RW_IN_W), lambda i, j: (i, 0, 0)),
                  pl.BlockSpec((1, RW_HEADS, HEAD_DIM, HEAD_DIM), lambda i, j: (i, 0, 0, 0)),
                  full(mu), full(vecs), full(w2), full(a2), full(g2)],
        out_specs=[pl.BlockSpec((1, chunk, RW_W), lambda i, j: (i, j, 0)),
                   pl.BlockSpec((1, RW_HEADS, HEAD_DIM, HEAD_DIM), lambda i, j: (i, 0, 0, 0))],
        out_shape=[jax.ShapeDtypeStruct((b, t, RW_W), F32),
                   jax.ShapeDtypeStruct((b, RW_HEADS, HEAD_DIM, HEAD_DIM), F32)],
        scratch_shapes=[pltpu.VMEM((RW_HEADS, HEAD_DIM, HEAD_DIM), F32),
                        pltpu.VMEM((chunk + 8, RW_IN_W), F32)],
        compiler_params=_params(2),
        name="rwkv_chunk%d" % chunk,
    )(zr, zp_first, s0, mu, vecs, w2, a2, g2)


def _outffn_body(x_ref, att_ref, rw_ref, wo_ref, n2_ref, wg_ref, wu_ref, wd_ref, nf_ref, xo_ref, *, final):
    mix = jnp.concatenate([att_ref[...], rw_ref[...]], axis=-1).astype(BF16)
    x1 = x_ref[...] + jnp.dot(mix, wo_ref[...], preferred_element_type=F32)
    hb = _rmsnorm(x1, n2_ref[...]).astype(BF16)
    gate = jnp.dot(hb, wg_ref[...], preferred_element_type=F32)
    up = jnp.dot(hb, wu_ref[...], preferred_element_type=F32)
    act = (gate * jax.nn.sigmoid(gate) * up).astype(BF16)
    x2 = x1 + jnp.dot(act, wd_ref[...], preferred_element_type=F32)
    xo_ref[...] = _rmsnorm(x2, nf_ref[...]) if final else x2


def _outffn(x, att, rw, wo, n2, wg, wu, wd, nf, *, final, tm):
    n = x.shape[0]
    resident = lambda a: pl.BlockSpec(a.shape, lambda i: (0,) * a.ndim, pipeline_mode=pl.Buffered(1))
    return pl.pallas_call(
        functools.partial(_outffn_body, final=final),
        grid=(n // tm,),
        in_specs=[pl.BlockSpec((tm, D_MODEL), lambda i: (i, 0)),
                  pl.BlockSpec((tm, ATT_W), lambda i: (i, 0)),
                  pl.BlockSpec((tm, RW_W), lambda i: (i, 0)),
                  resident(wo), resident(n2), resident(wg), resident(wu), resident(wd), resident(nf)],
        out_specs=pl.BlockSpec((tm, D_MODEL), lambda i: (i, 0)),
        out_shape=jax.ShapeDtypeStruct((n, D_MODEL), F32),
        compiler_params=_params(1),
        name="outffn_final" if final else "outffn",
    )(x, att, rw, wo, n2, wg, wu, wd, nf)


def _t5_bucket_np(d):
    max_exact = NUM_BUCKETS // 2
    n = np.maximum(d, 0)
    large = max_exact + (np.log(np.maximum(n, 1) / max_exact) / np.log(MAX_DISTANCE / max_exact)
                         * (NUM_BUCKETS - max_exact)).astype(np.int32)
    large = np.minimum(large, NUM_BUCKETS - 1)
    return np.where(n < max_exact, n, large).astype(np.int32)


def _masked_bias(rel_table, d, inside, rows_per_head):
    b = rel_table[_t5_bucket_np(d)].astype(F32)
    b = jnp.where(jnp.asarray(inside)[:, :, None], b, NEG)
    b = jnp.transpose(b, (2, 0, 1))
    q, s = d.shape
    if rows_per_head > q:
        b = jnp.concatenate([b, jnp.full((N_Q_HEADS, rows_per_head - q, s), NEG, F32)], axis=1)
    return b.reshape(N_KV_HEADS, GQA * rows_per_head, s)


def _sink_rows(sink_l, rows_per_head):
    return jnp.repeat(sink_l.astype(F32).reshape(N_KV_HEADS, GQA, 1), rows_per_head, axis=1).reshape(
        N_KV_HEADS, GQA * rows_per_head, 1)


def kernel(x_prompt, x_sample, cache_k, cache_v, state_shift, state_wkv, norm1, w_in, w_out, sink,
           rel_table, mu, w0, w2, a0, a2, g2, k_k, k_a, r_k, gn_w, gn_b, norm2, w_gate, w_up,
           w_down, norm_f):
    depth = w_in.shape[0]
    B, T, _ = x_prompt.shape
    DB, S, _ = x_sample.shape
    wc = cache_k.shape[2]

    qi = np.arange(BLOCK)[:, None]
    sj = np.arange(2 * BLOCK)[None, :]
    dp = qi + BLOCK - sj
    bias_p = _masked_bias(rel_table, dp, (dp >= 0) & (dp <= WINDOW), BLOCK)
    ds_full = (wc + np.arange(S))[:, None] - np.arange(wc + S)[None, :]
    inside_s = (ds_full >= 0) & (ds_full <= WINDOW)
    bias_c = _masked_bias(rel_table, ds_full[:, :wc], inside_s[:, :wc], SAMPLE_PAD)
    dn = np.full((S, SAMPLE_PAD), -1, np.int64)
    dn[:, :S] = ds_full[:, wc:]
    inn = np.zeros((S, SAMPLE_PAD), bool)
    inn[:, :S] = inside_s[:, wc:]
    bias_n = _masked_bias(rel_table, dn, inn, SAMPLE_PAD)

    xp = x_prompt.reshape(B * T, D_MODEL)
    xs = x_sample.reshape(DB * S, D_MODEL)
    zeros_shift = jnp.zeros((B, 1, RW_IN_W), F32)
    zeros_state = jnp.zeros((B, RW_HEADS, HEAD_DIM, HEAD_DIM), F32)
    nf = norm_f.reshape(1, D_MODEL)
    ones = jnp.ones((1, D_MODEL), F32)

    kp, vp, sp, wp, ksl, vsl, ssl, wsl = [], [], [], [], [], [], [], []
    for l in range(depth):
        w_in_b = w_in[l].astype(BF16)
        wo_b, wg_b, wu_b, wd_b = (w_out[l].astype(BF16), w_gate[l].astype(BF16),
                                  w_up[l].astype(BF16), w_down[l].astype(BF16))
        n1 = norm1[l].reshape(1, D_MODEL)
        n2 = norm2[l].reshape(1, D_MODEL)
        mu_l = mu[l].reshape(1, RW_IN_W)
        vecs = jnp.stack([w0[l], a0[l], k_k[l], k_a[l], r_k[l], gn_w[l], gn_b[l], jnp.zeros_like(w0[l])])
        w2_b, a2_b, g2_b = w2[l].astype(BF16), a2[l].astype(BF16), g2[l].astype(BF16)
        final = l == depth - 1

        za, zr, h = _inproj(xp, n1, w_in_b, norm=True, tm=512)
        za3 = za.reshape(B, T, ATT_IN_W)
        w = min(WINDOW, T)
        kp.append(za3[:, T - w:, ATT_W:ATT_W + KV_W].reshape(B, w, N_KV_HEADS, HEAD_DIM))
        vp.append(za3[:, T - w:, ATT_W + KV_W:].reshape(B, w, N_KV_HEADS, HEAD_DIM))
        sp.append(h.reshape(B, T, D_MODEL)[:, -1])
        att = _attn_prompt(za3, bias_p, _sink_rows(sink[l], BLOCK))
        rw, wkv = _rwkv(zr.reshape(B, T, RW_IN_W), zeros_shift, zeros_state, mu_l, vecs, w2_b, a2_b, g2_b,
                        chunk=PROMPT_CHUNK, n_valid=PROMPT_CHUNK)
        wp.append(wkv)
        xp = _outffn(xp, att.reshape(B * T, ATT_W), rw.reshape(B * T, RW_W), wo_b, n2, wg_b, wu_b, wd_b, nf,
                     final=final, tm=256)

        za_s, zr_s, h_s = _inproj(xs, n1, w_in_b, norm=True, tm=DB * S)
        _, zr_prev, _ = _inproj(state_shift[l], ones, w_in_b, norm=False, tm=DB)
        ssl.append(h_s.reshape(DB, S, D_MODEL)[:, -1])
        za_pad = jnp.pad(za_s.reshape(DB, S, ATT_IN_W), ((0, 0), (0, SAMPLE_PAD - S), (0, 0)))
        att_s, kb, vb = _attn_sample(za_pad, cache_k[l].reshape(DB, wc, KV_W), cache_v[l].reshape(DB, wc, KV_W),
                                     bias_c, bias_n, _sink_rows(sink[l], SAMPLE_PAD), n_new=S)
        ksl.append(kb.reshape(DB, wc, N_KV_HEADS, HEAD_DIM))
        vsl.append(vb.reshape(DB, wc, N_KV_HEADS, HEAD_DIM))
        zr_pad = jnp.pad(zr_s.reshape(DB, S, RW_IN_W), ((0, 0), (0, SAMPLE_CHUNK - S), (0, 0)))
        rw_s, wkv_s = _rwkv(zr_pad, zr_prev.reshape(DB, 1, RW_IN_W), state_wkv[l], mu_l, vecs, w2_b, a2_b, g2_b,
                            chunk=SAMPLE_CHUNK, n_valid=S)
        wsl.append(wkv_s)
        xs = _outffn(xs, att_s[:, :S].reshape(DB * S, ATT_W), rw_s[:, :S].reshape(DB * S, RW_W),
                     wo_b, n2, wg_b, wu_b, wd_b, nf, final=final, tm=DB * S)

    y_prompt = xp.reshape(B, T, D_MODEL)
    y_sample = xs.reshape(DB, S, D_MODEL)
    return (y_prompt, y_sample, jnp.stack(kp), jnp.stack(vp), jnp.stack(sp), jnp.stack(wp),
            jnp.stack(ksl), jnp.stack(vsl), jnp.stack(ssl), jnp.stack(wsl))
```

```python
import functools

import numpy as np
import jax
import jax.numpy as jnp
from jax import lax
from jax.experimental import pallas as pl
from jax.experimental.pallas import tpu as pltpu

D_MODEL = 1024
HEAD_DIM = 64
N_Q_HEADS = 8
N_KV_HEADS = 2
GQA = N_Q_HEADS // N_KV_HEADS
ATT_W = N_Q_HEADS * HEAD_DIM
KV_W = N_KV_HEADS * HEAD_DIM
WINDOW = 128
BLOCK = 128
NUM_BUCKETS = 32
MAX_DISTANCE = 128
RW_HEADS = 8
RW_W = RW_HEADS * HEAD_DIM
DECAY_LORA = 64
ICLR_LORA = 64
GATE_LORA = 128
RW_IN_W = 3 * RW_W + DECAY_LORA + ICLR_LORA + GATE_LORA
ATT_IN_W = ATT_W + 2 * KV_W
NORM_EPS = 1e-6
GN_EPS = 64e-5
NEG = -1e30
SCALE = HEAD_DIM ** -0.5

V7X_VMEM_LIMIT_BYTES = 56 * 1024 * 1024

PROMPT_CHUNK = 64
SAMPLE_CHUNK = 16
SAMPLE_PAD = 8

BF16 = jnp.bfloat16
F32 = jnp.float32


def _mm(a, b):
    return jnp.dot(a.astype(BF16), b.astype(BF16), preferred_element_type=F32)


def _mm_nt(a, b):
    return lax.dot_general(a.astype(BF16), b.astype(BF16), (((1,), (1,)), ((), ())),
                           preferred_element_type=F32)


def _mm_tn(a, b):
    return lax.dot_general(a.astype(BF16), b.astype(BF16), (((0,), (0,)), ((), ())),
                           preferred_element_type=F32)


def _rmsnorm(x, g):
    return x * lax.rsqrt(jnp.mean(x * x, -1, keepdims=True) + NORM_EPS) * g


def _params(n_axes):
    return pltpu.CompilerParams(dimension_semantics=("arbitrary",) * n_axes,
                                vmem_limit_bytes=V7X_VMEM_LIMIT_BYTES)


def _inproj_body(x_ref, g_ref, w_ref, za_ref, zr_ref, h_ref, *, norm):
    x = x_ref[...]
    h = _rmsnorm(x, g_ref[...]) if norm else x
    h_ref[...] = h
    z = jnp.dot(h.astype(BF16), w_ref[...], preferred_element_type=F32)
    za_ref[...] = z[:, :ATT_IN_W]
    zr_ref[...] = z[:, ATT_IN_W:]


def _inproj(x, g, w, *, norm, tm):
    n = x.shape[0]
    in_w = w.shape[1]
    return pl.pallas_call(
        functools.partial(_inproj_body, norm=norm),
        grid=(n // tm,),
        in_specs=[pl.BlockSpec((tm, D_MODEL), lambda i: (i, 0)),
                  pl.BlockSpec((1, D_MODEL), lambda i: (0, 0)),
                  pl.BlockSpec((D_MODEL, in_w), lambda i: (0, 0))],
        out_specs=[pl.BlockSpec((tm, ATT_IN_W), lambda i: (i, 0)),
                   pl.BlockSpec((tm, RW_IN_W), lambda i: (i, 0)),
                   pl.BlockSpec((tm, D_MODEL), lambda i: (i, 0))],
        out_shape=[jax.ShapeDtypeStruct((n, ATT_IN_W), F32),
                   jax.ShapeDtypeStruct((n, RW_IN_W), F32),
                   jax.ShapeDtypeStruct((n, D_MODEL), F32)],
        compiler_params=_params(1),
        name="inproj_norm" if norm else "inproj_plain",
    )(x, g, w)


def _sink_softmax(score_parts, sink_col):
    m = sink_col
    for s in score_parts:
        m = jnp.maximum(m, jnp.max(s, -1, keepdims=True))
    es = [jnp.exp(s - m) for s in score_parts]
    den = jnp.exp(sink_col - m)
    for e in es:
        den = den + jnp.sum(e, -1, keepdims=True)
    return [e / den for e in es]


def _attn_prompt_body(q_ref, kc_ref, vc_ref, kp_ref, vp_ref, bias_ref, sink_ref, o_ref):
    j = pl.program_id(1)
    q = q_ref[0]
    kc, vc, kp, vp = kc_ref[0], vc_ref[0], kp_ref[0], vp_ref[0]
    col = lax.broadcasted_iota(jnp.int32, (GQA * BLOCK, 2 * BLOCK), 1)
    exists = (j > 0) | (col >= BLOCK)
    outs = []
    for kv in range(N_KV_HEADS):
        sl = slice(kv * HEAD_DIM, (kv + 1) * HEAD_DIM)
        qg = jnp.concatenate(
            [q[:, (kv * GQA + g) * HEAD_DIM:(kv * GQA + g + 1) * HEAD_DIM] for g in range(GQA)], axis=0)
        kcat = jnp.concatenate([kp[:, sl], kc[:, sl]], axis=0)
        vcat = jnp.concatenate([vp[:, sl], vc[:, sl]], axis=0)
        s = _mm_nt(qg, kcat) * SCALE
        b = bias_ref[kv]
        s = jnp.where((b > 0.5 * NEG) & exists, s + b, NEG)
        (p,) = _sink_softmax([s], sink_ref[kv])
        o = _mm(p, vcat)
        outs += [o[g * BLOCK:(g + 1) * BLOCK] for g in range(GQA)]
    o_ref[0] = jnp.concatenate(outs, axis=1)


def _attn_prompt(za, bias, sink_rows):
    b, t, _ = za.shape
    nb = t // BLOCK
    kcol, vcol = ATT_W // KV_W, ATT_W // KV_W + 1
    return pl.pallas_call(
        _attn_prompt_body,
        grid=(b, nb),
        in_specs=[pl.BlockSpec((1, BLOCK, ATT_W), lambda i, j: (i, j, 0)),
                  pl.BlockSpec((1, BLOCK, KV_W), lambda i, j: (i, j, kcol)),
                  pl.BlockSpec((1, BLOCK, KV_W), lambda i, j: (i, j, vcol)),
                  pl.BlockSpec((1, BLOCK, KV_W), lambda i, j: (i, jnp.maximum(j - 1, 0), kcol)),
                  pl.BlockSpec((1, BLOCK, KV_W), lambda i, j: (i, jnp.maximum(j - 1, 0), vcol)),
                  pl.BlockSpec(bias.shape, lambda i, j: (0, 0, 0)),
                  pl.BlockSpec(sink_rows.shape, lambda i, j: (0, 0, 0))],
        out_specs=pl.BlockSpec((1, BLOCK, ATT_W), lambda i, j: (i, j, 0)),
        out_shape=jax.ShapeDtypeStruct((b, t, ATT_W), F32),
        compiler_params=_params(2),
        name="attn_prompt",
    )(za, za, za, za, za, bias, sink_rows)


def _attn_sample_body(z_ref, ck_ref, cv_ref, bias_c_ref, bias_n_ref, sink_ref,
                      o_ref, kb_ref, vb_ref, *, n_new):
    z = z_ref[0]
    q = z[:, :ATT_W]
    kn = z[:, ATT_W:ATT_W + KV_W]
    vn = z[:, ATT_W + KV_W:]
    ck, cv = ck_ref[0], cv_ref[0]
    outs = []
    for kv in range(N_KV_HEADS):
        sl = slice(kv * HEAD_DIM, (kv + 1) * HEAD_DIM)
        qg = jnp.concatenate(
            [q[:, (kv * GQA + g) * HEAD_DIM:(kv * GQA + g + 1) * HEAD_DIM] for g in range(GQA)], axis=0)
        bc, bn = bias_c_ref[kv], bias_n_ref[kv]
        sc = _mm_nt(qg, ck[:, sl]) * SCALE
        sn = _mm_nt(qg, kn[:, sl]) * SCALE
        sc = jnp.where(bc > 0.5 * NEG, sc + bc, NEG)
        sn = jnp.where(bn > 0.5 * NEG, sn + bn, NEG)
        pc, pn = _sink_softmax([sc, sn], sink_ref[kv])
        o = _mm(pc, cv[:, sl]) + _mm(pn, vn[:, sl])
        outs += [o[g * SAMPLE_PAD:(g + 1) * SAMPLE_PAD] for g in range(GQA)]
    o_ref[0] = jnp.concatenate(outs, axis=1)
    keep = WINDOW - n_new
    kb_ref[0, 0:keep, :] = ck_ref[0, n_new:WINDOW, :]
    kb_ref[0, keep:WINDOW, :] = kn[0:n_new]
    vb_ref[0, 0:keep, :] = cv_ref[0, n_new:WINDOW, :]
    vb_ref[0, keep:WINDOW, :] = vn[0:n_new]


def _attn_sample(za_pad, ck, cv, bias_c, bias_n, sink_rows, *, n_new):
    db = za_pad.shape[0]
    full = lambda a: pl.BlockSpec(a.shape, lambda i: (0,) * a.ndim)
    return pl.pallas_call(
        functools.partial(_attn_sample_body, n_new=n_new),
        grid=(db,),
        in_specs=[pl.BlockSpec((1, SAMPLE_PAD, ATT_IN_W), lambda i: (i, 0, 0)),
                  pl.BlockSpec((1, WINDOW, KV_W), lambda i: (i, 0, 0)),
                  pl.BlockSpec((1, WINDOW, KV_W), lambda i: (i, 0, 0)),
                  full(bias_c), full(bias_n), full(sink_rows)],
        out_specs=[pl.BlockSpec((1, SAMPLE_PAD, ATT_W), lambda i: (i, 0, 0)),
                   pl.BlockSpec((1, WINDOW, KV_W), lambda i: (i, 0, 0)),
                   pl.BlockSpec((1, WINDOW, KV_W), lambda i: (i, 0, 0))],
        out_shape=[jax.ShapeDtypeStruct((db, SAMPLE_PAD, ATT_W), F32),
                   jax.ShapeDtypeStruct((db, WINDOW, KV_W), F32),
                   jax.ShapeDtypeStruct((db, WINDOW, KV_W), F32)],
        compiler_params=_params(1),
        name="attn_sample",
    )(za_pad, ck, cv, bias_c, bias_n, sink_rows)


GROUP_HEADS = 4
GROUP_W = GROUP_HEADS * HEAD_DIM
N_GROUPS = RW_HEADS // GROUP_HEADS


def _split3(x):
    hi = x.astype(BF16)
    r1 = x - hi.astype(F32)
    mid = r1.astype(BF16)
    lo = (r1 - mid.astype(F32)).astype(BF16)
    return hi, mid, lo


def _head_sums(x, ones_blk):
    m = x.shape[0]
    hi = x.astype(BF16)
    lo = (x - hi.astype(F32)).astype(BF16)
    y = jnp.dot(jnp.concatenate([hi, lo], axis=0), ones_blk, preferred_element_type=F32)
    return y[:m] + y[m:]


def _expand(x, blk):
    xb = x.astype(BF16)
    return jnp.concatenate([xb] * GROUP_HEADS, axis=0) * blk


def _expand_state(s, g):
    zero = jnp.zeros((HEAD_DIM, HEAD_DIM), F32)
    rows = []
    for h in range(GROUP_HEADS):
        rows.append(jnp.concatenate(
            [s[g * GROUP_HEADS + h] if j == h else zero for j in range(GROUP_HEADS)], axis=1))
    return jnp.concatenate(rows, axis=0)


def _rwkv_body(z_ref, zpf_ref, s0_ref, mu_ref, vec_ref, w2_ref, a2_ref, g2_ref,
               blk_ref, strict_ref, incl_ref, ones_ref,
               o_ref, sout_ref, state, zbuf, *, chunk, n_valid, solve_steps, nb):
    C = chunk
    R = GROUP_HEADS * C
    c = pl.program_id(1)

    @pl.when(c == 0)
    def _():
        for b in range(nb):
            s0 = s0_ref[b]
            for g in range(N_GROUPS):
                state[b, g] = _expand_state(s0, g)
            zbuf[b, 7:8, :] = zpf_ref[b]

    @pl.when(c > 0)
    def _():
        for b in range(nb):
            zbuf[b, 7:8, :] = zbuf[b, C + 7:C + 8, :]

    w0, a0, k_k, k_a = vec_ref[0:1, :], vec_ref[1:2, :], vec_ref[2:3, :], vec_ref[3:4, :]
    r_k, gn_w, gn_b = vec_ref[4:5, :], vec_ref[5:6, :], vec_ref[6:7, :]
    blk = blk_ref[...]
    ones_blk = ones_ref[...]
    ri = lax.broadcasted_iota(jnp.int32, (C, C), 0)
    ci = lax.broadcasted_iota(jnp.int32, (C, C), 1)
    tri = (ri >= ci).astype(BF16)

    tails = []
    chains = []
    for b in range(nb):
        zc = z_ref[b]
        zbuf[b, 8:8 + C, :] = zc
        zp = zbuf[b, 7:7 + C, :]
        zm = zc + (zp - zc) * mu_ref[...]
        r = zm[:, 0:RW_W]
        k = zm[:, RW_W:2 * RW_W]
        v = zm[:, 2 * RW_W:3 * RW_W]
        o1 = 3 * RW_W
        zw = zm[:, o1:o1 + DECAY_LORA]
        za = zm[:, o1 + DECAY_LORA:o1 + DECAY_LORA + ICLR_LORA]
        zg = zm[:, o1 + DECAY_LORA + ICLR_LORA:]

        y = -(w0 + _mm(jnp.tanh(zw), w2_ref[...]))
        softplus = jnp.maximum(y, 0.0) + jnp.log1p(jnp.exp(-jnp.abs(y)))
        logdecay = -jnp.exp(-softplus - 0.5)
        a = jax.nn.sigmoid(a0 + _mm(za, a2_ref[...]))
        g = _mm(jax.nn.sigmoid(zg), g2_ref[...])
        kk_raw = k * k_k
        kmod = k * (1.0 + (a - 1.0) * k_a)
        vv = v
        if n_valid < C:
            rows = lax.broadcasted_iota(jnp.int32, (C, 1), 0) < n_valid
            logdecay = jnp.where(rows, logdecay, 0.0)
            kk_raw = jnp.where(rows, kk_raw, 0.0)
            kmod = jnp.where(rows, kmod, 0.0)
            vv = jnp.where(rows, v, 0.0)

        sums = _head_sums(jnp.concatenate([kk_raw * kk_raw, r * kmod * r_k], axis=0), ones_blk)
        kk = kk_raw / jnp.maximum(jnp.sqrt(sums[:C]), 1e-12)
        bonus = sums[C:] * v

        hi, mid, lo = _split3(logdecay)
        cum = jnp.dot(tri, jnp.concatenate([hi, mid, lo], axis=1), preferred_element_type=F32)
        cum = cum[:, :RW_W] + cum[:, RW_W:2 * RW_W] + cum[:, 2 * RW_W:]
        gam = jnp.exp(cum)
        gam_inv = jnp.exp(-cum)
        At = -kk * jnp.exp(cum - logdecay)
        Bt = kk * a * gam_inv
        Kt = kmod * gam_inv
        Rt = r * gam
        tails.append((bonus, g))
        for grp in range(N_GROUPS):
            sl = slice(grp * GROUP_W, (grp + 1) * GROUP_W)
            chains.append(dict(
                b=b, grp=grp,
                AR=jnp.concatenate([_expand(At[:, sl], blk), _expand(Rt[:, sl], blk)], axis=0),
                BK=jnp.concatenate([_expand(Bt[:, sl], blk), _expand(Kt[:, sl], blk)], axis=0),
                V=_expand(vv[:, sl], blk),
                gam_last=gam[C - 1:C, sl]))

    strict = strict_ref[...]
    incl = incl_ref[...]
    for ch in chains:
        G = _mm_nt(ch["AR"], ch["BK"])
        ch["P"] = (G[:R, :R] * strict).astype(BF16)
        ch["Lak"] = (G[:R, R:] * strict).astype(BF16)
        ch["M"] = jnp.concatenate([G[R:, :R] * incl, G[R:, R:] * incl], axis=1).astype(BF16)
        ch["S"] = state[ch["b"], ch["grp"]]
        ch["P0"] = _mm_nt(ch["AR"], ch["S"])
    for ch in chains:
        ch["X"] = ch["P0"][:R] + jnp.dot(ch["Lak"], ch["V"], preferred_element_type=F32)
    for i in range(solve_steps):
        for ch in chains:
            ch["X"] = ch["X"] + jnp.dot(ch["P"], ch["X"].astype(BF16), preferred_element_type=F32)
            if i + 1 < solve_steps:
                ch["P"] = jnp.dot(ch["P"], ch["P"], preferred_element_type=F32).astype(BF16)
    outs = [[None] * N_GROUPS for _ in range(nb)]
    for ch in chains:
        UV = jnp.concatenate([ch["X"].astype(BF16), ch["V"]], axis=0)
        O = ch["P0"][R:] + jnp.dot(ch["M"], UV, preferred_element_type=F32)
        state[ch["b"], ch["grp"]] = (ch["S"] + _mm_tn(UV, ch["BK"])) * ch["gam_last"]
        acc = O[0:C]
        for h in range(1, GROUP_HEADS):
            acc = acc + O[h * C:(h + 1) * C]
        outs[ch["b"]][ch["grp"]] = acc

    for b in range(nb):
        bonus, g = tails[b]
        O = jnp.concatenate(outs[b], axis=1)
        mean = _head_sums(O, ones_blk) * (1.0 / HEAD_DIM)
        d = O - mean
        var = _head_sums(d * d, ones_blk) * (1.0 / HEAD_DIM)
        on = d * lax.rsqrt(var + GN_EPS) * gn_w + gn_b
        o_ref[b] = (on + bonus) * g

    @pl.when(c == pl.num_programs(1) - 1)
    def _():
        for b in range(nb):
            for g in range(N_GROUPS):
                s = state[b, g]
                for h in range(GROUP_HEADS):
                    sout_ref[b, g * GROUP_HEADS + h] = s[h * HEAD_DIM:(h + 1) * HEAD_DIM,
                                                         h * HEAD_DIM:(h + 1) * HEAD_DIM]


def _rwkv_masks(chunk):
    R = GROUP_HEADS * chunk
    row_h, row_t = np.arange(R) // chunk, np.arange(R) % chunk
    lane_h = np.arange(GROUP_W) // HEAD_DIM
    blk = (row_h[:, None] == lane_h[None, :])
    same = row_h[:, None] == row_h[None, :]
    strict = same & (row_t[:, None] > row_t[None, :])
    incl = same & (row_t[:, None] >= row_t[None, :])
    seg = np.arange(RW_W) // HEAD_DIM
    ones_blk = seg[:, None] == seg[None, :]
    return (jnp.asarray(blk, BF16), jnp.asarray(strict, F32), jnp.asarray(incl, F32),
            jnp.asarray(ones_blk, BF16))


def _rwkv(zr, zp_first, s0, mu, vecs, w2, a2, g2, *, chunk, n_valid, nb):
    b, t, _ = zr.shape
    solve_steps = max(1, (n_valid - 1).bit_length())
    masks = _rwkv_masks(chunk)
    full = lambda a: pl.BlockSpec(a.shape, lambda i, j: (0,) * a.ndim)
    return pl.pallas_call(
        functools.partial(_rwkv_body, chunk=chunk, n_valid=n_valid, solve_steps=solve_steps, nb=nb),
        grid=(b // nb, t // chunk),
        in_specs=[pl.BlockSpec((nb, chunk, RW_IN_W), lambda i, j: (i, j, 0)),
                  pl.BlockSpec((nb, 1, RW_IN_W), lambda i, j: (i, 0, 0)),
                  pl.BlockSpec((nb, RW_HEADS, HEAD_DIM, HEAD_DIM), lambda i, j: (i, 0, 0, 0)),
                  full(mu), full(vecs), full(w2), full(a2), full(g2)] + [full(m) for m in masks],
        out_specs=[pl.BlockSpec((nb, chunk, RW_W), lambda i, j: (i, j, 0)),
                   pl.BlockSpec((nb, RW_HEADS, HEAD_DIM, HEAD_DIM), lambda i, j: (i, 0, 0, 0))],
        out_shape=[jax.ShapeDtypeStruct((b, t, RW_W), F32),
                   jax.ShapeDtypeStruct((b, RW_HEADS, HEAD_DIM, HEAD_DIM), F32)],
        scratch_shapes=[pltpu.VMEM((nb, N_GROUPS, GROUP_W, GROUP_W), F32),
                        pltpu.VMEM((nb, chunk + 8, RW_IN_W), F32)],
        compiler_params=_params(2),
        name="rwkv_chunk%d" % chunk,
    )(zr, zp_first, s0, mu, vecs, w2, a2, g2, *masks)


def _outffn_body(x_ref, att_ref, rw_ref, wo_ref, n2_ref, wg_ref, wu_ref, wd_ref, nf_ref, xo_ref, *, final):
    mix = jnp.concatenate([att_ref[...], rw_ref[...]], axis=-1).astype(BF16)
    x1 = x_ref[...] + jnp.dot(mix, wo_ref[...], preferred_element_type=F32)
    hb = _rmsnorm(x1, n2_ref[...]).astype(BF16)
    gate = jnp.dot(hb, wg_ref[...], preferred_element_type=F32)
    up = jnp.dot(hb, wu_ref[...], preferred_element_type=F32)
    act = (gate * jax.nn.sigmoid(gate) * up).astype(BF16)
    x2 = x1 + jnp.dot(act, wd_ref[...], preferred_element_type=F32)
    xo_ref[...] = _rmsnorm(x2, nf_ref[...]) if final else x2


def _outffn(x, att, rw, wo, n2, wg, wu, wd, nf, *, final, tm):
    n = x.shape[0]
    resident = lambda a: pl.BlockSpec(a.shape, lambda i: (0,) * a.ndim, pipeline_mode=pl.Buffered(1))
    return pl.pallas_call(
        functools.partial(_outffn_body, final=final),
        grid=(n // tm,),
        in_specs=[pl.BlockSpec((tm, D_MODEL), lambda i: (i, 0)),
                  pl.BlockSpec((tm, ATT_W), lambda i: (i, 0)),
                  pl.BlockSpec((tm, RW_W), lambda i: (i, 0)),
                  resident(wo), resident(n2), resident(wg), resident(wu), resident(wd), resident(nf)],
        out_specs=pl.BlockSpec((tm, D_MODEL), lambda i: (i, 0)),
        out_shape=jax.ShapeDtypeStruct((n, D_MODEL), F32),
        compiler_params=_params(1),
        name="outffn_final" if final else "outffn",
    )(x, att, rw, wo, n2, wg, wu, wd, nf)


def _t5_bucket_np(d):
    max_exact = NUM_BUCKETS // 2
    n = np.maximum(d, 0)
    large = max_exact + (np.log(np.maximum(n, 1) / max_exact) / np.log(MAX_DISTANCE / max_exact)
                         * (NUM_BUCKETS - max_exact)).astype(np.int32)
    large = np.minimum(large, NUM_BUCKETS - 1)
    return np.where(n < max_exact, n, large).astype(np.int32)


def _masked_bias(rel_table, d, inside, rows_per_head):
    b = rel_table[_t5_bucket_np(d)].astype(F32)
    b = jnp.where(jnp.asarray(inside)[:, :, None], b, NEG)
    b = jnp.transpose(b, (2, 0, 1))
    q, s = d.shape
    if rows_per_head > q:
        b = jnp.concatenate([b, jnp.full((N_Q_HEADS, rows_per_head - q, s), NEG, F32)], axis=1)
    return b.reshape(N_KV_HEADS, GQA * rows_per_head, s)


def _sink_rows(sink_l, rows_per_head):
    return jnp.repeat(sink_l.astype(F32).reshape(N_KV_HEADS, GQA, 1), rows_per_head, axis=1).reshape(
        N_KV_HEADS, GQA * rows_per_head, 1)


def kernel(x_prompt, x_sample, cache_k, cache_v, state_shift, state_wkv, norm1, w_in, w_out, sink,
           rel_table, mu, w0, w2, a0, a2, g2, k_k, k_a, r_k, gn_w, gn_b, norm2, w_gate, w_up,
           w_down, norm_f):
    depth = w_in.shape[0]
    B, T, _ = x_prompt.shape
    DB, S, _ = x_sample.shape
    wc = cache_k.shape[2]

    qi = np.arange(BLOCK)[:, None]
    sj = np.arange(2 * BLOCK)[None, :]
    dp = qi + BLOCK - sj
    bias_p = _masked_bias(rel_table, dp, (dp >= 0) & (dp <= WINDOW), BLOCK)
    ds_full = (wc + np.arange(S))[:, None] - np.arange(wc + S)[None, :]
    inside_s = (ds_full >= 0) & (ds_full <= WINDOW)
    bias_c = _masked_bias(rel_table, ds_full[:, :wc], inside_s[:, :wc], SAMPLE_PAD)
    dn = np.full((S, SAMPLE_PAD), -1, np.int64)
    dn[:, :S] = ds_full[:, wc:]
    inn = np.zeros((S, SAMPLE_PAD), bool)
    inn[:, :S] = inside_s[:, wc:]
    bias_n = _masked_bias(rel_table, dn, inn, SAMPLE_PAD)

    xp = x_prompt.reshape(B * T, D_MODEL)
    xs = x_sample.reshape(DB * S, D_MODEL)
    zeros_shift = jnp.zeros((B, 1, RW_IN_W), F32)
    zeros_state = jnp.zeros((B, RW_HEADS, HEAD_DIM, HEAD_DIM), F32)
    nf = norm_f.reshape(1, D_MODEL)
    ones = jnp.ones((1, D_MODEL), F32)

    kp, vp, sp, wp, ksl, vsl, ssl, wsl = [], [], [], [], [], [], [], []
    for l in range(depth):
        w_in_b = w_in[l].astype(BF16)
        wo_b, wg_b, wu_b, wd_b = (w_out[l].astype(BF16), w_gate[l].astype(BF16),
                                  w_up[l].astype(BF16), w_down[l].astype(BF16))
        n1 = norm1[l].reshape(1, D_MODEL)
        n2 = norm2[l].reshape(1, D_MODEL)
        mu_l = mu[l].reshape(1, RW_IN_W)
        vecs = jnp.stack([w0[l], a0[l], k_k[l], k_a[l], r_k[l], gn_w[l], gn_b[l], jnp.zeros_like(w0[l])])
        w2_b, a2_b, g2_b = w2[l].astype(BF16), a2[l].astype(BF16), g2[l].astype(BF16)
        final = l == depth - 1

        za, zr, h = _inproj(xp, n1, w_in_b, norm=True, tm=512)
        za3 = za.reshape(B, T, ATT_IN_W)
        w = min(WINDOW, T)
        kp.append(za3[:, T - w:, ATT_W:ATT_W + KV_W].reshape(B, w, N_KV_HEADS, HEAD_DIM))
        vp.append(za3[:, T - w:, ATT_W + KV_W:].reshape(B, w, N_KV_HEADS, HEAD_DIM))
        sp.append(h.reshape(B, T, D_MODEL)[:, -1])
        att = _attn_prompt(za3, bias_p, _sink_rows(sink[l], BLOCK))
        rw, wkv = _rwkv(zr.reshape(B, T, RW_IN_W), zeros_shift, zeros_state, mu_l, vecs, w2_b, a2_b, g2_b,
                        chunk=PROMPT_CHUNK, n_valid=PROMPT_CHUNK, nb=2)
        wp.append(wkv)
        xp = _outffn(xp, att.reshape(B * T, ATT_W), rw.reshape(B * T, RW_W), wo_b, n2, wg_b, wu_b, wd_b, nf,
                     final=final, tm=256)

        za_s, zr_s, h_s = _inproj(xs, n1, w_in_b, norm=True, tm=DB * S)
        _, zr_prev, _ = _inproj(state_shift[l], ones, w_in_b, norm=False, tm=DB)
        ssl.append(h_s.reshape(DB, S, D_MODEL)[:, -1])
        za_pad = jnp.pad(za_s.reshape(DB, S, ATT_IN_W), ((0, 0), (0, SAMPLE_PAD - S), (0, 0)))
        att_s, kb, vb = _attn_sample(za_pad, cache_k[l].reshape(DB, wc, KV_W), cache_v[l].reshape(DB, wc, KV_W),
                                     bias_c, bias_n, _sink_rows(sink[l], SAMPLE_PAD), n_new=S)
        ksl.append(kb.reshape(DB, wc, N_KV_HEADS, HEAD_DIM))
        vsl.append(vb.reshape(DB, wc, N_KV_HEADS, HEAD_DIM))
        zr_pad = jnp.pad(zr_s.reshape(DB, S, RW_IN_W), ((0, 0), (0, SAMPLE_CHUNK - S), (0, 0)))
        rw_s, wkv_s = _rwkv(zr_pad, zr_prev.reshape(DB, 1, RW_IN_W), state_wkv[l], mu_l, vecs, w2_b, a2_b, g2_b,
                            chunk=SAMPLE_CHUNK, n_valid=S, nb=2)
        wsl.append(wkv_s)
        xs = _outffn(xs, att_s[:, :S].reshape(DB * S, ATT_W), rw_s[:, :S].reshape(DB * S, RW_W),
                     wo_b, n2, wg_b, wu_b, wd_b, nf, final=final, tm=DB * S)

    y_prompt = xp.reshape(B, T, D_MODEL)
    y_sample = xs.reshape(DB, S, D_MODEL)
    return (y_prompt, y_sample, jnp.stack(kp), jnp.stack(vp), jnp.stack(sp), jnp.stack(wp),
            jnp.stack(ksl), jnp.stack(vsl), jnp.stack(ssl), jnp.stack(wsl))
```

```python
import functools

import numpy as np
import jax
import jax.numpy as jnp
from jax import lax
from jax.experimental import pallas as pl
from jax.experimental.pallas import tpu as pltpu

D_MODEL = 1024
HEAD_DIM = 64
N_Q_HEADS = 8
N_KV_HEADS = 2
GQA = N_Q_HEADS // N_KV_HEADS
ATT_W = N_Q_HEADS * HEAD_DIM
KV_W = N_KV_HEADS * HEAD_DIM
WINDOW = 128
BLOCK = 128
NUM_BUCKETS = 32
MAX_DISTANCE = 128
RW_HEADS = 8
RW_W = RW_HEADS * HEAD_DIM
DECAY_LORA = 64
ICLR_LORA = 64
GATE_LORA = 128
RW_IN_W = 3 * RW_W + DECAY_LORA + ICLR_LORA + GATE_LORA
ATT_IN_W = ATT_W + 2 * KV_W
NORM_EPS = 1e-6
GN_EPS = 64e-5
NEG = -1e30
SCALE = HEAD_DIM ** -0.5

V7X_VMEM_LIMIT_BYTES = 56 * 1024 * 1024

PROMPT_CHUNK = 64
SAMPLE_CHUNK = 16
SAMPLE_PAD = 8

BF16 = jnp.bfloat16
F32 = jnp.float32


def _mm(a, b):
    return jnp.dot(a.astype(BF16), b.astype(BF16), preferred_element_type=F32)


def _mm_nt(a, b):
    return lax.dot_general(a.astype(BF16), b.astype(BF16), (((1,), (1,)), ((), ())),
                           preferred_element_type=F32)


def _mm_tn(a, b):
    return lax.dot_general(a.astype(BF16), b.astype(BF16), (((0,), (0,)), ((), ())),
                           preferred_element_type=F32)


def _rmsnorm(x, g):
    return x * lax.rsqrt(jnp.mean(x * x, -1, keepdims=True) + NORM_EPS) * g


def _params(n_axes):
    return pltpu.CompilerParams(dimension_semantics=("arbitrary",) * n_axes,
                                vmem_limit_bytes=V7X_VMEM_LIMIT_BYTES)


def _inproj_body(x_ref, g_ref, w_ref, za_ref, zr_ref, h_ref, *, norm):
    x = x_ref[...]
    h = _rmsnorm(x, g_ref[...]) if norm else x
    h_ref[...] = h
    z = jnp.dot(h.astype(BF16), w_ref[...], preferred_element_type=F32)
    za_ref[...] = z[:, :ATT_IN_W]
    zr_ref[...] = z[:, ATT_IN_W:]


def _inproj(x, g, w, *, norm, tm):
    n = x.shape[0]
    in_w = w.shape[1]
    return pl.pallas_call(
        functools.partial(_inproj_body, norm=norm),
        grid=(n // tm,),
        in_specs=[pl.BlockSpec((tm, D_MODEL), lambda i: (i, 0)),
                  pl.BlockSpec((1, D_MODEL), lambda i: (0, 0)),
                  pl.BlockSpec((D_MODEL, in_w), lambda i: (0, 0))],
        out_specs=[pl.BlockSpec((tm, ATT_IN_W), lambda i: (i, 0)),
                   pl.BlockSpec((tm, RW_IN_W), lambda i: (i, 0)),
                   pl.BlockSpec((tm, D_MODEL), lambda i: (i, 0))],
        out_shape=[jax.ShapeDtypeStruct((n, ATT_IN_W), F32),
                   jax.ShapeDtypeStruct((n, RW_IN_W), F32),
                   jax.ShapeDtypeStruct((n, D_MODEL), F32)],
        compiler_params=_params(1),
        name="inproj_norm" if norm else "inproj_plain",
    )(x, g, w)


def _sink_softmax(score_parts, sink_col):
    m = sink_col
    for s in score_parts:
        m = jnp.maximum(m, jnp.max(s, -1, keepdims=True))
    es = [jnp.exp(s - m) for s in score_parts]
    den = jnp.exp(sink_col - m)
    for e in es:
        den = den + jnp.sum(e, -1, keepdims=True)
    return [e / den for e in es]


QG_W = GQA * HEAD_DIM


def _tile_kv_heads(x):
    rot = pltpu.roll(x, HEAD_DIM, 1)
    first = lax.broadcasted_iota(jnp.int32, x.shape, 1) < HEAD_DIM
    halves = (jnp.where(first, x, rot).astype(BF16), jnp.where(first, rot, x).astype(BF16))
    return [jnp.concatenate([h] * (QG_W // KV_W), axis=1) for h in halves]


def _attn_prompt_body(q_ref, kv_ref, bias_ref, sink_ref, blk_ref, o_ref, kt_prev, vT_prev, *, nblk):
    j = pl.program_id(1)

    @pl.when(j == 0)
    def _():
        kt_prev[...] = jnp.zeros_like(kt_prev)
        vT_prev[...] = jnp.zeros_like(vT_prev)

    blk = blk_ref[...]
    kts = [[kt_prev[kv] for kv in range(N_KV_HEADS)]]
    vTs = [vT_prev[...]]
    for i in range(nblk):
        rows = slice(i * BLOCK, (i + 1) * BLOCK)
        kts.append(_tile_kv_heads(kv_ref[0, rows, 0:KV_W]))
        vTs.append(kv_ref[0, rows, KV_W:2 * KV_W].T.astype(BF16))
    for kv in range(N_KV_HEADS):
        kt_prev[kv] = kts[nblk][kv]
    vT_prev[...] = vTs[nblk]

    units = []
    for i in range(nblk):
        qb = (q_ref[0, i * BLOCK:(i + 1) * BLOCK, :] * SCALE).astype(BF16)
        first = jnp.where(j == 0, 0, 1) if i == 0 else 1
        for kv in range(N_KV_HEADS):
            qe = jnp.concatenate([qb[:, kv * QG_W:(kv + 1) * QG_W]] * GQA, axis=0) * blk
            kcat = jnp.concatenate([kts[i][kv], kts[i + 1][kv]], axis=0)
            units.append(dict(i=i, kv=kv, s=_mm_nt(kcat, qe), first=first))
    for u in units:
        b = bias_ref[u["first"], u["kv"]]
        s = jnp.where(b > 0.5 * NEG, u["s"] + b, NEG)
        sink_row = sink_ref[u["kv"]]
        m = jnp.maximum(jnp.max(s, 0, keepdims=True), sink_row)
        e = jnp.exp(s - m)
        den = jnp.sum(e, 0, keepdims=True) + jnp.exp(sink_row - m)
        u["p"] = (e * (1.0 / den)).astype(BF16)
    pieces = [[None] * N_Q_HEADS for _ in range(nblk)]
    for u in units:
        i, kv = u["i"], u["kv"]
        hd = slice(kv * HEAD_DIM, (kv + 1) * HEAD_DIM)
        vcatT = jnp.concatenate([vTs[i][hd], vTs[i + 1][hd]], axis=1)
        oT = jnp.dot(vcatT, u["p"], preferred_element_type=F32)
        for g in range(GQA):
            pieces[i][kv * GQA + g] = oT[:, g * BLOCK:(g + 1) * BLOCK]
    for i in range(nblk):
        o_ref[0, i * BLOCK:(i + 1) * BLOCK, :] = jnp.concatenate(pieces[i], axis=0).T


def _attn_prompt(za, bias, sink_rows, *, nblk):
    b, t, _ = za.shape
    rows = nblk * BLOCK
    blk = np.arange(GQA * BLOCK)[:, None] // BLOCK == np.arange(QG_W)[None, :] // HEAD_DIM
    blk = jnp.asarray(blk, BF16)
    full = lambda a: pl.BlockSpec(a.shape, lambda i, j: (0,) * a.ndim)
    return pl.pallas_call(
        functools.partial(_attn_prompt_body, nblk=nblk),
        grid=(b, t // rows),
        in_specs=[pl.BlockSpec((1, rows, ATT_W), lambda i, j: (i, j, 0)),
                  pl.BlockSpec((1, rows, 2 * KV_W), lambda i, j: (i, j, ATT_W // (2 * KV_W))),
                  full(bias), full(sink_rows), full(blk)],
        out_specs=pl.BlockSpec((1, rows, ATT_W), lambda i, j: (i, j, 0)),
        out_shape=jax.ShapeDtypeStruct((b, t, ATT_W), F32),
        scratch_shapes=[pltpu.VMEM((N_KV_HEADS, BLOCK, QG_W), BF16),
                        pltpu.VMEM((KV_W, BLOCK), BF16)],
        compiler_params=_params(2),
        name="attn_prompt",
    )(za, za, bias, sink_rows, blk)


def _attn_sample_body(z_ref, ck_ref, cv_ref, bias_c_ref, bias_n_ref, sink_ref,
                      o_ref, kb_ref, vb_ref, *, n_new):
    z = z_ref[0]
    q = z[:, :ATT_W]
    kn = z[:, ATT_W:ATT_W + KV_W]
    vn = z[:, ATT_W + KV_W:]
    ck, cv = ck_ref[0], cv_ref[0]
    outs = []
    for kv in range(N_KV_HEADS):
        sl = slice(kv * HEAD_DIM, (kv + 1) * HEAD_DIM)
        qg = jnp.concatenate(
            [q[:, (kv * GQA + g) * HEAD_DIM:(kv * GQA + g + 1) * HEAD_DIM] for g in range(GQA)], axis=0)
        bc, bn = bias_c_ref[kv], bias_n_ref[kv]
        sc = _mm_nt(qg, ck[:, sl]) * SCALE
        sn = _mm_nt(qg, kn[:, sl]) * SCALE
        sc = jnp.where(bc > 0.5 * NEG, sc + bc, NEG)
        sn = jnp.where(bn > 0.5 * NEG, sn + bn, NEG)
        pc, pn = _sink_softmax([sc, sn], sink_ref[kv])
        o = _mm(pc, cv[:, sl]) + _mm(pn, vn[:, sl])
        outs += [o[g * SAMPLE_PAD:(g + 1) * SAMPLE_PAD] for g in range(GQA)]
    o_ref[0] = jnp.concatenate(outs, axis=1)
    keep = WINDOW - n_new
    kb_ref[0, 0:keep, :] = ck_ref[0, n_new:WINDOW, :]
    kb_ref[0, keep:WINDOW, :] = kn[0:n_new]
    vb_ref[0, 0:keep, :] = cv_ref[0, n_new:WINDOW, :]
    vb_ref[0, keep:WINDOW, :] = vn[0:n_new]


def _attn_sample(za_pad, ck, cv, bias_c, bias_n, sink_rows, *, n_new):
    db = za_pad.shape[0]
    full = lambda a: pl.BlockSpec(a.shape, lambda i: (0,) * a.ndim)
    return pl.pallas_call(
        functools.partial(_attn_sample_body, n_new=n_new),
        grid=(db,),
        in_specs=[pl.BlockSpec((1, SAMPLE_PAD, ATT_IN_W), lambda i: (i, 0, 0)),
                  pl.BlockSpec((1, WINDOW, KV_W), lambda i: (i, 0, 0)),
                  pl.BlockSpec((1, WINDOW, KV_W), lambda i: (i, 0, 0)),
                  full(bias_c), full(bias_n), full(sink_rows)],
        out_specs=[pl.BlockSpec((1, SAMPLE_PAD, ATT_W), lambda i: (i, 0, 0)),
                   pl.BlockSpec((1, WINDOW, KV_W), lambda i: (i, 0, 0)),
                   pl.BlockSpec((1, WINDOW, KV_W), lambda i: (i, 0, 0))],
        out_shape=[jax.ShapeDtypeStruct((db, SAMPLE_PAD, ATT_W), F32),
                   jax.ShapeDtypeStruct((db, WINDOW, KV_W), F32),
                   jax.ShapeDtypeStruct((db, WINDOW, KV_W), F32)],
        compiler_params=_params(1),
        name="attn_sample",
    )(za_pad, ck, cv, bias_c, bias_n, sink_rows)


GROUP_HEADS = 4
GROUP_W = GROUP_HEADS * HEAD_DIM
N_GROUPS = RW_HEADS // GROUP_HEADS


def _split3(x):
    hi = x.astype(BF16)
    r1 = x - hi.astype(F32)
    mid = r1.astype(BF16)
    lo = (r1 - mid.astype(F32)).astype(BF16)
    return hi, mid, lo


def _head_sums(x, ones_blk):
    m = x.shape[0]
    hi = x.astype(BF16)
    lo = (x - hi.astype(F32)).astype(BF16)
    y = jnp.dot(jnp.concatenate([hi, lo], axis=0), ones_blk, preferred_element_type=F32)
    return y[:m] + y[m:]


def _expand(x, blk):
    xb = x.astype(BF16)
    return jnp.concatenate([xb] * GROUP_HEADS, axis=0) * blk


def _expand_state(s, g):
    zero = jnp.zeros((HEAD_DIM, HEAD_DIM), F32)
    rows = []
    for h in range(GROUP_HEADS):
        rows.append(jnp.concatenate(
            [s[g * GROUP_HEADS + h] if j == h else zero for j in range(GROUP_HEADS)], axis=1))
    return jnp.concatenate(rows, axis=0)


def _rwkv_body(z_ref, zpf_ref, s0_ref, mu_ref, vec_ref, w2_ref, a2_ref, g2_ref,
               blk_ref, strict_ref, incl_ref, ones_ref,
               o_ref, sout_ref, state, zbuf, *, chunk, n_valid, solve_steps, nb):
    C = chunk
    R = GROUP_HEADS * C
    c = pl.program_id(1)

    @pl.when(c == 0)
    def _():
        for b in range(nb):
            s0 = s0_ref[b]
            for g in range(N_GROUPS):
                state[b, g] = _expand_state(s0, g)
            zbuf[b, 7:8, :] = zpf_ref[b]

    @pl.when(c > 0)
    def _():
        for b in range(nb):
            zbuf[b, 7:8, :] = zbuf[b, C + 7:C + 8, :]

    w0, a0, k_k, k_a = vec_ref[0:1, :], vec_ref[1:2, :], vec_ref[2:3, :], vec_ref[3:4, :]
    r_k, gn_w, gn_b = vec_ref[4:5, :], vec_ref[5:6, :], vec_ref[6:7, :]
    blk = blk_ref[...]
    ones_blk = ones_ref[...]
    ri = lax.broadcasted_iota(jnp.int32, (C, C), 0)
    ci = lax.broadcasted_iota(jnp.int32, (C, C), 1)
    tri = (ri >= ci).astype(BF16)

    tails = []
    chains = []
    for b in range(nb):
        zc = z_ref[b]
        zbuf[b, 8:8 + C, :] = zc
        zp = zbuf[b, 7:7 + C, :]
        zm = zc + (zp - zc) * mu_ref[...]
        r = zm[:, 0:RW_W]
        k = zm[:, RW_W:2 * RW_W]
        v = zm[:, 2 * RW_W:3 * RW_W]
        o1 = 3 * RW_W
        zw = zm[:, o1:o1 + DECAY_LORA]
        za = zm[:, o1 + DECAY_LORA:o1 + DECAY_LORA + ICLR_LORA]
        zg = zm[:, o1 + DECAY_LORA + ICLR_LORA:]

        y = -(w0 + _mm(jnp.tanh(zw), w2_ref[...]))
        softplus = jnp.maximum(y, 0.0) + jnp.log1p(jnp.exp(-jnp.abs(y)))
        logdecay = -jnp.exp(-softplus - 0.5)
        a = jax.nn.sigmoid(a0 + _mm(za, a2_ref[...]))
        g = _mm(jax.nn.sigmoid(zg), g2_ref[...])
        kk_raw = k * k_k
        kmod = k * (1.0 + (a - 1.0) * k_a)
        vv = v
        if n_valid < C:
            rows = lax.broadcasted_iota(jnp.int32, (C, 1), 0) < n_valid
            logdecay = jnp.where(rows, logdecay, 0.0)
            kk_raw = jnp.where(rows, kk_raw, 0.0)
            kmod = jnp.where(rows, kmod, 0.0)
            vv = jnp.where(rows, v, 0.0)

        sums = _head_sums(jnp.concatenate([kk_raw * kk_raw, r * kmod * r_k], axis=0), ones_blk)
        kk = kk_raw / jnp.maximum(jnp.sqrt(sums[:C]), 1e-12)
        bonus = sums[C:] * v

        hi, mid, lo = _split3(logdecay)
        cum = jnp.dot(tri, jnp.concatenate([hi, mid, lo], axis=1), preferred_element_type=F32)
        cum = cum[:, :RW_W] + cum[:, RW_W:2 * RW_W] + cum[:, 2 * RW_W:]
        gam = jnp.exp(cum)
        gam_inv = jnp.exp(-cum)
        At = -kk * jnp.exp(cum - logdecay)
        Bt = kk * a * gam_inv
        Kt = kmod * gam_inv
        Rt = r * gam
        tails.append((bonus, g))
        for grp in range(N_GROUPS):
            sl = slice(grp * GROUP_W, (grp + 1) * GROUP_W)
            chains.append(dict(
                b=b, grp=grp,
                AR=jnp.concatenate([_expand(At[:, sl], blk), _expand(Rt[:, sl], blk)], axis=0),
                BK=jnp.concatenate([_expand(Bt[:, sl], blk), _expand(Kt[:, sl], blk)], axis=0),
                V=_expand(vv[:, sl], blk),
                gam_last=gam[C - 1:C, sl]))

    strict = strict_ref[...]
    incl = incl_ref[...]
    for ch in chains:
        G = _mm_nt(ch["AR"], ch["BK"])
        ch["P"] = (G[:R, :R] * strict).astype(BF16)
        ch["Lak"] = (G[:R, R:] * strict).astype(BF16)
        ch["M"] = jnp.concatenate([G[R:, :R] * incl, G[R:, R:] * incl], axis=1).astype(BF16)
        ch["S"] = state[ch["b"], ch["grp"]]
        ch["P0"] = _mm_nt(ch["AR"], ch["S"])
    for ch in chains:
        ch["X"] = ch["P0"][:R] + jnp.dot(ch["Lak"], ch["V"], preferred_element_type=F32)
    for i in range(solve_steps):
        for ch in chains:
            ch["X"] = ch["X"] + jnp.dot(ch["P"], ch["X"].astype(BF16), preferred_element_type=F32)
            if i + 1 < solve_steps:
                ch["P"] = jnp.dot(ch["P"], ch["P"], preferred_element_type=F32).astype(BF16)
    outs = [[None] * N_GROUPS for _ in range(nb)]
    for ch in chains:
        UV = jnp.concatenate([ch["X"].astype(BF16), ch["V"]], axis=0)
        O = ch["P0"][R:] + jnp.dot(ch["M"], UV, preferred_element_type=F32)
        state[ch["b"], ch["grp"]] = (ch["S"] + _mm_tn(UV, ch["BK"])) * ch["gam_last"]
        acc = O[0:C]
        for h in range(1, GROUP_HEADS):
            acc = acc + O[h * C:(h + 1) * C]
        outs[ch["b"]][ch["grp"]] = acc

    for b in range(nb):
        bonus, g = tails[b]
        O = jnp.concatenate(outs[b], axis=1)
        mean = _head_sums(O, ones_blk) * (1.0 / HEAD_DIM)
        d = O - mean
        var = _head_sums(d * d, ones_blk) * (1.0 / HEAD_DIM)
        on = d * lax.rsqrt(var + GN_EPS) * gn_w + gn_b
        o_ref[b] = (on + bonus) * g

    @pl.when(c == pl.num_programs(1) - 1)
    def _():
        for b in range(nb):
            for g in range(N_GROUPS):
                s = state[b, g]
                for h in range(GROUP_HEADS):
                    sout_ref[b, g * GROUP_HEADS + h] = s[h * HEAD_DIM:(h + 1) * HEAD_DIM,
                                                         h * HEAD_DIM:(h + 1) * HEAD_DIM]


def _rwkv_masks(chunk):
    R = GROUP_HEADS * chunk
    row_h, row_t = np.arange(R) // chunk, np.arange(R) % chunk
    lane_h = np.arange(GROUP_W) // HEAD_DIM
    blk = (row_h[:, None] == lane_h[None, :])
    same = row_h[:, None] == row_h[None, :]
    strict = same & (row_t[:, None] > row_t[None, :])
    incl = same & (row_t[:, None] >= row_t[None, :])
    seg = np.arange(RW_W) // HEAD_DIM
    ones_blk = seg[:, None] == seg[None, :]
    return (jnp.asarray(blk, BF16), jnp.asarray(strict, F32), jnp.asarray(incl, F32),
            jnp.asarray(ones_blk, BF16))


def _rwkv(zr, zp_first, s0, mu, vecs, w2, a2, g2, *, chunk, n_valid, nb):
    b, t, _ = zr.shape
    solve_steps = max(1, (n_valid - 1).bit_length())
    masks = _rwkv_masks(chunk)
    full = lambda a: pl.BlockSpec(a.shape, lambda i, j: (0,) * a.ndim)
    return pl.pallas_call(
        functools.partial(_rwkv_body, chunk=chunk, n_valid=n_valid, solve_steps=solve_steps, nb=nb),
        grid=(b // nb, t // chunk),
        in_specs=[pl.BlockSpec((nb, chunk, RW_IN_W), lambda i, j: (i, j, 0)),
                  pl.BlockSpec((nb, 1, RW_IN_W), lambda i, j: (i, 0, 0)),
                  pl.BlockSpec((nb, RW_HEADS, HEAD_DIM, HEAD_DIM), lambda i, j: (i, 0, 0, 0)),
                  full(mu), full(vecs), full(w2), full(a2), full(g2)] + [full(m) for m in masks],
        out_specs=[pl.BlockSpec((nb, chunk, RW_W), lambda i, j: (i, j, 0)),
                   pl.BlockSpec((nb, RW_HEADS, HEAD_DIM, HEAD_DIM), lambda i, j: (i, 0, 0, 0))],
        out_shape=[jax.ShapeDtypeStruct((b, t, RW_W), F32),
                   jax.ShapeDtypeStruct((b, RW_HEADS, HEAD_DIM, HEAD_DIM), F32)],
        scratch_shapes=[pltpu.VMEM((nb, N_GROUPS, GROUP_W, GROUP_W), F32),
                        pltpu.VMEM((nb, chunk + 8, RW_IN_W), F32)],
        compiler_params=_params(2),
        name="rwkv_chunk%d" % chunk,
    )(zr, zp_first, s0, mu, vecs, w2, a2, g2, *masks)


def _outffn_body(x_ref, att_ref, rw_ref, wo_ref, n2_ref, wg_ref, wu_ref, wd_ref, nf_ref, xo_ref, *, final):
    mix = jnp.concatenate([att_ref[...], rw_ref[...]], axis=-1).astype(BF16)
    x1 = x_ref[...] + jnp.dot(mix, wo_ref[...], preferred_element_type=F32)
    hb = _rmsnorm(x1, n2_ref[...]).astype(BF16)
    gate = jnp.dot(hb, wg_ref[...], preferred_element_type=F32)
    up = jnp.dot(hb, wu_ref[...], preferred_element_type=F32)
    act = (gate * jax.nn.sigmoid(gate) * up).astype(BF16)
    x2 = x1 + jnp.dot(act, wd_ref[...], preferred_element_type=F32)
    xo_ref[...] = _rmsnorm(x2, nf_ref[...]) if final else x2


def _outffn(x, att, rw, wo, n2, wg, wu, wd, nf, *, final, tm):
    n = x.shape[0]
    resident = lambda a: pl.BlockSpec(a.shape, lambda i: (0,) * a.ndim, pipeline_mode=pl.Buffered(1))
    return pl.pallas_call(
        functools.partial(_outffn_body, final=final),
        grid=(n // tm,),
        in_specs=[pl.BlockSpec((tm, D_MODEL), lambda i: (i, 0)),
                  pl.BlockSpec((tm, ATT_W), lambda i: (i, 0)),
                  pl.BlockSpec((tm, RW_W), lambda i: (i, 0)),
                  resident(wo), resident(n2), resident(wg), resident(wu), resident(wd), resident(nf)],
        out_specs=pl.BlockSpec((tm, D_MODEL), lambda i: (i, 0)),
        out_shape=jax.ShapeDtypeStruct((n, D_MODEL), F32),
        compiler_params=_params(1),
        name="outffn_final" if final else "outffn",
    )(x, att, rw, wo, n2, wg, wu, wd, nf)


def _t5_bucket_np(d):
    max_exact = NUM_BUCKETS // 2
    n = np.maximum(d, 0)
    large = max_exact + (np.log(np.maximum(n, 1) / max_exact) / np.log(MAX_DISTANCE / max_exact)
                         * (NUM_BUCKETS - max_exact)).astype(np.int32)
    large = np.minimum(large, NUM_BUCKETS - 1)
    return np.where(n < max_exact, n, large).astype(np.int32)


def _bias_body(tab_ref, bucket_ref, inside_ref, o_ref, *, keys_major):
    bucket = bucket_ref[...]
    inside = inside_ref[...] > 0
    for h in range(N_Q_HEADS):
        acc = jnp.zeros(bucket.shape, F32)
        for bk in range(NUM_BUCKETS):
            acc = jnp.where(bucket == bk, tab_ref[bk, h], acc)
        acc = jnp.where(inside, acc, NEG)
        if keys_major:
            kv, g, q = h // GQA, h % GQA, bucket.shape[1]
            o_ref[kv, :, g * q:(g + 1) * q] = acc
        else:
            o_ref[h] = acc


def _masked_bias(rel_table, d, inside, rows_per_head, keys_major=False):
    q, s = d.shape
    bucket = np.zeros((rows_per_head, s), np.int32)
    bucket[:q] = _t5_bucket_np(d)
    ins = np.zeros((rows_per_head, s), np.int32)
    ins[:q] = inside
    if keys_major:
        bucket, ins = bucket.T.copy(), ins.T.copy()
        out_shape = (N_KV_HEADS, s, GQA * rows_per_head)
    else:
        out_shape = (N_Q_HEADS, rows_per_head, s)
    vm = lambda shape: pl.BlockSpec(shape, lambda: (0,) * len(shape))
    b = pl.pallas_call(
        functools.partial(_bias_body, keys_major=keys_major),
        in_specs=[pl.BlockSpec(memory_space=pltpu.SMEM), vm(bucket.shape), vm(ins.shape)],
        out_specs=vm(out_shape),
        out_shape=jax.ShapeDtypeStruct(out_shape, F32),
        name="rel_bias",
    )(rel_table.astype(F32), jnp.asarray(bucket), jnp.asarray(ins))
    return b if keys_major else b.reshape(N_KV_HEADS, GQA * rows_per_head, s)


def _sink_rows(sink_l, rows_per_head):
    return jnp.repeat(sink_l.astype(F32).reshape(N_KV_HEADS, GQA, 1), rows_per_head, axis=1).reshape(
        N_KV_HEADS, GQA * rows_per_head, 1)


def kernel(x_prompt, x_sample, cache_k, cache_v, state_shift, state_wkv, norm1, w_in, w_out, sink,
           rel_table, mu, w0, w2, a0, a2, g2, k_k, k_a, r_k, gn_w, gn_b, norm2, w_gate, w_up,
           w_down, norm_f):
    depth = w_in.shape[0]
    B, T, _ = x_prompt.shape
    DB, S, _ = x_sample.shape
    wc = cache_k.shape[2]

    qi = np.arange(BLOCK)[:, None]
    sj = np.arange(2 * BLOCK)[None, :]
    dp = qi + BLOCK - sj
    inside_p = (dp >= 0) & (dp <= WINDOW)
    bias_p = jnp.stack([_masked_bias(rel_table, dp, inside_p & (sj >= BLOCK), BLOCK, keys_major=True),
                        _masked_bias(rel_table, dp, inside_p, BLOCK, keys_major=True)])
    ds_full = (wc + np.arange(S))[:, None] - np.arange(wc + S)[None, :]
    inside_s = (ds_full >= 0) & (ds_full <= WINDOW)
    bias_c = _masked_bias(rel_table, ds_full[:, :wc], inside_s[:, :wc], SAMPLE_PAD)
    dn = np.full((S, SAMPLE_PAD), -1, np.int64)
    dn[:, :S] = ds_full[:, wc:]
    inn = np.zeros((S, SAMPLE_PAD), bool)
    inn[:, :S] = inside_s[:, wc:]
    bias_n = _masked_bias(rel_table, dn, inn, SAMPLE_PAD)

    xp = x_prompt.reshape(B * T, D_MODEL)
    xs = x_sample.reshape(DB * S, D_MODEL)
    zeros_shift = jnp.zeros((B, 1, RW_IN_W), F32)
    zeros_state = jnp.zeros((B, RW_HEADS, HEAD_DIM, HEAD_DIM), F32)
    nf = norm_f.reshape(1, D_MODEL)
    ones = jnp.ones((1, D_MODEL), F32)

    kp, vp, sp, wp, ksl, vsl, ssl, wsl = [], [], [], [], [], [], [], []
    for l in range(depth):
        w_in_b = w_in[l].astype(BF16)
        wo_b, wg_b, wu_b, wd_b = (w_out[l].astype(BF16), w_gate[l].astype(BF16),
                                  w_up[l].astype(BF16), w_down[l].astype(BF16))
        n1 = norm1[l].reshape(1, D_MODEL)
        n2 = norm2[l].reshape(1, D_MODEL)
        mu_l = mu[l].reshape(1, RW_IN_W)
        vecs = jnp.stack([w0[l], a0[l], k_k[l], k_a[l], r_k[l], gn_w[l], gn_b[l], jnp.zeros_like(w0[l])])
        w2_b, a2_b, g2_b = w2[l].astype(BF16), a2[l].astype(BF16), g2[l].astype(BF16)
        final = l == depth - 1

        za, zr, h = _inproj(xp, n1, w_in_b, norm=True, tm=512)
        za3 = za.reshape(B, T, ATT_IN_W)
        w = min(WINDOW, T)
        kp.append(za3[:, T - w:, ATT_W:ATT_W + KV_W].reshape(B, w, N_KV_HEADS, HEAD_DIM))
        vp.append(za3[:, T - w:, ATT_W + KV_W:].reshape(B, w, N_KV_HEADS, HEAD_DIM))
        sp.append(h.reshape(B, T, D_MODEL)[:, -1])
        sink_lanes = jnp.swapaxes(_sink_rows(sink[l], BLOCK), 1, 2)
        att = _attn_prompt(za3, bias_p, sink_lanes, nblk=2)
        rw, wkv = _rwkv(zr.reshape(B, T, RW_IN_W), zeros_shift, zeros_state, mu_l, vecs, w2_b, a2_b, g2_b,
                        chunk=PROMPT_CHUNK, n_valid=PROMPT_CHUNK, nb=2)
        wp.append(wkv)
        xp = _outffn(xp, att.reshape(B * T, ATT_W), rw.reshape(B * T, RW_W), wo_b, n2, wg_b, wu_b, wd_b, nf,
                     final=final, tm=256)

        za_s, zr_s, h_s = _inproj(xs, n1, w_in_b, norm=True, tm=DB * S)
        _, zr_prev, _ = _inproj(state_shift[l], ones, w_in_b, norm=False, tm=DB)
        ssl.append(h_s.reshape(DB, S, D_MODEL)[:, -1])
        za_pad = jnp.pad(za_s.reshape(DB, S, ATT_IN_W), ((0, 0), (0, SAMPLE_PAD - S), (0, 0)))
        att_s, kb, vb = _attn_sample(za_pad, cache_k[l].reshape(DB, wc, KV_W), cache_v[l].reshape(DB, wc, KV_W),
                                     bias_c, bias_n, _sink_rows(sink[l], SAMPLE_PAD), n_new=S)
        ksl.append(kb.reshape(DB, wc, N_KV_HEADS, HEAD_DIM))
        vsl.append(vb.reshape(DB, wc, N_KV_HEADS, HEAD_DIM))
        zr_pad = jnp.pad(zr_s.reshape(DB, S, RW_IN_W), ((0, 0), (0, SAMPLE_CHUNK - S), (0, 0)))
        rw_s, wkv_s = _rwkv(zr_pad, zr_prev.reshape(DB, 1, RW_IN_W), state_wkv[l], mu_l, vecs, w2_b, a2_b, g2_b,
                            chunk=SAMPLE_CHUNK, n_valid=S, nb=2)
        wsl.append(wkv_s)
        xs = _outffn(xs, att_s[:, :S].reshape(DB * S, ATT_W), rw_s[:, :S].reshape(DB * S, RW_W),
                     wo_b, n2, wg_b, wu_b, wd_b, nf, final=final, tm=DB * S)

    y_prompt = xp.reshape(B, T, D_MODEL)
    y_sample = xs.reshape(DB, S, D_MODEL)
    return (y_prompt, y_sample, jnp.stack(kp), jnp.stack(vp), jnp.stack(sp), jnp.stack(wp),
            jnp.stack(ksl), jnp.stack(vsl), jnp.stack(ssl), jnp.stack(wsl))
```

```python
import functools

import numpy as np
import jax
import jax.numpy as jnp
from jax import lax
from jax.experimental import pallas as pl
from jax.experimental.pallas import tpu as pltpu

D_MODEL = 1024
HEAD_DIM = 64
N_Q_HEADS = 8
N_KV_HEADS = 2
GQA = N_Q_HEADS // N_KV_HEADS
ATT_W = N_Q_HEADS * HEAD_DIM
KV_W = N_KV_HEADS * HEAD_DIM
WINDOW = 128
BLOCK = 128
NUM_BUCKETS = 32
MAX_DISTANCE = 128
RW_HEADS = 8
RW_W = RW_HEADS * HEAD_DIM
DECAY_LORA = 64
ICLR_LORA = 64
GATE_LORA = 128
RW_IN_W = 3 * RW_W + DECAY_LORA + ICLR_LORA + GATE_LORA
ATT_IN_W = ATT_W + 2 * KV_W
NORM_EPS = 1e-6
GN_EPS = 64e-5
NEG = -1e30
SCALE = HEAD_DIM ** -0.5

V7X_VMEM_LIMIT_BYTES = 56 * 1024 * 1024

PROMPT_CHUNK = 64
SAMPLE_CHUNK = 16
SAMPLE_PAD = 8

BF16 = jnp.bfloat16
F32 = jnp.float32


def _mm(a, b):
    return jnp.dot(a.astype(BF16), b.astype(BF16), preferred_element_type=F32)


def _mm_nt(a, b):
    return lax.dot_general(a.astype(BF16), b.astype(BF16), (((1,), (1,)), ((), ())),
                           preferred_element_type=F32)


def _mm_tn(a, b):
    return lax.dot_general(a.astype(BF16), b.astype(BF16), (((0,), (0,)), ((), ())),
                           preferred_element_type=F32)


def _rmsnorm(x, g):
    return x * lax.rsqrt(jnp.mean(x * x, -1, keepdims=True) + NORM_EPS) * g


def _params(n_axes):
    return pltpu.CompilerParams(dimension_semantics=("arbitrary",) * n_axes,
                                vmem_limit_bytes=V7X_VMEM_LIMIT_BYTES)


def _inproj_body(x_ref, g_ref, w_ref, za_ref, zr_ref, h_ref, *, norm):
    x = x_ref[...]
    h = _rmsnorm(x, g_ref[...]) if norm else x
    h_ref[...] = h
    z = jnp.dot(h.astype(BF16), w_ref[...], preferred_element_type=F32)
    za_ref[...] = z[:, :ATT_IN_W]
    zr_ref[...] = z[:, ATT_IN_W:]


def _inproj(x, g, w, *, norm, tm):
    n = x.shape[0]
    in_w = w.shape[1]
    return pl.pallas_call(
        functools.partial(_inproj_body, norm=norm),
        grid=(n // tm,),
        in_specs=[pl.BlockSpec((tm, D_MODEL), lambda i: (i, 0)),
                  pl.BlockSpec((1, D_MODEL), lambda i: (0, 0)),
                  pl.BlockSpec((D_MODEL, in_w), lambda i: (0, 0))],
        out_specs=[pl.BlockSpec((tm, ATT_IN_W), lambda i: (i, 0)),
                   pl.BlockSpec((tm, RW_IN_W), lambda i: (i, 0)),
                   pl.BlockSpec((tm, D_MODEL), lambda i: (i, 0))],
        out_shape=[jax.ShapeDtypeStruct((n, ATT_IN_W), F32),
                   jax.ShapeDtypeStruct((n, RW_IN_W), F32),
                   jax.ShapeDtypeStruct((n, D_MODEL), F32)],
        compiler_params=_params(1),
        name="inproj_norm" if norm else "inproj_plain",
    )(x, g, w)


def _sink_softmax(score_parts, sink_col):
    m = sink_col
    for s in score_parts:
        m = jnp.maximum(m, jnp.max(s, -1, keepdims=True))
    es = [jnp.exp(s - m) for s in score_parts]
    den = jnp.exp(sink_col - m)
    for e in es:
        den = den + jnp.sum(e, -1, keepdims=True)
    return [e / den for e in es]


QG_W = GQA * HEAD_DIM


def _tile_kv_heads(x):
    rot = pltpu.roll(x, HEAD_DIM, 1)
    first = lax.broadcasted_iota(jnp.int32, x.shape, 1) < HEAD_DIM
    halves = (jnp.where(first, x, rot).astype(BF16), jnp.where(first, rot, x).astype(BF16))
    return [jnp.concatenate([h] * (QG_W // KV_W), axis=1) for h in halves]


def _attn_prompt_body(q_ref, kv_ref, bias_ref, sink_ref, o_ref, kt_prev, vT_prev, *, nblk):
    j = pl.program_id(1)

    @pl.when(j == 0)
    def _():
        kt_prev[...] = jnp.zeros_like(kt_prev)
        vT_prev[...] = jnp.zeros_like(vT_prev)

    lane = lax.broadcasted_iota(jnp.int32, (1, 128), 1)
    lane_half = ((lane < HEAD_DIM).astype(BF16), (lane >= HEAD_DIM).astype(BF16))
    kts = [[kt_prev[kv] for kv in range(N_KV_HEADS)]]
    vTs = [vT_prev[...]]
    for i in range(nblk):
        rows = slice(i * BLOCK, (i + 1) * BLOCK)
        kts.append(_tile_kv_heads(kv_ref[0, rows, 0:KV_W]))
        vTs.append(kv_ref[0, rows, KV_W:2 * KV_W].T.astype(BF16))
    for kv in range(N_KV_HEADS):
        kt_prev[kv] = kts[nblk][kv]
    vT_prev[...] = vTs[nblk]

    units = []
    for i in range(nblk):
        qb = (q_ref[0, i * BLOCK:(i + 1) * BLOCK, :] * SCALE).astype(BF16)
        first = jnp.where(j == 0, 0, 1) if i == 0 else 1
        for kv in range(N_KV_HEADS):
            qe = _expand(qb[:, kv * QG_W:(kv + 1) * QG_W], lane_half)
            kcat = jnp.concatenate([kts[i][kv], kts[i + 1][kv]], axis=0)
            units.append(dict(i=i, kv=kv, s=_mm_nt(kcat, qe), first=first))
    for u in units:
        b = bias_ref[u["first"], u["kv"]]
        s = jnp.where(b > 0.5 * NEG, u["s"] + b, NEG)
        sink_row = sink_ref[u["kv"]]
        m = jnp.maximum(jnp.max(s, 0, keepdims=True), sink_row)
        e = jnp.exp(s - m)
        den = jnp.sum(e, 0, keepdims=True) + jnp.exp(sink_row - m)
        u["p"] = (e * (1.0 / den)).astype(BF16)
    pieces = [[None] * N_Q_HEADS for _ in range(nblk)]
    for u in units:
        i, kv = u["i"], u["kv"]
        hd = slice(kv * HEAD_DIM, (kv + 1) * HEAD_DIM)
        vcatT = jnp.concatenate([vTs[i][hd], vTs[i + 1][hd]], axis=1)
        oT = jnp.dot(vcatT, u["p"], preferred_element_type=F32)
        for g in range(GQA):
            pieces[i][kv * GQA + g] = oT[:, g * BLOCK:(g + 1) * BLOCK]
    for i in range(nblk):
        o_ref[0, i * BLOCK:(i + 1) * BLOCK, :] = jnp.concatenate(pieces[i], axis=0).T


def _attn_prompt(za, bias, sink_rows, *, nblk):
    b, t, _ = za.shape
    rows = nblk * BLOCK
    full = lambda a: pl.BlockSpec(a.shape, lambda i, j: (0,) * a.ndim)
    return pl.pallas_call(
        functools.partial(_attn_prompt_body, nblk=nblk),
        grid=(b, t // rows),
        in_specs=[pl.BlockSpec((1, rows, ATT_W), lambda i, j: (i, j, 0)),
                  pl.BlockSpec((1, rows, 2 * KV_W), lambda i, j: (i, j, ATT_W // (2 * KV_W))),
                  full(bias), full(sink_rows)],
        out_specs=pl.BlockSpec((1, rows, ATT_W), lambda i, j: (i, j, 0)),
        out_shape=jax.ShapeDtypeStruct((b, t, ATT_W), F32),
        scratch_shapes=[pltpu.VMEM((N_KV_HEADS, BLOCK, QG_W), BF16),
                        pltpu.VMEM((KV_W, BLOCK), BF16)],
        compiler_params=_params(2),
        name="attn_prompt",
    )(za, za, bias, sink_rows)


def _attn_sample_body(z_ref, ck_ref, cv_ref, bias_c_ref, bias_n_ref, sink_ref,
                      o_ref, kb_ref, vb_ref, *, n_new):
    z = z_ref[0]
    q = z[:, :ATT_W]
    kn = z[:, ATT_W:ATT_W + KV_W]
    vn = z[:, ATT_W + KV_W:]
    ck, cv = ck_ref[0], cv_ref[0]
    outs = []
    for kv in range(N_KV_HEADS):
        sl = slice(kv * HEAD_DIM, (kv + 1) * HEAD_DIM)
        qg = jnp.concatenate(
            [q[:, (kv * GQA + g) * HEAD_DIM:(kv * GQA + g + 1) * HEAD_DIM] for g in range(GQA)], axis=0)
        bc, bn = bias_c_ref[kv], bias_n_ref[kv]
        sc = _mm_nt(qg, ck[:, sl]) * SCALE
        sn = _mm_nt(qg, kn[:, sl]) * SCALE
        sc = jnp.where(bc > 0.5 * NEG, sc + bc, NEG)
        sn = jnp.where(bn > 0.5 * NEG, sn + bn, NEG)
        pc, pn = _sink_softmax([sc, sn], sink_ref[kv])
        o = _mm(pc, cv[:, sl]) + _mm(pn, vn[:, sl])
        outs += [o[g * SAMPLE_PAD:(g + 1) * SAMPLE_PAD] for g in range(GQA)]
    o_ref[0] = jnp.concatenate(outs, axis=1)
    keep = WINDOW - n_new
    kb_ref[0, 0:keep, :] = ck_ref[0, n_new:WINDOW, :]
    kb_ref[0, keep:WINDOW, :] = kn[0:n_new]
    vb_ref[0, 0:keep, :] = cv_ref[0, n_new:WINDOW, :]
    vb_ref[0, keep:WINDOW, :] = vn[0:n_new]


def _attn_sample(za_pad, ck, cv, bias_c, bias_n, sink_rows, *, n_new):
    db = za_pad.shape[0]
    full = lambda a: pl.BlockSpec(a.shape, lambda i: (0,) * a.ndim)
    return pl.pallas_call(
        functools.partial(_attn_sample_body, n_new=n_new),
        grid=(db,),
        in_specs=[pl.BlockSpec((1, SAMPLE_PAD, ATT_IN_W), lambda i: (i, 0, 0)),
                  pl.BlockSpec((1, WINDOW, KV_W), lambda i: (i, 0, 0)),
                  pl.BlockSpec((1, WINDOW, KV_W), lambda i: (i, 0, 0)),
                  full(bias_c), full(bias_n), full(sink_rows)],
        out_specs=[pl.BlockSpec((1, SAMPLE_PAD, ATT_W), lambda i: (i, 0, 0)),
                   pl.BlockSpec((1, WINDOW, KV_W), lambda i: (i, 0, 0)),
                   pl.BlockSpec((1, WINDOW, KV_W), lambda i: (i, 0, 0))],
        out_shape=[jax.ShapeDtypeStruct((db, SAMPLE_PAD, ATT_W), F32),
                   jax.ShapeDtypeStruct((db, WINDOW, KV_W), F32),
                   jax.ShapeDtypeStruct((db, WINDOW, KV_W), F32)],
        compiler_params=_params(1),
        name="attn_sample",
    )(za_pad, ck, cv, bias_c, bias_n, sink_rows)


GROUP_HEADS = 4
GROUP_W = GROUP_HEADS * HEAD_DIM
N_GROUPS = RW_HEADS // GROUP_HEADS


def _head_sums(x, ones_blk):
    m = x.shape[0]
    hi = x.astype(BF16)
    lo = (x - hi.astype(F32)).astype(BF16)
    hl = jnp.concatenate([hi, lo], axis=0)
    y = jnp.concatenate(
        [jnp.dot(hl[:, g * GROUP_W:(g + 1) * GROUP_W], ones_blk, preferred_element_type=F32)
         for g in range(N_GROUPS)], axis=1)
    return y[:m] + y[m:]


def _expand(x, lane_half):
    xb = x.astype(BF16)
    c = xb.shape[0]
    zero = jnp.zeros((c, 128), BF16)
    blocks = []
    for h in range(GROUP_HEADS):
        col = h // 2
        piece = xb[:, col * 128:(col + 1) * 128] * lane_half[h % 2]
        blocks.append(jnp.concatenate([piece, zero] if col == 0 else [zero, piece], axis=1))
    return jnp.concatenate(blocks, axis=0)


def _expand_state(s, g):
    zero = jnp.zeros((HEAD_DIM, HEAD_DIM), F32)
    rows = []
    for h in range(GROUP_HEADS):
        rows.append(jnp.concatenate(
            [s[g * GROUP_HEADS + h] if j == h else zero for j in range(GROUP_HEADS)], axis=1))
    return jnp.concatenate(rows, axis=0)


def _rwkv_body(z_ref, zpf_ref, s0_ref, mu_ref, vec_ref, w2_ref, a2_ref, g2_ref,
               blkp_ref, blks_ref, strict_ref, incl_ref, ones_ref,
               o_ref, sout_ref, state, zbuf, *, chunk, n_valid, solve_steps, nb, wave):
    C = chunk
    R = GROUP_HEADS * C
    c = pl.program_id(1)

    @pl.when(c == 0)
    def _():
        for b in range(nb):
            s0 = s0_ref[b]
            for g in range(N_GROUPS):
                state[b, g] = _expand_state(s0, g)
            zbuf[b, 7:8, :] = zpf_ref[b]

    @pl.when(c > 0)
    def _():
        for b in range(nb):
            zbuf[b, 7:8, :] = zbuf[b, C + 7:C + 8, :]

    w0, a0, k_k, k_a = vec_ref[0:1, :], vec_ref[1:2, :], vec_ref[2:3, :], vec_ref[3:4, :]
    r_k, gn_w, gn_b = vec_ref[4:5, :], vec_ref[5:6, :], vec_ref[6:7, :]
    ones_blk = ones_ref[...]
    ri = lax.broadcasted_iota(jnp.int32, (C, C), 0)
    ci = lax.broadcasted_iota(jnp.int32, (C, C), 1)
    tri = (ri >= ci).astype(BF16)
    lane = lax.broadcasted_iota(jnp.int32, (1, 128), 1)
    blk = ((lane < HEAD_DIM).astype(BF16), (lane >= HEAD_DIM).astype(BF16))
    strict = strict_ref[...]
    incl = incl_ref[...]
    blk_p = blkp_ref[...]
    blk_s = blks_ref[...]

    for first in range(0, nb, wave):
        _rwkv_wave(range(first, min(nb, first + wave)), z_ref, zbuf, mu_ref, w2_ref, a2_ref, g2_ref, o_ref,
                   state, (w0, a0, k_k, k_a, r_k, gn_w, gn_b), (ones_blk, tri, blk, strict, incl, blk_p, blk_s),
                   C=C, n_valid=n_valid, solve_steps=solve_steps)

    @pl.when(c == pl.num_programs(1) - 1)
    def _():
        for b in range(nb):
            for g in range(N_GROUPS):
                s = state[b, g]
                for h in range(GROUP_HEADS):
                    sout_ref[b, g * GROUP_HEADS + h] = s[h * HEAD_DIM:(h + 1) * HEAD_DIM,
                                                         h * HEAD_DIM:(h + 1) * HEAD_DIM]


def _rwkv_wave(batches, z_ref, zbuf, mu_ref, w2_ref, a2_ref, g2_ref, o_ref, state, vecs, consts,
               *, C, n_valid, solve_steps):
    w0, a0, k_k, k_a, r_k, gn_w, gn_b = vecs
    ones_blk, tri, blk, strict, incl, blk_p, blk_s = consts
    R = GROUP_HEADS * C

    tails = {}
    chains = []
    for b in batches:
        zc = z_ref[b]
        zbuf[b, 8:8 + C, :] = zc
        zp = zbuf[b, 7:7 + C, :]
        zm = zc + (zp - zc) * mu_ref[...]
        r = zm[:, 0:RW_W]
        k = zm[:, RW_W:2 * RW_W]
        v = zm[:, 2 * RW_W:3 * RW_W]
        o1 = 3 * RW_W
        zw = zm[:, o1:o1 + DECAY_LORA]
        za = zm[:, o1 + DECAY_LORA:o1 + DECAY_LORA + ICLR_LORA]
        zg = zm[:, o1 + DECAY_LORA + ICLR_LORA:]

        y = -(w0 + _mm(jnp.tanh(zw), w2_ref[...]))
        softplus = jnp.maximum(y, 0.0) + jnp.log(1.0 + jnp.exp(-jnp.abs(y)))
        logdecay = -jnp.exp(-softplus - 0.5)
        a = jax.nn.sigmoid(a0 + _mm(za, a2_ref[...]))
        g = _mm(jax.nn.sigmoid(zg), g2_ref[...])
        kk_raw = k * k_k
        kmod = k * (1.0 + (a - 1.0) * k_a)
        vv = v
        if n_valid < C:
            rows = lax.broadcasted_iota(jnp.int32, (C, 1), 0) < n_valid
            logdecay = jnp.where(rows, logdecay, 0.0)
            kk_raw = jnp.where(rows, kk_raw, 0.0)
            kmod = jnp.where(rows, kmod, 0.0)
            vv = jnp.where(rows, v, 0.0)

        sums = _head_sums(jnp.concatenate([kk_raw * kk_raw, r * kmod * r_k], axis=0), ones_blk)
        kk = kk_raw / jnp.maximum(jnp.sqrt(sums[:C]), 1e-12)
        bonus = sums[C:] * v

        hi = logdecay.astype(BF16)
        lo = (logdecay - hi.astype(F32)).astype(BF16)
        cum = jnp.dot(tri, jnp.concatenate([hi, lo], axis=1), preferred_element_type=F32)
        cum = cum[:, :RW_W] + cum[:, RW_W:]
        gam = jnp.exp(cum)
        gam_inv = jnp.exp(-cum)
        At = -kk * jnp.exp(cum - logdecay)
        Bt = kk * a * gam_inv
        Kt = kmod * gam_inv
        Rt = r * gam
        tails[b] = (bonus, g)
        for grp in range(N_GROUPS):
            sl = slice(grp * GROUP_W, (grp + 1) * GROUP_W)
            chains.append(dict(
                b=b, grp=grp,
                AR=jnp.concatenate([At[:, sl], Rt[:, sl]], axis=0).astype(BF16),
                BK=jnp.concatenate([Bt[:, sl], Kt[:, sl]], axis=0).astype(BF16),
                BKe=jnp.concatenate([_expand(Bt[:, sl], blk), _expand(Kt[:, sl], blk)], axis=0),
                V=vv[:, sl],
                Ve=_expand(vv[:, sl], blk),
                gam_last=gam[C - 1:C, sl]))

    for ch in chains:
        G = _mm_nt(ch["AR"], ch["BKe"])
        ch["P"] = (G[:C, :R] * strict).astype(BF16)
        ch["Lak"] = (G[:C, R:] * strict).astype(BF16)
        ch["M"] = jnp.concatenate([G[C:, :R] * incl, G[C:, R:] * incl], axis=1).astype(BF16)
        ch["S"] = state[ch["b"], ch["grp"]]
        ch["P0"] = _mm_nt(ch["AR"], ch["S"])
    for ch in chains:
        ch["X"] = ch["P0"][:C] + jnp.dot(ch["Lak"], ch["Ve"], preferred_element_type=F32)
    for i in range(solve_steps):
        for ch in chains:
            ch["X"] = ch["X"] + jnp.dot(ch["P"], _expand(ch["X"], blk), preferred_element_type=F32)
            if i + 1 < solve_steps:
                if C == HEAD_DIM:
                    pe = _expand(ch["P"], blk)
                else:
                    pe = jnp.concatenate([ch["P"]] * GROUP_HEADS, axis=0) * blk_p
                ch["P"] = jnp.dot(ch["P"], pe, preferred_element_type=F32).astype(BF16)
    outs = {b: [None] * N_GROUPS for b in batches}
    for ch in chains:
        UVe = jnp.concatenate([_expand(ch["X"], blk), ch["Ve"]], axis=0)
        outs[ch["b"]][ch["grp"]] = ch["P0"][C:] + jnp.dot(ch["M"], UVe, preferred_element_type=F32)
        UV = jnp.concatenate([ch["X"], ch["V"]], axis=0)
        upd = _mm_tn(UV, ch["BK"]) * blk_s
        state[ch["b"], ch["grp"]] = (ch["S"] + upd) * ch["gam_last"]

    for b in batches:
        bonus, g = tails[b]
        O = jnp.concatenate(outs[b], axis=1)
        mean = _head_sums(O, ones_blk) * (1.0 / HEAD_DIM)
        d = O - mean
        var = _head_sums(d * d, ones_blk) * (1.0 / HEAD_DIM)
        on = d * lax.rsqrt(var + GN_EPS) * gn_w + gn_b
        o_ref[b] = (on + bonus) * g


def _rwkv_masks(chunk):
    R = GROUP_HEADS * chunk
    row_h = np.arange(R) // chunk
    lane_h = np.arange(GROUP_W) // HEAD_DIM
    blk_p = row_h[:, None] == row_h[None, :]
    blk_s = lane_h[:, None] == lane_h[None, :]
    t, s = np.arange(chunk)[:, None], (np.arange(R) % chunk)[None, :]
    return (jnp.asarray(blk_p, BF16), jnp.asarray(blk_s, F32),
            jnp.asarray(t > s, F32), jnp.asarray(t >= s, F32), jnp.asarray(blk_s, BF16))


def _rwkv(zr, zp_first, s0, mu, vecs, w2, a2, g2, *, chunk, n_valid, nb, wave):
    b, t, _ = zr.shape
    solve_steps = max(1, (n_valid - 1).bit_length())
    masks = _rwkv_masks(chunk)
    full = lambda a: pl.BlockSpec(a.shape, lambda i, j: (0,) * a.ndim)
    return pl.pallas_call(
        functools.partial(_rwkv_body, chunk=chunk, n_valid=n_valid, solve_steps=solve_steps, nb=nb, wave=wave),
        grid=(b // nb, t // chunk),
        in_specs=[pl.BlockSpec((nb, chunk, RW_IN_W), lambda i, j: (i, j, 0)),
                  pl.BlockSpec((nb, 1, RW_IN_W), lambda i, j: (i, 0, 0)),
                  pl.BlockSpec((nb, RW_HEADS, HEAD_DIM, HEAD_DIM), lambda i, j: (i, 0, 0, 0)),
                  full(mu), full(vecs), full(w2), full(a2), full(g2)] + [full(m) for m in masks],
        out_specs=[pl.BlockSpec((nb, chunk, RW_W), lambda i, j: (i, j, 0)),
                   pl.BlockSpec((nb, RW_HEADS, HEAD_DIM, HEAD_DIM), lambda i, j: (i, 0, 0, 0))],
        out_shape=[jax.ShapeDtypeStruct((b, t, RW_W), F32),
                   jax.ShapeDtypeStruct((b, RW_HEADS, HEAD_DIM, HEAD_DIM), F32)],
        scratch_shapes=[pltpu.VMEM((nb, N_GROUPS, GROUP_W, GROUP_W), F32),
                        pltpu.VMEM((nb, chunk + 8, RW_IN_W), F32)],
        compiler_params=_params(2),
        name="rwkv_chunk%d" % chunk,
    )(zr, zp_first, s0, mu, vecs, w2, a2, g2, *masks)


def _outffn_body(x_ref, att_ref, rw_ref, wo_ref, n2_ref, wg_ref, wu_ref, wd_ref, nf_ref, xo_ref, *, final):
    mix = jnp.concatenate([att_ref[...], rw_ref[...]], axis=-1).astype(BF16)
    x1 = x_ref[...] + jnp.dot(mix, wo_ref[...], preferred_element_type=F32)
    hb = _rmsnorm(x1, n2_ref[...]).astype(BF16)
    gate = jnp.dot(hb, wg_ref[...], preferred_element_type=F32)
    up = jnp.dot(hb, wu_ref[...], preferred_element_type=F32)
    act = (gate * jax.nn.sigmoid(gate) * up).astype(BF16)
    x2 = x1 + jnp.dot(act, wd_ref[...], preferred_element_type=F32)
    xo_ref[...] = _rmsnorm(x2, nf_ref[...]) if final else x2


def _outffn(x, att, rw, wo, n2, wg, wu, wd, nf, *, final, tm):
    n = x.shape[0]
    resident = lambda a: pl.BlockSpec(a.shape, lambda i: (0,) * a.ndim, pipeline_mode=pl.Buffered(1))
    return pl.pallas_call(
        functools.partial(_outffn_body, final=final),
        grid=(n // tm,),
        in_specs=[pl.BlockSpec((tm, D_MODEL), lambda i: (i, 0)),
                  pl.BlockSpec((tm, ATT_W), lambda i: (i, 0)),
                  pl.BlockSpec((tm, RW_W), lambda i: (i, 0)),
                  resident(wo), resident(n2), resident(wg), resident(wu), resident(wd), resident(nf)],
        out_specs=pl.BlockSpec((tm, D_MODEL), lambda i: (i, 0)),
        out_shape=jax.ShapeDtypeStruct((n, D_MODEL), F32),
        compiler_params=_params(1),
        name="outffn_final" if final else "outffn",
    )(x, att, rw, wo, n2, wg, wu, wd, nf)


def _t5_bucket_np(d):
    max_exact = NUM_BUCKETS // 2
    n = np.maximum(d, 0)
    large = max_exact + (np.log(np.maximum(n, 1) / max_exact) / np.log(MAX_DISTANCE / max_exact)
                         * (NUM_BUCKETS - max_exact)).astype(np.int32)
    large = np.minimum(large, NUM_BUCKETS - 1)
    return np.where(n < max_exact, n, large).astype(np.int32)


def _bias_body(tab_ref, bucket_ref, inside_ref, o_ref, *, keys_major):
    bucket = bucket_ref[...]
    inside = inside_ref[...] > 0
    for h in range(N_Q_HEADS):
        acc = jnp.zeros(bucket.shape, F32)
        for bk in range(NUM_BUCKETS):
            acc = jnp.where(bucket == bk, tab_ref[bk, h], acc)
        acc = jnp.where(inside, acc, NEG)
        if keys_major:
            kv, g, q = h // GQA, h % GQA, bucket.shape[1]
            o_ref[kv, :, g * q:(g + 1) * q] = acc
        else:
            o_ref[h] = acc


def _masked_bias(rel_table, d, inside, rows_per_head, keys_major=False):
    q, s = d.shape
    bucket = np.zeros((rows_per_head, s), np.int32)
    bucket[:q] = _t5_bucket_np(d)
    ins = np.zeros((rows_per_head, s), np.int32)
    ins[:q] = inside
    if keys_major:
        bucket, ins = bucket.T.copy(), ins.T.copy()
        out_shape = (N_KV_HEADS, s, GQA * rows_per_head)
    else:
        out_shape = (N_Q_HEADS, rows_per_head, s)
    vm = lambda shape: pl.BlockSpec(shape, lambda: (0,) * len(shape))
    b = pl.pallas_call(
        functools.partial(_bias_body, keys_major=keys_major),
        in_specs=[pl.BlockSpec(memory_space=pltpu.SMEM), vm(bucket.shape), vm(ins.shape)],
        out_specs=vm(out_shape),
        out_shape=jax.ShapeDtypeStruct(out_shape, F32),
        name="rel_bias",
    )(rel_table.astype(F32), jnp.asarray(bucket), jnp.asarray(ins))
    return b if keys_major else b.reshape(N_KV_HEADS, GQA * rows_per_head, s)


def _sink_rows(sink_l, rows_per_head):
    return jnp.repeat(sink_l.astype(F32).reshape(N_KV_HEADS, GQA, 1), rows_per_head, axis=1).reshape(
        N_KV_HEADS, GQA * rows_per_head, 1)


def kernel(x_prompt, x_sample, cache_k, cache_v, state_shift, state_wkv, norm1, w_in, w_out, sink,
           rel_table, mu, w0, w2, a0, a2, g2, k_k, k_a, r_k, gn_w, gn_b, norm2, w_gate, w_up,
           w_down, norm_f):
    depth = w_in.shape[0]
    B, T, _ = x_prompt.shape
    DB, S, _ = x_sample.shape
    wc = cache_k.shape[2]

    qi = np.arange(BLOCK)[:, None]
    sj = np.arange(2 * BLOCK)[None, :]
    dp = qi + BLOCK - sj
    inside_p = (dp >= 0) & (dp <= WINDOW)
    bias_p = jnp.stack([_masked_bias(rel_table, dp, inside_p & (sj >= BLOCK), BLOCK, keys_major=True),
                        _masked_bias(rel_table, dp, inside_p, BLOCK, keys_major=True)])
    ds_full = (wc + np.arange(S))[:, None] - np.arange(wc + S)[None, :]
    inside_s = (ds_full >= 0) & (ds_full <= WINDOW)
    bias_c = _masked_bias(rel_table, ds_full[:, :wc], inside_s[:, :wc], SAMPLE_PAD)
    dn = np.full((S, SAMPLE_PAD), -1, np.int64)
    dn[:, :S] = ds_full[:, wc:]
    inn = np.zeros((S, SAMPLE_PAD), bool)
    inn[:, :S] = inside_s[:, wc:]
    bias_n = _masked_bias(rel_table, dn, inn, SAMPLE_PAD)

    xp = x_prompt.reshape(B * T, D_MODEL)
    xs = x_sample.reshape(DB * S, D_MODEL)
    zeros_shift = jnp.zeros((B, 1, RW_IN_W), F32)
    zeros_state = jnp.zeros((B, RW_HEADS, HEAD_DIM, HEAD_DIM), F32)
    nf = norm_f.reshape(1, D_MODEL)
    ones = jnp.ones((1, D_MODEL), F32)

    kp, vp, sp, wp, ksl, vsl, ssl, wsl = [], [], [], [], [], [], [], []
    for l in range(depth):
        w_in_b = w_in[l].astype(BF16)
        wo_b, wg_b, wu_b, wd_b = (w_out[l].astype(BF16), w_gate[l].astype(BF16),
                                  w_up[l].astype(BF16), w_down[l].astype(BF16))
        n1 = norm1[l].reshape(1, D_MODEL)
        n2 = norm2[l].reshape(1, D_MODEL)
        mu_l = mu[l].reshape(1, RW_IN_W)
        vecs = jnp.stack([w0[l], a0[l], k_k[l], k_a[l], r_k[l], gn_w[l], gn_b[l], jnp.zeros_like(w0[l])])
        w2_b, a2_b, g2_b = w2[l].astype(BF16), a2[l].astype(BF16), g2[l].astype(BF16)
        final = l == depth - 1

        za, zr, h = _inproj(xp, n1, w_in_b, norm=True, tm=512)
        za3 = za.reshape(B, T, ATT_IN_W)
        w = min(WINDOW, T)
        kp.append(za3[:, T - w:, ATT_W:ATT_W + KV_W].reshape(B, w, N_KV_HEADS, HEAD_DIM))
        vp.append(za3[:, T - w:, ATT_W + KV_W:].reshape(B, w, N_KV_HEADS, HEAD_DIM))
        sp.append(h.reshape(B, T, D_MODEL)[:, -1])
        sink_lanes = jnp.swapaxes(_sink_rows(sink[l], BLOCK), 1, 2)
        att = _attn_prompt(za3, bias_p, sink_lanes, nblk=2)
        rw, wkv = _rwkv(zr.reshape(B, T, RW_IN_W), zeros_shift, zeros_state, mu_l, vecs, w2_b, a2_b, g2_b,
                        chunk=PROMPT_CHUNK, n_valid=PROMPT_CHUNK, nb=4, wave=4)
        wp.append(wkv)
        xp = _outffn(xp, att.reshape(B * T, ATT_W), rw.reshape(B * T, RW_W), wo_b, n2, wg_b, wu_b, wd_b, nf,
                     final=final, tm=256)

        za_s, zr_s, h_s = _inproj(xs, n1, w_in_b, norm=True, tm=DB * S)
        _, zr_prev, _ = _inproj(state_shift[l], ones, w_in_b, norm=False, tm=DB)
        ssl.append(h_s.reshape(DB, S, D_MODEL)[:, -1])
        za_pad = jnp.pad(za_s.reshape(DB, S, ATT_IN_W), ((0, 0), (0, SAMPLE_PAD - S), (0, 0)))
        att_s, kb, vb = _attn_sample(za_pad, cache_k[l].reshape(DB, wc, KV_W), cache_v[l].reshape(DB, wc, KV_W),
                                     bias_c, bias_n, _sink_rows(sink[l], SAMPLE_PAD), n_new=S)
        ksl.append(kb.reshape(DB, wc, N_KV_HEADS, HEAD_DIM))
        vsl.append(vb.reshape(DB, wc, N_KV_HEADS, HEAD_DIM))
        zr_pad = jnp.pad(zr_s.reshape(DB, S, RW_IN_W), ((0, 0), (0, SAMPLE_CHUNK - S), (0, 0)))
        rw_s, wkv_s = _rwkv(zr_pad, zr_prev.reshape(DB, 1, RW_IN_W), state_wkv[l], mu_l, vecs, w2_b, a2_b, g2_b,
                            chunk=SAMPLE_CHUNK, n_valid=S, nb=4, wave=2)
        wsl.append(wkv_s)
        xs = _outffn(xs, att_s[:, :S].reshape(DB * S, ATT_W), rw_s[:, :S].reshape(DB * S, RW_W),
                     wo_b, n2, wg_b, wu_b, wd_b, nf, final=final, tm=DB * S)

    y_prompt = xp.reshape(B, T, D_MODEL)
    y_sample = xs.reshape(DB, S, D_MODEL)
    return (y_prompt, y_sample, jnp.stack(kp), jnp.stack(vp), jnp.stack(sp), jnp.stack(wp),
            jnp.stack(ksl), jnp.stack(vsl), jnp.stack(ssl), jnp.stack(wsl))
```

```python
import functools

import numpy as np
import jax
import jax.numpy as jnp
from jax import lax
from jax.experimental import pallas as pl
from jax.experimental.pallas import tpu as pltpu

D_MODEL = 1024
HEAD_DIM = 64
N_Q_HEADS = 8
N_KV_HEADS = 2
GQA = N_Q_HEADS // N_KV_HEADS
ATT_W = N_Q_HEADS * HEAD_DIM
KV_W = N_KV_HEADS * HEAD_DIM
WINDOW = 128
BLOCK = 128
NUM_BUCKETS = 32
MAX_DISTANCE = 128
RW_HEADS = 8
RW_W = RW_HEADS * HEAD_DIM
DECAY_LORA = 64
ICLR_LORA = 64
GATE_LORA = 128
RW_IN_W = 3 * RW_W + DECAY_LORA + ICLR_LORA + GATE_LORA
ATT_IN_W = ATT_W + 2 * KV_W
NORM_EPS = 1e-6
GN_EPS = 64e-5
NEG = -1e30
SCALE = HEAD_DIM ** -0.5

V7X_VMEM_LIMIT_BYTES = 56 * 1024 * 1024

PROMPT_CHUNK = 64
SAMPLE_CHUNK = 16
SAMPLE_PAD = 8

BF16 = jnp.bfloat16
F32 = jnp.float32


def _mm(a, b):
    return jnp.dot(a.astype(BF16), b.astype(BF16), preferred_element_type=F32)


def _mm_nt(a, b):
    return lax.dot_general(a.astype(BF16), b.astype(BF16), (((1,), (1,)), ((), ())),
                           preferred_element_type=F32)


def _mm_tn(a, b):
    return lax.dot_general(a.astype(BF16), b.astype(BF16), (((0,), (0,)), ((), ())),
                           preferred_element_type=F32)


def _rmsnorm(x, g):
    return x * lax.rsqrt(jnp.mean(x * x, -1, keepdims=True) + NORM_EPS) * g


def _params(n_axes):
    return pltpu.CompilerParams(dimension_semantics=("arbitrary",) * n_axes,
                                vmem_limit_bytes=V7X_VMEM_LIMIT_BYTES)


def _inproj_body(x_ref, g_ref, w_ref, za_ref, zr_ref, h_ref, *, norm, seq_len):
    x = x_ref[...]
    h = _rmsnorm(x, g_ref[...]) if norm else x
    if seq_len is None:
        h_ref[...] = h
    else:
        h_ref[0] = h[x.shape[0] - 1:, :]
    z = jnp.dot(h.astype(BF16), w_ref[...], preferred_element_type=F32)
    za_ref[...] = z[:, :ATT_IN_W]
    zr_ref[...] = z[:, ATT_IN_W:]


def _inproj(x, g, w, *, norm, tm, seq_len=None):
    n = x.shape[0]
    in_w = w.shape[1]
    if seq_len is None:
        h_spec = pl.BlockSpec((tm, D_MODEL), lambda i: (i, 0))
        h_shape = jax.ShapeDtypeStruct((n, D_MODEL), F32)
    else:
        tiles = seq_len // tm
        h_spec = pl.BlockSpec((1, 1, D_MODEL), lambda i: (i // tiles, 0, 0))
        h_shape = jax.ShapeDtypeStruct((n // seq_len, 1, D_MODEL), F32)
    return pl.pallas_call(
        functools.partial(_inproj_body, norm=norm, seq_len=seq_len),
        grid=(n // tm,),
        in_specs=[pl.BlockSpec((tm, D_MODEL), lambda i: (i, 0)),
                  pl.BlockSpec((1, D_MODEL), lambda i: (0, 0)),
                  pl.BlockSpec((D_MODEL, in_w), lambda i: (0, 0))],
        out_specs=[pl.BlockSpec((tm, ATT_IN_W), lambda i: (i, 0)),
                   pl.BlockSpec((tm, RW_IN_W), lambda i: (i, 0)),
                   h_spec],
        out_shape=[jax.ShapeDtypeStruct((n, ATT_IN_W), F32),
                   jax.ShapeDtypeStruct((n, RW_IN_W), F32),
                   h_shape],
        compiler_params=_params(1),
        name="inproj_norm" if norm else "inproj_plain",
    )(x, g, w)


def _sink_softmax(score_parts, sink_col):
    m = sink_col
    for s in score_parts:
        m = jnp.maximum(m, jnp.max(s, -1, keepdims=True))
    es = [jnp.exp(s - m) for s in score_parts]
    den = jnp.exp(sink_col - m)
    for e in es:
        den = den + jnp.sum(e, -1, keepdims=True)
    return [e / den for e in es]


QG_W = GQA * HEAD_DIM


def _tile_kv_heads(x):
    rot = pltpu.roll(x, HEAD_DIM, 1)
    first = lax.broadcasted_iota(jnp.int32, x.shape, 1) < HEAD_DIM
    halves = (jnp.where(first, x, rot).astype(BF16), jnp.where(first, rot, x).astype(BF16))
    return [jnp.concatenate([h] * (QG_W // KV_W), axis=1) for h in halves]


def _attn_prompt_body(q_ref, kv_ref, bias_ref, sink_ref, o_ref, kt_prev, vT_prev, *, nblk):
    j = pl.program_id(1)

    @pl.when(j == 0)
    def _():
        kt_prev[...] = jnp.zeros_like(kt_prev)
        vT_prev[...] = jnp.zeros_like(vT_prev)

    lane = lax.broadcasted_iota(jnp.int32, (1, 128), 1)
    lane_half = ((lane < HEAD_DIM).astype(BF16), (lane >= HEAD_DIM).astype(BF16))
    kts = [[kt_prev[kv] for kv in range(N_KV_HEADS)]]
    vTs = [vT_prev[...]]
    for i in range(nblk):
        rows = slice(i * BLOCK, (i + 1) * BLOCK)
        kts.append(_tile_kv_heads(kv_ref[0, rows, 0:KV_W]))
        vTs.append(kv_ref[0, rows, KV_W:2 * KV_W].T.astype(BF16))
    for kv in range(N_KV_HEADS):
        kt_prev[kv] = kts[nblk][kv]
    vT_prev[...] = vTs[nblk]

    units = []
    for i in range(nblk):
        qb = (q_ref[0, i * BLOCK:(i + 1) * BLOCK, :] * SCALE).astype(BF16)
        first = jnp.where(j == 0, 0, 1) if i == 0 else 1
        for kv in range(N_KV_HEADS):
            qe = _expand(qb[:, kv * QG_W:(kv + 1) * QG_W], lane_half)
            kcat = jnp.concatenate([kts[i][kv], kts[i + 1][kv]], axis=0)
            units.append(dict(i=i, kv=kv, s=_mm_nt(kcat, qe), first=first))
    for u in units:
        b = bias_ref[u["first"], u["kv"]]
        s = jnp.where(b > 0.5 * NEG, u["s"] + b, NEG)
        sink_row = sink_ref[u["kv"]]
        m = jnp.maximum(jnp.max(s, 0, keepdims=True), sink_row)
        e = jnp.exp(s - m)
        den = jnp.sum(e, 0, keepdims=True) + jnp.exp(sink_row - m)
        u["p"] = (e * (1.0 / den)).astype(BF16)
    pieces = [[None] * N_Q_HEADS for _ in range(nblk)]
    for u in units:
        i, kv = u["i"], u["kv"]
        hd = slice(kv * HEAD_DIM, (kv + 1) * HEAD_DIM)
        vcatT = jnp.concatenate([vTs[i][hd], vTs[i + 1][hd]], axis=1)
        oT = jnp.dot(vcatT, u["p"], preferred_element_type=F32)
        for g in range(GQA):
            pieces[i][kv * GQA + g] = oT[:, g * BLOCK:(g + 1) * BLOCK]
    for i in range(nblk):
        o_ref[0, i * BLOCK:(i + 1) * BLOCK, :] = jnp.concatenate(pieces[i], axis=0).T


def _attn_prompt(za, bias, sink_rows, *, nblk):
    b, t, _ = za.shape
    rows = nblk * BLOCK
    full = lambda a: pl.BlockSpec(a.shape, lambda i, j: (0,) * a.ndim)
    return pl.pallas_call(
        functools.partial(_attn_prompt_body, nblk=nblk),
        grid=(b, t // rows),
        in_specs=[pl.BlockSpec((1, rows, ATT_W), lambda i, j: (i, j, 0)),
                  pl.BlockSpec((1, rows, 2 * KV_W), lambda i, j: (i, j, ATT_W // (2 * KV_W))),
                  full(bias), full(sink_rows)],
        out_specs=pl.BlockSpec((1, rows, ATT_W), lambda i, j: (i, j, 0)),
        out_shape=jax.ShapeDtypeStruct((b, t, ATT_W), F32),
        scratch_shapes=[pltpu.VMEM((N_KV_HEADS, BLOCK, QG_W), BF16),
                        pltpu.VMEM((KV_W, BLOCK), BF16)],
        compiler_params=_params(2),
        name="attn_prompt",
    )(za, za, bias, sink_rows)


def _attn_sample_body(z_ref, ck_ref, cv_ref, bias_c_ref, bias_n_ref, sink_ref,
                      o_ref, kb_ref, vb_ref, *, n_new, ns):
    units = []
    for s in range(ns):
        z = z_ref[s]
        q = z[:, :ATT_W]
        kn = z[:, ATT_W:ATT_W + KV_W]
        vn = z[:, ATT_W + KV_W:]
        ck, cv = ck_ref[s], cv_ref[s]
        keep = WINDOW - n_new
        kb_ref[s, 0:keep, :] = ck_ref[s, n_new:WINDOW, :]
        kb_ref[s, keep:WINDOW, :] = kn[0:n_new]
        vb_ref[s, 0:keep, :] = cv_ref[s, n_new:WINDOW, :]
        vb_ref[s, keep:WINDOW, :] = vn[0:n_new]
        for kv in range(N_KV_HEADS):
            sl = slice(kv * HEAD_DIM, (kv + 1) * HEAD_DIM)
            qg = jnp.concatenate(
                [q[:, (kv * GQA + g) * HEAD_DIM:(kv * GQA + g + 1) * HEAD_DIM] for g in range(GQA)], axis=0)
            units.append(dict(s=s, kv=kv, qg=qg, ck=ck[:, sl], kn=kn[:, sl], cv=cv[:, sl], vn=vn[:, sl]))
    for u in units:
        u["sc"] = _mm_nt(u["qg"], u["ck"]) * SCALE
        u["sn"] = _mm_nt(u["qg"], u["kn"]) * SCALE
    for u in units:
        bc, bn = bias_c_ref[u["kv"]], bias_n_ref[u["kv"]]
        sc = jnp.where(bc > 0.5 * NEG, u["sc"] + bc, NEG)
        sn = jnp.where(bn > 0.5 * NEG, u["sn"] + bn, NEG)
        u["pc"], u["pn"] = _sink_softmax([sc, sn], sink_ref[u["kv"]])
    outs = [[] for _ in range(ns)]
    for u in units:
        o = _mm(u["pc"], u["cv"]) + _mm(u["pn"], u["vn"])
        outs[u["s"]] += [o[g * SAMPLE_PAD:(g + 1) * SAMPLE_PAD] for g in range(GQA)]
    for s in range(ns):
        o_ref[s] = jnp.concatenate(outs[s], axis=1)


def _attn_sample(za_pad, ck, cv, bias_c, bias_n, sink_rows, *, n_new, ns):
    db = za_pad.shape[0]
    full = lambda a: pl.BlockSpec(a.shape, lambda i: (0,) * a.ndim)
    return pl.pallas_call(
        functools.partial(_attn_sample_body, n_new=n_new, ns=ns),
        grid=(db // ns,),
        in_specs=[pl.BlockSpec((ns, SAMPLE_PAD, ATT_IN_W), lambda i: (i, 0, 0)),
                  pl.BlockSpec((ns, WINDOW, KV_W), lambda i: (i, 0, 0)),
                  pl.BlockSpec((ns, WINDOW, KV_W), lambda i: (i, 0, 0)),
                  full(bias_c), full(bias_n), full(sink_rows)],
        out_specs=[pl.BlockSpec((ns, SAMPLE_PAD, ATT_W), lambda i: (i, 0, 0)),
                   pl.BlockSpec((ns, WINDOW, KV_W), lambda i: (i, 0, 0)),
                   pl.BlockSpec((ns, WINDOW, KV_W), lambda i: (i, 0, 0))],
        out_shape=[jax.ShapeDtypeStruct((db, SAMPLE_PAD, ATT_W), F32),
                   jax.ShapeDtypeStruct((db, WINDOW, KV_W), F32),
                   jax.ShapeDtypeStruct((db, WINDOW, KV_W), F32)],
        compiler_params=_params(1),
        name="attn_sample",
    )(za_pad, ck, cv, bias_c, bias_n, sink_rows)


GROUP_HEADS = 4
GROUP_W = GROUP_HEADS * HEAD_DIM
N_GROUPS = RW_HEADS // GROUP_HEADS


def _head_sums(x, ones_blk):
    m = x.shape[0]
    hi = x.astype(BF16)
    lo = (x - hi.astype(F32)).astype(BF16)
    hl = jnp.concatenate([hi, lo], axis=0)
    y = jnp.concatenate(
        [jnp.dot(hl[:, g * GROUP_W:(g + 1) * GROUP_W], ones_blk, preferred_element_type=F32)
         for g in range(N_GROUPS)], axis=1)
    return y[:m] + y[m:]


def _expand(x, lane_half):
    xb = x.astype(BF16)
    c = xb.shape[0]
    zero = jnp.zeros((c, 128), BF16)
    blocks = []
    for h in range(GROUP_HEADS):
        col = h // 2
        piece = xb[:, col * 128:(col + 1) * 128] * lane_half[h % 2]
        blocks.append(jnp.concatenate([piece, zero] if col == 0 else [zero, piece], axis=1))
    return jnp.concatenate(blocks, axis=0)


def _expand_state(s, g):
    zero = jnp.zeros((HEAD_DIM, HEAD_DIM), F32)
    rows = []
    for h in range(GROUP_HEADS):
        rows.append(jnp.concatenate(
            [s[g * GROUP_HEADS + h] if j == h else zero for j in range(GROUP_HEADS)], axis=1))
    return jnp.concatenate(rows, axis=0)


def _rwkv_body(z_ref, zpf_ref, s0_ref, mu_ref, vec_ref, w2_ref, a2_ref, g2_ref,
               blkp_ref, blks_ref, strict_ref, incl_ref, ones_ref,
               o_ref, sout_ref, state, zbuf, ar_s, bk_s, v_s, gl_s, bonus_s, gate_s,
               *, chunk, n_valid, solve_steps, nb, pipelined):
    C = chunk
    R = GROUP_HEADS * C
    j = pl.program_id(1)
    last = pl.num_programs(1) - 1
    wslot = j % 2 if pipelined else 0
    rslot = 1 - wslot if pipelined else 0

    @pl.when(j == 0)
    def _():
        for b in range(nb):
            s0 = s0_ref[b]
            for g in range(N_GROUPS):
                state[b, g] = _expand_state(s0, g)
            zbuf[b, 7:8, :] = zpf_ref[b]
        if pipelined:
            ar_s[1] = jnp.zeros(ar_s.shape[1:], BF16)
            bk_s[1] = jnp.zeros(bk_s.shape[1:], BF16)
            v_s[1] = jnp.zeros(v_s.shape[1:], BF16)
            gl_s[1] = jnp.ones(gl_s.shape[1:], F32)
            bonus_s[1] = jnp.zeros(bonus_s.shape[1:], F32)
            gate_s[1] = jnp.zeros(gate_s.shape[1:], F32)

    @pl.when(j > 0)
    def _():
        for b in range(nb):
            zbuf[b, 7:8, :] = zbuf[b, C + 7:C + 8, :]

    w0, a0, k_k, k_a = vec_ref[0:1, :], vec_ref[1:2, :], vec_ref[2:3, :], vec_ref[3:4, :]
    r_k, gn_w, gn_b = vec_ref[4:5, :], vec_ref[5:6, :], vec_ref[6:7, :]
    ones_blk = ones_ref[...]
    ri = lax.broadcasted_iota(jnp.int32, (C, C), 0)
    ci = lax.broadcasted_iota(jnp.int32, (C, C), 1)
    tri = (ri >= ci).astype(BF16)
    lane = lax.broadcasted_iota(jnp.int32, (1, 128), 1)
    blk = ((lane < HEAD_DIM).astype(BF16), (lane >= HEAD_DIM).astype(BF16))

    def prep_stages(b):
        p = {}

        def s1():
            zc = z_ref[b]
            zbuf[b, 8:8 + C, :] = zc
            zp = zbuf[b, 7:7 + C, :]
            zm = zc + (zp - zc) * mu_ref[...]
            p["r"] = zm[:, 0:RW_W]
            p["k"] = zm[:, RW_W:2 * RW_W]
            p["v"] = zm[:, 2 * RW_W:3 * RW_W]
            o1 = 3 * RW_W
            p["zw"] = zm[:, o1:o1 + DECAY_LORA]
            p["za"] = zm[:, o1 + DECAY_LORA:o1 + DECAY_LORA + ICLR_LORA]
            p["zg"] = zm[:, o1 + DECAY_LORA + ICLR_LORA:]

        def s2():
            y = -(w0 + _mm(jnp.tanh(p["zw"]), w2_ref[...]))
            softplus = jnp.maximum(y, 0.0) + jnp.log(1.0 + jnp.exp(-jnp.abs(y)))
            p["logdecay"] = -jnp.exp(-softplus - 0.5)
            p["a"] = jax.nn.sigmoid(a0 + _mm(p["za"], a2_ref[...]))
            gate_s[wslot, b] = _mm(jax.nn.sigmoid(p["zg"]), g2_ref[...])

        def s3():
            k, v, a = p["k"], p["v"], p["a"]
            kk_raw = k * k_k
            kmod = k * (1.0 + (a - 1.0) * k_a)
            vv = v
            if n_valid < C:
                rows = lax.broadcasted_iota(jnp.int32, (C, 1), 0) < n_valid
                p["logdecay"] = jnp.where(rows, p["logdecay"], 0.0)
                kk_raw = jnp.where(rows, kk_raw, 0.0)
                kmod = jnp.where(rows, kmod, 0.0)
                vv = jnp.where(rows, v, 0.0)
            v_s[wslot, b] = vv.astype(BF16)
            sums = _head_sums(jnp.concatenate([kk_raw * kk_raw, p["r"] * kmod * r_k], axis=0), ones_blk)
            p["kk"] = kk_raw / jnp.maximum(jnp.sqrt(sums[:C]), 1e-12)
            bonus_s[wslot, b] = sums[C:] * v
            p["kmod"] = kmod

        def s4():
            logdecay = p["logdecay"]
            hi = logdecay.astype(BF16)
            lo = (logdecay - hi.astype(F32)).astype(BF16)
            cum = jnp.dot(tri, jnp.concatenate([hi, lo], axis=1), preferred_element_type=F32)
            cum = cum[:, :RW_W] + cum[:, RW_W:]
            p["cum"] = cum
            gam = jnp.exp(cum)
            p["gam"] = gam
            gl_s[wslot, b] = jnp.broadcast_to(gam[C - 1:C, :], (8, RW_W))

        def s5():
            cum, gam, kk = p["cum"], p["gam"], p["kk"]
            gam_inv = jnp.exp(-cum)
            At = -kk * jnp.exp(cum - p["logdecay"])
            Rt = p["r"] * gam
            Bt = kk * p["a"] * gam_inv
            Kt = p["kmod"] * gam_inv
            ar_s[wslot, b] = jnp.concatenate([At, Rt], axis=0).astype(BF16)
            bk_s[wslot, b] = jnp.concatenate([Bt, Kt], axis=0).astype(BF16)

        return [s1, s2, s3, s4, s5]

    strict = strict_ref[...]
    incl = incl_ref[...]
    blk_p = blkp_ref[...]
    blk_s = blks_ref[...]
    chains = [dict(b=b, grp=grp, sl=slice(grp * GROUP_W, (grp + 1) * GROUP_W))
              for b in range(nb) for grp in range(N_GROUPS)]
    outs = {b: [None] * N_GROUPS for b in range(nb)}

    def c_load():
        for ch in chains:
            b, sl = ch["b"], ch["sl"]
            ch["AR"] = ar_s[rslot, b, :, sl]
            ch["BK"] = bk_s[rslot, b, :, sl]
            ch["V"] = v_s[rslot, b, :, sl]
            ch["BKe"] = jnp.concatenate([_expand(ch["BK"][:C], blk), _expand(ch["BK"][C:], blk)], axis=0)
            ch["Ve"] = _expand(ch["V"], blk)

    def c_gram():
        for ch in chains:
            G = _mm_nt(ch["AR"], ch["BKe"])
            ch["P"] = (G[:C, :R] * strict).astype(BF16)
            ch["Lak"] = (G[:C, R:] * strict).astype(BF16)
            ch["M"] = jnp.concatenate([G[C:, :R] * incl, G[C:, R:] * incl], axis=1).astype(BF16)
            ch["S"] = state[ch["b"], ch["grp"]]
            ch["P0"] = _mm_nt(ch["AR"], ch["S"])

    def c_rhs():
        for ch in chains:
            ch["X"] = ch["P0"][:C] + jnp.dot(ch["Lak"], ch["Ve"], preferred_element_type=F32)

    def c_solve(i):
        def step():
            for ch in chains:
                ch["X"] = ch["X"] + jnp.dot(ch["P"], _expand(ch["X"], blk), preferred_element_type=F32)
                if i + 1 < solve_steps:
                    if C == HEAD_DIM:
                        pe = _expand(ch["P"], blk)
                    else:
                        pe = jnp.concatenate([ch["P"]] * GROUP_HEADS, axis=0) * blk_p
                    ch["P"] = jnp.dot(ch["P"], pe, preferred_element_type=F32).astype(BF16)
        return step

    def c_out():
        for ch in chains:
            b, grp, sl = ch["b"], ch["grp"], ch["sl"]
            xb = ch["X"].astype(BF16)
            UVe = jnp.concatenate([_expand(xb, blk), ch["Ve"]], axis=0)
            outs[b][grp] = ch["P0"][C:] + jnp.dot(ch["M"], UVe, preferred_element_type=F32)
            UV = jnp.concatenate([xb, ch["V"]], axis=0)
            upd = _mm_tn(UV, ch["BK"]) * blk_s
            state[b, grp] = (ch["S"] + upd) * gl_s[rslot, b, 0:1, sl]

    def c_tail():
        for b in range(nb):
            O = jnp.concatenate(outs[b], axis=1)
            mean = _head_sums(O, ones_blk) * (1.0 / HEAD_DIM)
            d = O - mean
            var = _head_sums(d * d, ones_blk) * (1.0 / HEAD_DIM)
            on = d * lax.rsqrt(var + GN_EPS) * gn_w + gn_b
            o_ref[b] = (on + bonus_s[rslot, b]) * gate_s[rslot, b]

    chain = [c_load, c_gram, c_rhs] + [c_solve(i) for i in range(solve_steps)] + [c_out, c_tail]
    prep = [s for b in range(nb) for s in prep_stages(b)]
    if pipelined:
        done = 0
        for n, stage in enumerate(chain):
            stage()
            upto = (len(prep) * (n + 1)) // len(chain)
            for s in prep[done:upto]:
                s()
            done = upto
    else:
        for s in prep + chain:
            s()

    @pl.when(j == last)
    def _():
        for b in range(nb):
            for g in range(N_GROUPS):
                s = state[b, g]
                for h in range(GROUP_HEADS):
                    sout_ref[b, g * GROUP_HEADS + h] = s[h * HEAD_DIM:(h + 1) * HEAD_DIM,
                                                         h * HEAD_DIM:(h + 1) * HEAD_DIM]


def _rwkv_masks(chunk):
    R = GROUP_HEADS * chunk
    row_h = np.arange(R) // chunk
    lane_h = np.arange(GROUP_W) // HEAD_DIM
    blk_p = row_h[:, None] == row_h[None, :]
    blk_s = lane_h[:, None] == lane_h[None, :]
    t, s = np.arange(chunk)[:, None], (np.arange(R) % chunk)[None, :]
    return (jnp.asarray(blk_p, BF16), jnp.asarray(blk_s, F32),
            jnp.asarray(t > s, F32), jnp.asarray(t >= s, F32), jnp.asarray(blk_s, BF16))


def _rwkv(zr, zp_first, s0, mu, vecs, w2, a2, g2, *, chunk, n_valid, nb):
    b, t, _ = zr.shape
    nc = t // chunk
    pipelined = nc > 1
    lag = 1 if pipelined else 0
    solve_steps = max(1, (n_valid - 1).bit_length())
    masks = _rwkv_masks(chunk)
    full = lambda a: pl.BlockSpec(a.shape, lambda i, j: (0,) * a.ndim)
    return pl.pallas_call(
        functools.partial(_rwkv_body, chunk=chunk, n_valid=n_valid, solve_steps=solve_steps, nb=nb,
                          pipelined=pipelined),
        grid=(b // nb, nc + lag),
        in_specs=[pl.BlockSpec((nb, chunk, RW_IN_W), lambda i, j: (i, jnp.minimum(j, nc - 1), 0)),
                  pl.BlockSpec((nb, 1, RW_IN_W), lambda i, j: (i, 0, 0)),
                  pl.BlockSpec((nb, RW_HEADS, HEAD_DIM, HEAD_DIM), lambda i, j: (i, 0, 0, 0)),
                  full(mu), full(vecs), full(w2), full(a2), full(g2)] + [full(m) for m in masks],
        out_specs=[pl.BlockSpec((nb, chunk, RW_W), lambda i, j: (i, jnp.maximum(j - lag, 0), 0)),
                   pl.BlockSpec((nb, RW_HEADS, HEAD_DIM, HEAD_DIM), lambda i, j: (i, 0, 0, 0))],
        out_shape=[jax.ShapeDtypeStruct((b, t, RW_W), F32),
                   jax.ShapeDtypeStruct((b, RW_HEADS, HEAD_DIM, HEAD_DIM), F32)],
        scratch_shapes=[pltpu.VMEM((nb, N_GROUPS, GROUP_W, GROUP_W), F32),
                        pltpu.VMEM((nb, chunk + 8, RW_IN_W), F32),
                        pltpu.VMEM((2, nb, 2 * chunk, RW_W), BF16),
                        pltpu.VMEM((2, nb, 2 * chunk, RW_W), BF16),
                        pltpu.VMEM((2, nb, chunk, RW_W), BF16),
                        pltpu.VMEM((2, nb, 8, RW_W), F32),
                        pltpu.VMEM((2, nb, chunk, RW_W), F32),
                        pltpu.VMEM((2, nb, chunk, RW_W), F32)],
        compiler_params=_params(2),
        name="rwkv_chunk%d" % chunk,
    )(zr, zp_first, s0, mu, vecs, w2, a2, g2, *masks)


def _outffn_body(x_ref, att_ref, rw_ref, wo_ref, n2_ref, wg_ref, wu_ref, wd_ref, nf_ref, xo_ref, *, final):
    mix = jnp.concatenate([att_ref[...], rw_ref[...]], axis=-1).astype(BF16)
    x1 = x_ref[...] + jnp.dot(mix, wo_ref[...], preferred_element_type=F32)
    hb = _rmsnorm(x1, n2_ref[...]).astype(BF16)
    gate = jnp.dot(hb, wg_ref[...], preferred_element_type=F32)
    up = jnp.dot(hb, wu_ref[...], preferred_element_type=F32)
    act = (gate * jax.nn.sigmoid(gate) * up).astype(BF16)
    x2 = x1 + jnp.dot(act, wd_ref[...], preferred_element_type=F32)
    xo_ref[...] = _rmsnorm(x2, nf_ref[...]) if final else x2


def _outffn(x, att, rw, wo, n2, wg, wu, wd, nf, *, final, tm):
    n = x.shape[0]
    resident = lambda a: pl.BlockSpec(a.shape, lambda i: (0,) * a.ndim, pipeline_mode=pl.Buffered(1))
    return pl.pallas_call(
        functools.partial(_outffn_body, final=final),
        grid=(n // tm,),
        in_specs=[pl.BlockSpec((tm, D_MODEL), lambda i: (i, 0)),
                  pl.BlockSpec((tm, ATT_W), lambda i: (i, 0)),
                  pl.BlockSpec((tm, RW_W), lambda i: (i, 0)),
                  resident(wo), resident(n2), resident(wg), resident(wu), resident(wd), resident(nf)],
        out_specs=pl.BlockSpec((tm, D_MODEL), lambda i: (i, 0)),
        out_shape=jax.ShapeDtypeStruct((n, D_MODEL), F32),
        compiler_params=_params(1),
        name="outffn_final" if final else "outffn",
    )(x, att, rw, wo, n2, wg, wu, wd, nf)


def _t5_bucket_np(d):
    max_exact = NUM_BUCKETS // 2
    n = np.maximum(d, 0)
    large = max_exact + (np.log(np.maximum(n, 1) / max_exact) / np.log(MAX_DISTANCE / max_exact)
                         * (NUM_BUCKETS - max_exact)).astype(np.int32)
    large = np.minimum(large, NUM_BUCKETS - 1)
    return np.where(n < max_exact, n, large).astype(np.int32)


def _bias_body(tab_ref, bucket_ref, inside_ref, o_ref, *, keys_major):
    bucket = bucket_ref[...]
    inside = inside_ref[...] > 0
    for h in range(N_Q_HEADS):
        acc = jnp.zeros(bucket.shape, F32)
        for bk in range(NUM_BUCKETS):
            acc = jnp.where(bucket == bk, tab_ref[bk, h], acc)
        acc = jnp.where(inside, acc, NEG)
        if keys_major:
            kv, g, q = h // GQA, h % GQA, bucket.shape[1]
            o_ref[kv, :, g * q:(g + 1) * q] = acc
        else:
            o_ref[h] = acc


def _masked_bias(rel_table, d, inside, rows_per_head, keys_major=False):
    q, s = d.shape
    bucket = np.zeros((rows_per_head, s), np.int32)
    bucket[:q] = _t5_bucket_np(d)
    ins = np.zeros((rows_per_head, s), np.int32)
    ins[:q] = inside
    if keys_major:
        bucket, ins = bucket.T.copy(), ins.T.copy()
        out_shape = (N_KV_HEADS, s, GQA * rows_per_head)
    else:
        out_shape = (N_Q_HEADS, rows_per_head, s)
    vm = lambda shape: pl.BlockSpec(shape, lambda: (0,) * len(shape))
    b = pl.pallas_call(
        functools.partial(_bias_body, keys_major=keys_major),
        in_specs=[pl.BlockSpec(memory_space=pltpu.SMEM), vm(bucket.shape), vm(ins.shape)],
        out_specs=vm(out_shape),
        out_shape=jax.ShapeDtypeStruct(out_shape, F32),
        name="rel_bias",
    )(rel_table.astype(F32), jnp.asarray(bucket), jnp.asarray(ins))
    return b if keys_major else b.reshape(N_KV_HEADS, GQA * rows_per_head, s)


def _sink_rows(sink_l, rows_per_head):
    return jnp.repeat(sink_l.astype(F32).reshape(N_KV_HEADS, GQA, 1), rows_per_head, axis=1).reshape(
        N_KV_HEADS, GQA * rows_per_head, 1)


def kernel(x_prompt, x_sample, cache_k, cache_v, state_shift, state_wkv, norm1, w_in, w_out, sink,
           rel_table, mu, w0, w2, a0, a2, g2, k_k, k_a, r_k, gn_w, gn_b, norm2, w_gate, w_up,
           w_down, norm_f):
    depth = w_in.shape[0]
    B, T, _ = x_prompt.shape
    DB, S, _ = x_sample.shape
    wc = cache_k.shape[2]

    qi = np.arange(BLOCK)[:, None]
    sj = np.arange(2 * BLOCK)[None, :]
    dp = qi + BLOCK - sj
    inside_p = (dp >= 0) & (dp <= WINDOW)
    bias_p = jnp.stack([_masked_bias(rel_table, dp, inside_p & (sj >= BLOCK), BLOCK, keys_major=True),
                        _masked_bias(rel_table, dp, inside_p, BLOCK, keys_major=True)])
    ds_full = (wc + np.arange(S))[:, None] - np.arange(wc + S)[None, :]
    inside_s = (ds_full >= 0) & (ds_full <= WINDOW)
    bias_c = _masked_bias(rel_table, ds_full[:, :wc], inside_s[:, :wc], SAMPLE_PAD)
    dn = np.full((S, SAMPLE_PAD), -1, np.int64)
    dn[:, :S] = ds_full[:, wc:]
    inn = np.zeros((S, SAMPLE_PAD), bool)
    inn[:, :S] = inside_s[:, wc:]
    bias_n = _masked_bias(rel_table, dn, inn, SAMPLE_PAD)

    xp = x_prompt.reshape(B * T, D_MODEL)
    xs = x_sample.reshape(DB * S, D_MODEL)
    zeros_shift = jnp.zeros((B, 1, RW_IN_W), F32)
    zeros_state = jnp.zeros((B, RW_HEADS, HEAD_DIM, HEAD_DIM), F32)
    nf = norm_f.reshape(1, D_MODEL)
    ones = jnp.ones((1, D_MODEL), F32)

    kp, vp, sp, wp, ksl, vsl, ssl, wsl = [], [], [], [], [], [], [], []
    for l in range(depth):
        w_in_b = w_in[l].astype(BF16)
        wo_b, wg_b, wu_b, wd_b = (w_out[l].astype(BF16), w_gate[l].astype(BF16),
                                  w_up[l].astype(BF16), w_down[l].astype(BF16))
        n1 = norm1[l].reshape(1, D_MODEL)
        n2 = norm2[l].reshape(1, D_MODEL)
        mu_l = mu[l].reshape(1, RW_IN_W)
        vecs = jnp.stack([w0[l], a0[l], k_k[l], k_a[l], r_k[l], gn_w[l], gn_b[l], jnp.zeros_like(w0[l])])
        w2_b, a2_b, g2_b = w2[l].astype(BF16), a2[l].astype(BF16), g2[l].astype(BF16)
        final = l == depth - 1

        za, zr, h_last = _inproj(xp, n1, w_in_b, norm=True, tm=1024, seq_len=T)
        za3 = za.reshape(B, T, ATT_IN_W)
        w = min(WINDOW, T)
        kp.append(za3[:, T - w:, ATT_W:ATT_W + KV_W].reshape(B, w, N_KV_HEADS, HEAD_DIM))
        vp.append(za3[:, T - w:, ATT_W + KV_W:].reshape(B, w, N_KV_HEADS, HEAD_DIM))
        sp.append(h_last.reshape(B, D_MODEL))
        sink_lanes = jnp.swapaxes(_sink_rows(sink[l], BLOCK), 1, 2)
        att = _attn_prompt(za3, bias_p, sink_lanes, nblk=2)
        rw, wkv = _rwkv(zr.reshape(B, T, RW_IN_W), zeros_shift, zeros_state, mu_l, vecs, w2_b, a2_b, g2_b,
                        chunk=PROMPT_CHUNK, n_valid=PROMPT_CHUNK, nb=4)
        wp.append(wkv)
        xp = _outffn(xp, att.reshape(B * T, ATT_W), rw.reshape(B * T, RW_W), wo_b, n2, wg_b, wu_b, wd_b, nf,
                     final=final, tm=512)

        za_s, zr_s, h_s = _inproj(xs, n1, w_in_b, norm=True, tm=DB * S)
        _, zr_prev, _ = _inproj(state_shift[l], ones, w_in_b, norm=False, tm=DB)
        ssl.append(h_s.reshape(DB, S, D_MODEL)[:, -1])
        za_pad = jnp.pad(za_s.reshape(DB, S, ATT_IN_W), ((0, 0), (0, SAMPLE_PAD - S), (0, 0)))
        att_s, kb, vb = _attn_sample(za_pad, cache_k[l].reshape(DB, wc, KV_W), cache_v[l].reshape(DB, wc, KV_W),
                                     bias_c, bias_n, _sink_rows(sink[l], SAMPLE_PAD), n_new=S, ns=8)
        ksl.append(kb.reshape(DB, wc, N_KV_HEADS, HEAD_DIM))
        vsl.append(vb.reshape(DB, wc, N_KV_HEADS, HEAD_DIM))
        zr_pad = jnp.pad(zr_s.reshape(DB, S, RW_IN_W), ((0, 0), (0, SAMPLE_CHUNK - S), (0, 0)))
        rw_s, wkv_s = _rwkv(zr_pad, zr_prev.reshape(DB, 1, RW_IN_W), state_wkv[l], mu_l, vecs, w2_b, a2_b, g2_b,
                            chunk=SAMPLE_CHUNK, n_valid=S, nb=4)
        wsl.append(wkv_s)
        xs = _outffn(xs, att_s[:, :S].reshape(DB * S, ATT_W), rw_s[:, :S].reshape(DB * S, RW_W),
                     wo_b, n2, wg_b, wu_b, wd_b, nf, final=final, tm=DB * S)

    y_prompt = xp.reshape(B, T, D_MODEL)
    y_sample = xs.reshape(DB, S, D_MODEL)
    return (y_prompt, y_sample, jnp.stack(kp), jnp.stack(vp), jnp.stack(sp), jnp.stack(wp),
            jnp.stack(ksl), jnp.stack(vsl), jnp.stack(ssl), jnp.stack(wsl))
```

```python
import functools

import numpy as np
import jax
import jax.numpy as jnp
from jax import lax
from jax.experimental import pallas as pl
from jax.experimental.pallas import tpu as pltpu

D_MODEL = 1024
HEAD_DIM = 64
N_Q_HEADS = 8
N_KV_HEADS = 2
GQA = N_Q_HEADS // N_KV_HEADS
ATT_W = N_Q_HEADS * HEAD_DIM
KV_W = N_KV_HEADS * HEAD_DIM
WINDOW = 128
BLOCK = 128
NUM_BUCKETS = 32
MAX_DISTANCE = 128
RW_HEADS = 8
RW_W = RW_HEADS * HEAD_DIM
DECAY_LORA = 64
ICLR_LORA = 64
GATE_LORA = 128
RW_IN_W = 3 * RW_W + DECAY_LORA + ICLR_LORA + GATE_LORA
ATT_IN_W = ATT_W + 2 * KV_W
NORM_EPS = 1e-6
GN_EPS = 64e-5
NEG = -1e30
SCALE = HEAD_DIM ** -0.5

V7X_VMEM_LIMIT_BYTES = 56 * 1024 * 1024

PROMPT_CHUNK = 64
SAMPLE_CHUNK = 16
SAMPLE_PAD = 8

BF16 = jnp.bfloat16
F32 = jnp.float32


def _mm(a, b):
    return jnp.dot(a.astype(BF16), b.astype(BF16), preferred_element_type=F32)


def _mm_nt(a, b):
    return lax.dot_general(a.astype(BF16), b.astype(BF16), (((1,), (1,)), ((), ())),
                           preferred_element_type=F32)


def _rmsnorm(x, g):
    return x * lax.rsqrt(jnp.mean(x * x, -1, keepdims=True) + NORM_EPS) * g


def _params(n_axes):
    return pltpu.CompilerParams(dimension_semantics=("arbitrary",) * n_axes,
                                vmem_limit_bytes=V7X_VMEM_LIMIT_BYTES)


def _inproj_body(x_ref, g_ref, w_ref, za_ref, zr_ref, h_ref, *, norm, seq_len):
    x = x_ref[...]
    h = _rmsnorm(x, g_ref[...]) if norm else x
    if seq_len is None:
        h_ref[...] = h
    else:
        h_ref[0] = h[x.shape[0] - 1:, :]
    z = jnp.dot(h.astype(BF16), w_ref[...], preferred_element_type=F32)
    za_ref[...] = z[:, :ATT_IN_W]
    zr_ref[...] = z[:, ATT_IN_W:]


def _inproj(x, g, w, *, norm, tm, seq_len=None):
    n = x.shape[0]
    in_w = w.shape[1]
    if seq_len is None:
        h_spec = pl.BlockSpec((tm, D_MODEL), lambda i: (i, 0))
        h_shape = jax.ShapeDtypeStruct((n, D_MODEL), F32)
    else:
        tiles = seq_len // tm
        h_spec = pl.BlockSpec((1, 1, D_MODEL), lambda i: (i // tiles, 0, 0))
        h_shape = jax.ShapeDtypeStruct((n // seq_len, 1, D_MODEL), F32)
    return pl.pallas_call(
        functools.partial(_inproj_body, norm=norm, seq_len=seq_len),
        grid=(n // tm,),
        in_specs=[pl.BlockSpec((tm, D_MODEL), lambda i: (i, 0)),
                  pl.BlockSpec((1, D_MODEL), lambda i: (0, 0)),
                  pl.BlockSpec((D_MODEL, in_w), lambda i: (0, 0))],
        out_specs=[pl.BlockSpec((tm, ATT_IN_W), lambda i: (i, 0)),
                   pl.BlockSpec((tm, RW_IN_W), lambda i: (i, 0)),
                   h_spec],
        out_shape=[jax.ShapeDtypeStruct((n, ATT_IN_W), F32),
                   jax.ShapeDtypeStruct((n, RW_IN_W), F32),
                   h_shape],
        compiler_params=_params(1),
        name="inproj_norm" if norm else "inproj_plain",
    )(x, g, w)


def _sink_softmax(score_parts, sink_col):
    m = sink_col
    for s in score_parts:
        m = jnp.maximum(m, jnp.max(s, -1, keepdims=True))
    es = [jnp.exp(s - m) for s in score_parts]
    den = jnp.exp(sink_col - m)
    for e in es:
        den = den + jnp.sum(e, -1, keepdims=True)
    return [e / den for e in es]


QG_W = GQA * HEAD_DIM


def _tile_kv_heads(x):
    rot = pltpu.roll(x, HEAD_DIM, 1)
    first = lax.broadcasted_iota(jnp.int32, x.shape, 1) < HEAD_DIM
    halves = (jnp.where(first, x, rot).astype(BF16), jnp.where(first, rot, x).astype(BF16))
    return [jnp.concatenate([h] * (QG_W // KV_W), axis=1) for h in halves]


def _attn_prompt_body(q_ref, kv_ref, bias_ref, sink_ref, o_ref, kt_prev, vT_prev, *, nblk):
    j = pl.program_id(1)

    @pl.when(j == 0)
    def _():
        kt_prev[...] = jnp.zeros_like(kt_prev)
        vT_prev[...] = jnp.zeros_like(vT_prev)

    lane = lax.broadcasted_iota(jnp.int32, (1, 128), 1)
    lane_half = ((lane < HEAD_DIM).astype(BF16), (lane >= HEAD_DIM).astype(BF16))
    kts = [[kt_prev[kv] for kv in range(N_KV_HEADS)]]
    vTs = [vT_prev[...]]
    for i in range(nblk):
        rows = slice(i * BLOCK, (i + 1) * BLOCK)
        kts.append(_tile_kv_heads(kv_ref[0, rows, 0:KV_W]))
        vTs.append(kv_ref[0, rows, KV_W:2 * KV_W].T.astype(BF16))
    for kv in range(N_KV_HEADS):
        kt_prev[kv] = kts[nblk][kv]
    vT_prev[...] = vTs[nblk]

    units = []
    for i in range(nblk):
        qb = (q_ref[0, i * BLOCK:(i + 1) * BLOCK, :] * SCALE).astype(BF16)
        first = jnp.where(j == 0, 0, 1) if i == 0 else 1
        for kv in range(N_KV_HEADS):
            qe = _expand(qb[:, kv * QG_W:(kv + 1) * QG_W], lane_half)
            kcat = jnp.concatenate([kts[i][kv], kts[i + 1][kv]], axis=0)
            units.append(dict(i=i, kv=kv, s=_mm_nt(kcat, qe), first=first))
    for u in units:
        b = bias_ref[u["first"], u["kv"]]
        s = jnp.where(b > 0.5 * NEG, u["s"] + b, NEG)
        sink_row = sink_ref[u["kv"]]
        m = jnp.maximum(jnp.max(s, 0, keepdims=True), sink_row)
        e = jnp.exp(s - m)
        den = jnp.sum(e, 0, keepdims=True) + jnp.exp(sink_row - m)
        u["p"] = (e * (1.0 / den)).astype(BF16)
    pieces = [[None] * N_Q_HEADS for _ in range(nblk)]
    for u in units:
        i, kv = u["i"], u["kv"]
        hd = slice(kv * HEAD_DIM, (kv + 1) * HEAD_DIM)
        vcatT = jnp.concatenate([vTs[i][hd], vTs[i + 1][hd]], axis=1)
        oT = jnp.dot(vcatT, u["p"], preferred_element_type=F32)
        for g in range(GQA):
            pieces[i][kv * GQA + g] = oT[:, g * BLOCK:(g + 1) * BLOCK]
    for i in range(nblk):
        o_ref[0, i * BLOCK:(i + 1) * BLOCK, :] = jnp.concatenate(pieces[i], axis=0).T


def _attn_prompt(za, bias, sink_rows, *, nblk):
    b, t, _ = za.shape
    rows = nblk * BLOCK
    full = lambda a: pl.BlockSpec(a.shape, lambda i, j: (0,) * a.ndim)
    return pl.pallas_call(
        functools.partial(_attn_prompt_body, nblk=nblk),
        grid=(b, t // rows),
        in_specs=[pl.BlockSpec((1, rows, ATT_W), lambda i, j: (i, j, 0)),
                  pl.BlockSpec((1, rows, 2 * KV_W), lambda i, j: (i, j, ATT_W // (2 * KV_W))),
                  full(bias), full(sink_rows)],
        out_specs=pl.BlockSpec((1, rows, ATT_W), lambda i, j: (i, j, 0)),
        out_shape=jax.ShapeDtypeStruct((b, t, ATT_W), F32),
        scratch_shapes=[pltpu.VMEM((N_KV_HEADS, BLOCK, QG_W), BF16),
                        pltpu.VMEM((KV_W, BLOCK), BF16)],
        compiler_params=_params(2),
        name="attn_prompt",
    )(za, za, bias, sink_rows)


def _attn_sample_body(z_ref, ck_ref, cv_ref, bias_c_ref, bias_n_ref, sink_ref,
                      o_ref, kb_ref, vb_ref, *, n_new, ns):
    units = []
    for s in range(ns):
        z = z_ref[s]
        q = z[:, :ATT_W]
        kn = z[:, ATT_W:ATT_W + KV_W]
        vn = z[:, ATT_W + KV_W:]
        ck, cv = ck_ref[s], cv_ref[s]
        keep = WINDOW - n_new
        kb_ref[s, 0:keep, :] = ck_ref[s, n_new:WINDOW, :]
        kb_ref[s, keep:WINDOW, :] = kn[0:n_new]
        vb_ref[s, 0:keep, :] = cv_ref[s, n_new:WINDOW, :]
        vb_ref[s, keep:WINDOW, :] = vn[0:n_new]
        for kv in range(N_KV_HEADS):
            sl = slice(kv * HEAD_DIM, (kv + 1) * HEAD_DIM)
            qg = jnp.concatenate(
                [q[:, (kv * GQA + g) * HEAD_DIM:(kv * GQA + g + 1) * HEAD_DIM] for g in range(GQA)], axis=0)
            units.append(dict(s=s, kv=kv, qg=qg, ck=ck[:, sl], kn=kn[:, sl], cv=cv[:, sl], vn=vn[:, sl]))
    for u in units:
        u["sc"] = _mm_nt(u["qg"], u["ck"]) * SCALE
        u["sn"] = _mm_nt(u["qg"], u["kn"]) * SCALE
    for u in units:
        bc, bn = bias_c_ref[u["kv"]], bias_n_ref[u["kv"]]
        sc = jnp.where(bc > 0.5 * NEG, u["sc"] + bc, NEG)
        sn = jnp.where(bn > 0.5 * NEG, u["sn"] + bn, NEG)
        u["pc"], u["pn"] = _sink_softmax([sc, sn], sink_ref[u["kv"]])
    outs = [[] for _ in range(ns)]
    for u in units:
        o = _mm(u["pc"], u["cv"]) + _mm(u["pn"], u["vn"])
        outs[u["s"]] += [o[g * SAMPLE_PAD:(g + 1) * SAMPLE_PAD] for g in range(GQA)]
    for s in range(ns):
        o_ref[s] = jnp.concatenate(outs[s], axis=1)


def _attn_sample(za_pad, ck, cv, bias_c, bias_n, sink_rows, *, n_new, ns):
    db = za_pad.shape[0]
    full = lambda a: pl.BlockSpec(a.shape, lambda i: (0,) * a.ndim)
    return pl.pallas_call(
        functools.partial(_attn_sample_body, n_new=n_new, ns=ns),
        grid=(db // ns,),
        in_specs=[pl.BlockSpec((ns, SAMPLE_PAD, ATT_IN_W), lambda i: (i, 0, 0)),
                  pl.BlockSpec((ns, WINDOW, KV_W), lambda i: (i, 0, 0)),
                  pl.BlockSpec((ns, WINDOW, KV_W), lambda i: (i, 0, 0)),
                  full(bias_c), full(bias_n), full(sink_rows)],
        out_specs=[pl.BlockSpec((ns, SAMPLE_PAD, ATT_W), lambda i: (i, 0, 0)),
                   pl.BlockSpec((ns, WINDOW, KV_W), lambda i: (i, 0, 0)),
                   pl.BlockSpec((ns, WINDOW, KV_W), lambda i: (i, 0, 0))],
        out_shape=[jax.ShapeDtypeStruct((db, SAMPLE_PAD, ATT_W), F32),
                   jax.ShapeDtypeStruct((db, WINDOW, KV_W), F32),
                   jax.ShapeDtypeStruct((db, WINDOW, KV_W), F32)],
        compiler_params=_params(1),
        name="attn_sample",
    )(za_pad, ck, cv, bias_c, bias_n, sink_rows)


GROUP_HEADS = 4
GROUP_W = GROUP_HEADS * HEAD_DIM
N_GROUPS = RW_HEADS // GROUP_HEADS


def _head_sums(x, ones_blk):
    m = x.shape[0]
    hi = x.astype(BF16)
    lo = (x - hi.astype(F32)).astype(BF16)
    hl = jnp.concatenate([hi, lo], axis=0)
    y = jnp.concatenate(
        [jnp.dot(hl[:, g * GROUP_W:(g + 1) * GROUP_W], ones_blk, preferred_element_type=F32)
         for g in range(N_GROUPS)], axis=1)
    return y[:m] + y[m:]


def _expand(x, lane_half):
    xb = x.astype(BF16)
    c = xb.shape[0]
    zero = jnp.zeros((c, 128), BF16)
    blocks = []
    for h in range(GROUP_HEADS):
        col = h // 2
        piece = xb[:, col * 128:(col + 1) * 128] * lane_half[h % 2]
        blocks.append(jnp.concatenate([piece, zero] if col == 0 else [zero, piece], axis=1))
    return jnp.concatenate(blocks, axis=0)


def _pack_state(s, g):
    return jnp.concatenate([s[g * GROUP_HEADS + h] for h in range(GROUP_HEADS)], axis=1)


def _heads_to_lanes(x, c):
    heads = x.shape[0] // HEAD_DIM
    return jnp.concatenate([x[h * HEAD_DIM:(h + 1) * HEAD_DIM, :c] for h in range(heads)], axis=1)


def _rwkv_body(z_ref, zpf_ref, s0_ref, mu_ref, vec_ref, w2_ref, a2_ref, g2_ref,
               blkp_ref, strict_ref, incl_ref, ones_ref,
               o_ref, sout_ref, state, zbuf, ar_s, bk_s, vt_s, gl_s, bonus_s, gate_s,
               *, chunk, n_valid, solve_steps, nb, pipelined):
    C = chunk
    R = GROUP_HEADS * C
    j = pl.program_id(1)
    last = pl.num_programs(1) - 1
    wslot = j % 2 if pipelined else 0
    rslot = 1 - wslot if pipelined else 0

    @pl.when(j == 0)
    def _():
        for b in range(nb):
            s0 = s0_ref[b]
            for g in range(N_GROUPS):
                state[b, g] = _pack_state(s0, g)
            zbuf[b, 7:8, :] = zpf_ref[b]
        if pipelined:
            ar_s[1] = jnp.zeros(ar_s.shape[1:], BF16)
            bk_s[1] = jnp.zeros(bk_s.shape[1:], BF16)
            vt_s[1] = jnp.zeros(vt_s.shape[1:], BF16)
            gl_s[1] = jnp.ones(gl_s.shape[1:], F32)
            bonus_s[1] = jnp.zeros(bonus_s.shape[1:], F32)
            gate_s[1] = jnp.zeros(gate_s.shape[1:], F32)

    @pl.when(j > 0)
    def _():
        for b in range(nb):
            zbuf[b, 7:8, :] = zbuf[b, C + 7:C + 8, :]

    w0, a0, k_k, k_a = vec_ref[0:1, :], vec_ref[1:2, :], vec_ref[2:3, :], vec_ref[3:4, :]
    r_k, gn_w, gn_b = vec_ref[4:5, :], vec_ref[5:6, :], vec_ref[6:7, :]
    ones_blk = ones_ref[...]
    ri = lax.broadcasted_iota(jnp.int32, (C, C), 0)
    ci = lax.broadcasted_iota(jnp.int32, (C, C), 1)
    tri = (ri >= ci).astype(BF16)
    lane = lax.broadcasted_iota(jnp.int32, (1, 128), 1)
    blk = ((lane < HEAD_DIM).astype(BF16), (lane >= HEAD_DIM).astype(BF16))

    def prep_stages(b):
        p = {}

        def s1():
            zc = z_ref[b]
            zbuf[b, 8:8 + C, :] = zc
            zp = zbuf[b, 7:7 + C, :]
            zm = zc + (zp - zc) * mu_ref[...]
            p["r"] = zm[:, 0:RW_W]
            p["k"] = zm[:, RW_W:2 * RW_W]
            p["v"] = zm[:, 2 * RW_W:3 * RW_W]
            o1 = 3 * RW_W
            p["zw"] = zm[:, o1:o1 + DECAY_LORA]
            p["za"] = zm[:, o1 + DECAY_LORA:o1 + DECAY_LORA + ICLR_LORA]
            p["zg"] = zm[:, o1 + DECAY_LORA + ICLR_LORA:]

        def s2():
            y = -(w0 + _mm(jnp.tanh(p["zw"]), w2_ref[...]))
            softplus = jnp.maximum(y, 0.0) + jnp.log(1.0 + jnp.exp(-jnp.abs(y)))
            p["logdecay"] = -jnp.exp(-softplus - 0.5)
            p["a"] = jax.nn.sigmoid(a0 + _mm(p["za"], a2_ref[...]))
            gate_s[wslot, b] = _mm(jax.nn.sigmoid(p["zg"]), g2_ref[...])

        def s3():
            k, v, a = p["k"], p["v"], p["a"]
            kk_raw = k * k_k
            kmod = k * (1.0 + (a - 1.0) * k_a)
            vv = v
            if n_valid < C:
                rows = lax.broadcasted_iota(jnp.int32, (C, 1), 0) < n_valid
                p["logdecay"] = jnp.where(rows, p["logdecay"], 0.0)
                kk_raw = jnp.where(rows, kk_raw, 0.0)
                kmod = jnp.where(rows, kmod, 0.0)
                vv = jnp.where(rows, v, 0.0)
            vpad = jnp.concatenate([vv, jnp.zeros((128 - C, RW_W), F32)], axis=0) if C < 128 else vv
            vt_s[wslot, b] = _heads_to_lanes(vpad.T, C).astype(BF16)
            sums = _head_sums(jnp.concatenate([kk_raw * kk_raw, p["r"] * kmod * r_k], axis=0), ones_blk)
            p["kk"] = kk_raw / jnp.maximum(jnp.sqrt(sums[:C]), 1e-12)
            bonus_s[wslot, b] = sums[C:] * v
            p["kmod"] = kmod

        def s4():
            logdecay = p["logdecay"]
            hi = logdecay.astype(BF16)
            lo = (logdecay - hi.astype(F32)).astype(BF16)
            cum = jnp.dot(tri, jnp.concatenate([hi, lo], axis=1), preferred_element_type=F32)
            cum = cum[:, :RW_W] + cum[:, RW_W:]
            p["cum"] = cum
            gam = jnp.exp(cum)
            p["gam"] = gam
            gl_s[wslot, b] = jnp.broadcast_to(gam[C - 1:C, :], (8, RW_W))

        def s5():
            cum, gam, kk = p["cum"], p["gam"], p["kk"]
            gam_inv = jnp.exp(-cum)
            At = -kk * jnp.exp(cum - p["logdecay"])
            Rt = p["r"] * gam
            Bt = kk * p["a"] * gam_inv
            Kt = p["kmod"] * gam_inv
            ar_s[wslot, b] = jnp.concatenate([At, Rt], axis=0).astype(BF16)
            bk_s[wslot, b] = jnp.concatenate([Bt, Kt], axis=0).astype(BF16)

        return [s1, s2, s3, s4, s5]

    strict = strict_ref[...]
    incl = incl_ref[...]
    blk_p = blkp_ref[...]
    chains = [dict(b=b, grp=grp, sl=slice(grp * GROUP_W, (grp + 1) * GROUP_W),
                   slt=slice(grp * R, (grp + 1) * R))
              for b in range(nb) for grp in range(N_GROUPS)]
    on_parts = {b: [None] * N_GROUPS for b in range(nb)}

    def expand_t(x):
        if C == HEAD_DIM:
            return _expand(x, blk)
        return jnp.concatenate([x.astype(BF16)] * GROUP_HEADS, axis=0) * blk_p

    def c_load():
        for ch in chains:
            b, sl = ch["b"], ch["sl"]
            ar = ar_s[rslot, b, :, sl]
            ch["BK"] = bk_s[rslot, b, :, sl]
            ch["VT"] = vt_s[rslot, b, :, ch["slt"]]
            ch["ARe"] = jnp.concatenate([_expand(ar[:C], blk), _expand(ar[C:], blk)], axis=0)
            ch["BKe"] = jnp.concatenate([_expand(ch["BK"][:C], blk), _expand(ch["BK"][C:], blk)], axis=0)

    def c_gram():
        for ch in chains:
            ch["S"] = state[ch["b"], ch["grp"]]
            lhs = jnp.concatenate([ch["BK"], ch["S"].astype(BF16)], axis=0)
            GT = _mm_nt(lhs, ch["ARe"])
            ch["Q"] = (GT[:C, :R] * strict).astype(BF16)
            ch["LakT"] = (GT[C:2 * C, :R] * strict).astype(BF16)
            ch["MT"] = jnp.concatenate([expand_t(GT[:C, R:] * incl), expand_t(GT[C:2 * C, R:] * incl)], axis=0)
            ch["P0T"] = GT[2 * C:, :]

    def c_rhs():
        for ch in chains:
            ch["XT"] = ch["P0T"][:, :R] + jnp.dot(ch["VT"], expand_t(ch["LakT"]), preferred_element_type=F32)

    def c_solve(i):
        def step():
            for ch in chains:
                qe = expand_t(ch["Q"])
                xb = ch["XT"].astype(BF16)
                if i + 1 < solve_steps:
                    res = jnp.dot(jnp.concatenate([xb, ch["Q"]], axis=0), qe, preferred_element_type=F32)
                    ch["XT"] = ch["XT"] + res[:HEAD_DIM]
                    ch["Q"] = res[HEAD_DIM:].astype(BF16)
                else:
                    ch["XT"] = ch["XT"] + jnp.dot(xb, qe, preferred_element_type=F32)
        return step

    def c_out():
        for ch in chains:
            b, grp, sl = ch["b"], ch["grp"], ch["sl"]
            uv = jnp.concatenate([ch["XT"].astype(BF16), ch["VT"]], axis=1)
            rhs = jnp.concatenate([ch["MT"], ch["BKe"]], axis=1)
            res = jnp.dot(uv, rhs, preferred_element_type=F32)
            state[b, grp] = (ch["S"] + res[:, R:]) * gl_s[rslot, b, 0:1, sl]
            OT = ch["P0T"][:, R:] + res[:, :R]
            mean = jnp.mean(OT, 0, keepdims=True)
            d = OT - mean
            var = jnp.mean(d * d, 0, keepdims=True)
            onT = d * lax.rsqrt(var + GN_EPS)
            rows = jnp.concatenate([onT[:, h * C:(h + 1) * C] for h in range(GROUP_HEADS)], axis=0)
            if C < 128:
                rows = jnp.concatenate([rows, jnp.zeros((GROUP_W, 128 - C), F32)], axis=1)
            on_parts[b][grp] = rows.T[:C]

    def c_tail():
        for b in range(nb):
            on = jnp.concatenate(on_parts[b], axis=1)
            o_ref[b] = (on * gn_w + gn_b + bonus_s[rslot, b]) * gate_s[rslot, b]

    chain = [c_load, c_gram, c_rhs] + [c_solve(i) for i in range(solve_steps)] + [c_out, c_tail]
    prep = [s for b in range(nb) for s in prep_stages(b)]
    if pipelined:
        done = 0
        for n, stage in enumerate(chain):
            stage()
            upto = (len(prep) * (n + 1)) // len(chain)
            for s in prep[done:upto]:
                s()
            done = upto
    else:
        for s in prep + chain:
            s()

    @pl.when(j == last)
    def _():
        for b in range(nb):
            for g in range(N_GROUPS):
                s = state[b, g]
                for h in range(GROUP_HEADS):
                    sout_ref[b, g * GROUP_HEADS + h] = s[:, h * HEAD_DIM:(h + 1) * HEAD_DIM]


def _rwkv_masks(chunk):
    R = GROUP_HEADS * chunk
    row_h = np.arange(R) // chunk
    lane_h = np.arange(GROUP_W) // HEAD_DIM
    blk_p = row_h[:, None] == row_h[None, :]
    ones_blk = lane_h[:, None] == lane_h[None, :]
    s, t = np.arange(chunk)[:, None], (np.arange(R) % chunk)[None, :]
    return (jnp.asarray(blk_p, BF16), jnp.asarray(t > s, F32), jnp.asarray(t >= s, F32),
            jnp.asarray(ones_blk, BF16))


def _rwkv(zr, zp_first, s0, mu, vecs, w2, a2, g2, *, chunk, n_valid, nb):
    b, t, _ = zr.shape
    nc = t // chunk
    pipelined = nc > 1
    lag = 1 if pipelined else 0
    solve_steps = max(1, (n_valid - 1).bit_length())
    masks = _rwkv_masks(chunk)
    full = lambda a: pl.BlockSpec(a.shape, lambda i, j: (0,) * a.ndim)
    return pl.pallas_call(
        functools.partial(_rwkv_body, chunk=chunk, n_valid=n_valid, solve_steps=solve_steps, nb=nb,
                          pipelined=pipelined),
        grid=(b // nb, nc + lag),
        in_specs=[pl.BlockSpec((nb, chunk, RW_IN_W), lambda i, j: (i, jnp.minimum(j, nc - 1), 0)),
                  pl.BlockSpec((nb, 1, RW_IN_W), lambda i, j: (i, 0, 0)),
                  pl.BlockSpec((nb, RW_HEADS, HEAD_DIM, HEAD_DIM), lambda i, j: (i, 0, 0, 0)),
                  full(mu), full(vecs), full(w2), full(a2), full(g2)] + [full(m) for m in masks],
        out_specs=[pl.BlockSpec((nb, chunk, RW_W), lambda i, j: (i, jnp.maximum(j - lag, 0), 0)),
                   pl.BlockSpec((nb, RW_HEADS, HEAD_DIM, HEAD_DIM), lambda i, j: (i, 0, 0, 0))],
        out_shape=[jax.ShapeDtypeStruct((b, t, RW_W), F32),
                   jax.ShapeDtypeStruct((b, RW_HEADS, HEAD_DIM, HEAD_DIM), F32)],
        scratch_shapes=[pltpu.VMEM((nb, N_GROUPS, HEAD_DIM, GROUP_W), F32),
                        pltpu.VMEM((nb, chunk + 8, RW_IN_W), F32),
                        pltpu.VMEM((2, nb, 2 * chunk, RW_W), BF16),
                        pltpu.VMEM((2, nb, 2 * chunk, RW_W), BF16),
                        pltpu.VMEM((2, nb, HEAD_DIM, RW_HEADS * chunk), BF16),
                        pltpu.VMEM((2, nb, 8, RW_W), F32),
                        pltpu.VMEM((2, nb, chunk, RW_W), F32),
                        pltpu.VMEM((2, nb, chunk, RW_W), F32)],
        compiler_params=_params(2),
        name="rwkv_chunk%d" % chunk,
    )(zr, zp_first, s0, mu, vecs, w2, a2, g2, *masks)


def _outffn_body(x_ref, att_ref, rw_ref, wo_ref, n2_ref, wg_ref, wu_ref, wd_ref, nf_ref, xo_ref, *, final):
    mix = jnp.concatenate([att_ref[...], rw_ref[...]], axis=-1).astype(BF16)
    x1 = x_ref[...] + jnp.dot(mix, wo_ref[...], preferred_element_type=F32)
    hb = _rmsnorm(x1, n2_ref[...]).astype(BF16)
    gate = jnp.dot(hb, wg_ref[...], preferred_element_type=F32)
    up = jnp.dot(hb, wu_ref[...], preferred_element_type=F32)
    act = (gate * jax.nn.sigmoid(gate) * up).astype(BF16)
    x2 = x1 + jnp.dot(act, wd_ref[...], preferred_element_type=F32)
    xo_ref[...] = _rmsnorm(x2, nf_ref[...]) if final else x2


def _outffn(x, att, rw, wo, n2, wg, wu, wd, nf, *, final, tm):
    n = x.shape[0]
    resident = lambda a: pl.BlockSpec(a.shape, lambda i: (0,) * a.ndim, pipeline_mode=pl.Buffered(1))
    return pl.pallas_call(
        functools.partial(_outffn_body, final=final),
        grid=(n // tm,),
        in_specs=[pl.BlockSpec((tm, D_MODEL), lambda i: (i, 0)),
                  pl.BlockSpec((tm, ATT_W), lambda i: (i, 0)),
                  pl.BlockSpec((tm, RW_W), lambda i: (i, 0)),
                  resident(wo), resident(n2), resident(wg), resident(wu), resident(wd), resident(nf)],
        out_specs=pl.BlockSpec((tm, D_MODEL), lambda i: (i, 0)),
        out_shape=jax.ShapeDtypeStruct((n, D_MODEL), F32),
        compiler_params=_params(1),
        name="outffn_final" if final else "outffn",
    )(x, att, rw, wo, n2, wg, wu, wd, nf)


def _t5_bucket_np(d):
    max_exact = NUM_BUCKETS // 2
    n = np.maximum(d, 0)
    large = max_exact + (np.log(np.maximum(n, 1) / max_exact) / np.log(MAX_DISTANCE / max_exact)
                         * (NUM_BUCKETS - max_exact)).astype(np.int32)
    large = np.minimum(large, NUM_BUCKETS - 1)
    return np.where(n < max_exact, n, large).astype(np.int32)


def _bias_body(tab_ref, bucket_ref, inside_ref, o_ref, *, keys_major):
    bucket = bucket_ref[...]
    inside = inside_ref[...] > 0
    for h in range(N_Q_HEADS):
        acc = jnp.zeros(bucket.shape, F32)
        for bk in range(NUM_BUCKETS):
            acc = jnp.where(bucket == bk, tab_ref[bk, h], acc)
        acc = jnp.where(inside, acc, NEG)
        if keys_major:
            kv, g, q = h // GQA, h % GQA, bucket.shape[1]
            o_ref[kv, :, g * q:(g + 1) * q] = acc
        else:
            o_ref[h] = acc


def _masked_bias(rel_table, d, inside, rows_per_head, keys_major=False):
    q, s = d.shape
    bucket = np.zeros((rows_per_head, s), np.int32)
    bucket[:q] = _t5_bucket_np(d)
    ins = np.zeros((rows_per_head, s), np.int32)
    ins[:q] = inside
    if keys_major:
        bucket, ins = bucket.T.copy(), ins.T.copy()
        out_shape = (N_KV_HEADS, s, GQA * rows_per_head)
    else:
        out_shape = (N_Q_HEADS, rows_per_head, s)
    vm = lambda shape: pl.BlockSpec(shape, lambda: (0,) * len(shape))
    b = pl.pallas_call(
        functools.partial(_bias_body, keys_major=keys_major),
        in_specs=[pl.BlockSpec(memory_space=pltpu.SMEM), vm(bucket.shape), vm(ins.shape)],
        out_specs=vm(out_shape),
        out_shape=jax.ShapeDtypeStruct(out_shape, F32),
        name="rel_bias",
    )(rel_table.astype(F32), jnp.asarray(bucket), jnp.asarray(ins))
    return b if keys_major else b.reshape(N_KV_HEADS, GQA * rows_per_head, s)


def _sink_rows(sink_l, rows_per_head):
    return jnp.repeat(sink_l.astype(F32).reshape(N_KV_HEADS, GQA, 1), rows_per_head, axis=1).reshape(
        N_KV_HEADS, GQA * rows_per_head, 1)


def kernel(x_prompt, x_sample, cache_k, cache_v, state_shift, state_wkv, norm1, w_in, w_out, sink,
           rel_table, mu, w0, w2, a0, a2, g2, k_k, k_a, r_k, gn_w, gn_b, norm2, w_gate, w_up,
           w_down, norm_f):
    depth = w_in.shape[0]
    B, T, _ = x_prompt.shape
    DB, S, _ = x_sample.shape
    wc = cache_k.shape[2]

    qi = np.arange(BLOCK)[:, None]
    sj = np.arange(2 * BLOCK)[None, :]
    dp = qi + BLOCK - sj
    inside_p = (dp >= 0) & (dp <= WINDOW)
    bias_p = jnp.stack([_masked_bias(rel_table, dp, inside_p & (sj >= BLOCK), BLOCK, keys_major=True),
                        _masked_bias(rel_table, dp, inside_p, BLOCK, keys_major=True)])
    ds_full = (wc + np.arange(S))[:, None] - np.arange(wc + S)[None, :]
    inside_s = (ds_full >= 0) & (ds_full <= WINDOW)
    bias_c = _masked_bias(rel_table, ds_full[:, :wc], inside_s[:, :wc], SAMPLE_PAD)
    dn = np.full((S, SAMPLE_PAD), -1, np.int64)
    dn[:, :S] = ds_full[:, wc:]
    inn = np.zeros((S, SAMPLE_PAD), bool)
    inn[:, :S] = inside_s[:, wc:]
    bias_n = _masked_bias(rel_table, dn, inn, SAMPLE_PAD)

    xp = x_prompt.reshape(B * T, D_MODEL)
    xs = x_sample.reshape(DB * S, D_MODEL)
    zeros_shift = jnp.zeros((B, 1, RW_IN_W), F32)
    zeros_state = jnp.zeros((B, RW_HEADS, HEAD_DIM, HEAD_DIM), F32)
    nf = norm_f.reshape(1, D_MODEL)
    ones = jnp.ones((1, D_MODEL), F32)

    kp, vp, sp, wp, ksl, vsl, ssl, wsl = [], [], [], [], [], [], [], []
    for l in range(depth):
        w_in_b = w_in[l].astype(BF16)
        wo_b, wg_b, wu_b, wd_b = (w_out[l].astype(BF16), w_gate[l].astype(BF16),
                                  w_up[l].astype(BF16), w_down[l].astype(BF16))
        n1 = norm1[l].reshape(1, D_MODEL)
        n2 = norm2[l].reshape(1, D_MODEL)
        mu_l = mu[l].reshape(1, RW_IN_W)
        vecs = jnp.stack([w0[l], a0[l], k_k[l], k_a[l], r_k[l], gn_w[l], gn_b[l], jnp.zeros_like(w0[l])])
        w2_b, a2_b, g2_b = w2[l].astype(BF16), a2[l].astype(BF16), g2[l].astype(BF16)
        final = l == depth - 1

        za, zr, h_last = _inproj(xp, n1, w_in_b, norm=True, tm=1024, seq_len=T)
        za3 = za.reshape(B, T, ATT_IN_W)
        w = min(WINDOW, T)
        kp.append(za3[:, T - w:, ATT_W:ATT_W + KV_W].reshape(B, w, N_KV_HEADS, HEAD_DIM))
        vp.append(za3[:, T - w:, ATT_W + KV_W:].reshape(B, w, N_KV_HEADS, HEAD_DIM))
        sp.append(h_last.reshape(B, D_MODEL))
        sink_lanes = jnp.swapaxes(_sink_rows(sink[l], BLOCK), 1, 2)
        att = _attn_prompt(za3, bias_p, sink_lanes, nblk=2)
        rw, wkv = _rwkv(zr.reshape(B, T, RW_IN_W), zeros_shift, zeros_state, mu_l, vecs, w2_b, a2_b, g2_b,
                        chunk=PROMPT_CHUNK, n_valid=PROMPT_CHUNK, nb=4)
        wp.append(wkv)
        xp = _outffn(xp, att.reshape(B * T, ATT_W), rw.reshape(B * T, RW_W), wo_b, n2, wg_b, wu_b, wd_b, nf,
                     final=final, tm=512)

        za_s, zr_s, h_s = _inproj(xs, n1, w_in_b, norm=True, tm=DB * S)
        _, zr_prev, _ = _inproj(state_shift[l], ones, w_in_b, norm=False, tm=DB)
        ssl.append(h_s.reshape(DB, S, D_MODEL)[:, -1])
        za_pad = jnp.pad(za_s.reshape(DB, S, ATT_IN_W), ((0, 0), (0, SAMPLE_PAD - S), (0, 0)))
        att_s, kb, vb = _attn_sample(za_pad, cache_k[l].reshape(DB, wc, KV_W), cache_v[l].reshape(DB, wc, KV_W),
                                     bias_c, bias_n, _sink_rows(sink[l], SAMPLE_PAD), n_new=S, ns=8)
        ksl.append(kb.reshape(DB, wc, N_KV_HEADS, HEAD_DIM))
        vsl.append(vb.reshape(DB, wc, N_KV_HEADS, HEAD_DIM))
        zr_pad = jnp.pad(zr_s.reshape(DB, S, RW_IN_W), ((0, 0), (0, SAMPLE_CHUNK - S), (0, 0)))
        rw_s, wkv_s = _rwkv(zr_pad, zr_prev.reshape(DB, 1, RW_IN_W), state_wkv[l], mu_l, vecs, w2_b, a2_b, g2_b,
                            chunk=SAMPLE_CHUNK, n_valid=S, nb=4)
        wsl.append(wkv_s)
        xs = _outffn(xs, att_s[:, :S].reshape(DB * S, ATT_W), rw_s[:, :S].reshape(DB * S, RW_W),
                     wo_b, n2, wg_b, wu_b, wd_b, nf, final=final, tm=DB * S)

    y_prompt = xp.reshape(B, T, D_MODEL)
    y_sample = xs.reshape(DB, S, D_MODEL)
    return (y_prompt, y_sample, jnp.stack(kp), jnp.stack(vp), jnp.stack(sp), jnp.stack(wp),
            jnp.stack(ksl), jnp.stack(vsl), jnp.stack(ssl), jnp.stack(wsl))
```

```python
import functools

import numpy as np
import jax
import jax.numpy as jnp
from jax import lax
from jax.experimental import pallas as pl
from jax.experimental.pallas import tpu as pltpu

D_MODEL = 1024
HEAD_DIM = 64
N_Q_HEADS = 8
N_KV_HEADS = 2
GQA = N_Q_HEADS // N_KV_HEADS
ATT_W = N_Q_HEADS * HEAD_DIM
KV_W = N_KV_HEADS * HEAD_DIM
WINDOW = 128
BLOCK = 128
NUM_BUCKETS = 32
MAX_DISTANCE = 128
RW_HEADS = 8
RW_W = RW_HEADS * HEAD_DIM
DECAY_LORA = 64
ICLR_LORA = 64
GATE_LORA = 128
RW_IN_W = 3 * RW_W + DECAY_LORA + ICLR_LORA + GATE_LORA
ATT_IN_W = ATT_W + 2 * KV_W
NORM_EPS = 1e-6
GN_EPS = 64e-5
NEG = -1e30
SCALE = HEAD_DIM ** -0.5

V7X_VMEM_LIMIT_BYTES = 56 * 1024 * 1024

PROMPT_CHUNK = 64
SAMPLE_CHUNK = 16
SAMPLE_PAD = 8

BF16 = jnp.bfloat16
F32 = jnp.float32


def _mm(a, b):
    return jnp.dot(a.astype(BF16), b.astype(BF16), preferred_element_type=F32)


def _mm_nt(a, b):
    return lax.dot_general(a.astype(BF16), b.astype(BF16), (((1,), (1,)), ((), ())),
                           preferred_element_type=F32)


def _rmsnorm(x, g):
    return x * lax.rsqrt(jnp.mean(x * x, -1, keepdims=True) + NORM_EPS) * g


def _params(n_axes):
    return pltpu.CompilerParams(dimension_semantics=("arbitrary",) * n_axes,
                                vmem_limit_bytes=V7X_VMEM_LIMIT_BYTES)


def _inproj_body(x_ref, g_ref, w_ref, za_ref, zr_ref, h_ref, *tail_refs, norm, seq_len):
    x = x_ref[...]
    h = _rmsnorm(x, g_ref[...]) if norm else x
    z = jnp.dot(h.astype(BF16), w_ref[...], preferred_element_type=F32)
    za_ref[...] = z[:, :ATT_IN_W]
    zr_ref[...] = z[:, ATT_IN_W:]
    if seq_len is None:
        h_ref[...] = h
    else:
        (kv_ref,) = tail_refs
        tm = x.shape[0]
        h_ref[0] = h[tm - 1:, :]
        kv_ref[0] = z[tm - WINDOW:, ATT_W:ATT_IN_W]


def _inproj(x, g, w, *, norm, tm, seq_len=None):
    n = x.shape[0]
    in_w = w.shape[1]
    if seq_len is None:
        tail_specs = [pl.BlockSpec((tm, D_MODEL), lambda i: (i, 0))]
        tail_shapes = [jax.ShapeDtypeStruct((n, D_MODEL), F32)]
    else:
        tiles = seq_len // tm
        tail_specs = [pl.BlockSpec((1, 1, D_MODEL), lambda i: (i // tiles, 0, 0)),
                      pl.BlockSpec((1, WINDOW, 2 * KV_W), lambda i: (i // tiles, 0, 0))]
        tail_shapes = [jax.ShapeDtypeStruct((n // seq_len, 1, D_MODEL), F32),
                       jax.ShapeDtypeStruct((n // seq_len, WINDOW, 2 * KV_W), F32)]
    return pl.pallas_call(
        functools.partial(_inproj_body, norm=norm, seq_len=seq_len),
        grid=(n // tm,),
        in_specs=[pl.BlockSpec((tm, D_MODEL), lambda i: (i, 0)),
                  pl.BlockSpec((1, D_MODEL), lambda i: (0, 0)),
                  pl.BlockSpec((D_MODEL, in_w), lambda i: (0, 0))],
        out_specs=[pl.BlockSpec((tm, ATT_IN_W), lambda i: (i, 0)),
                   pl.BlockSpec((tm, RW_IN_W), lambda i: (i, 0))] + tail_specs,
        out_shape=[jax.ShapeDtypeStruct((n, ATT_IN_W), F32),
                   jax.ShapeDtypeStruct((n, RW_IN_W), F32)] + tail_shapes,
        compiler_params=_params(1),
        name="inproj_norm" if norm else "inproj_plain",
    )(x, g, w)


def _sink_softmax(score_parts, sink_col):
    m = sink_col
    for s in score_parts:
        m = jnp.maximum(m, jnp.max(s, -1, keepdims=True))
    es = [jnp.exp(s - m) for s in score_parts]
    den = jnp.exp(sink_col - m)
    for e in es:
        den = den + jnp.sum(e, -1, keepdims=True)
    return [e / den for e in es]


QG_W = GQA * HEAD_DIM


def _tile_kv_heads(x):
    rot = pltpu.roll(x, HEAD_DIM, 1)
    first = lax.broadcasted_iota(jnp.int32, x.shape, 1) < HEAD_DIM
    halves = (jnp.where(first, x, rot).astype(BF16), jnp.where(first, rot, x).astype(BF16))
    return [jnp.concatenate([h] * (QG_W // KV_W), axis=1) for h in halves]


def _attn_prompt_body(q_ref, kv_ref, bias_ref, sink_ref, o_ref, kt_prev, vT_prev, *, nblk):
    j = pl.program_id(1)

    @pl.when(j == 0)
    def _():
        kt_prev[...] = jnp.zeros_like(kt_prev)
        vT_prev[...] = jnp.zeros_like(vT_prev)

    lane = lax.broadcasted_iota(jnp.int32, (1, 128), 1)
    lane_half = ((lane < HEAD_DIM).astype(BF16), (lane >= HEAD_DIM).astype(BF16))
    kts = [[kt_prev[kv] for kv in range(N_KV_HEADS)]]
    vTs = [vT_prev[...]]
    for i in range(nblk):
        rows = slice(i * BLOCK, (i + 1) * BLOCK)
        kts.append(_tile_kv_heads(kv_ref[0, rows, 0:KV_W]))
        vTs.append(kv_ref[0, rows, KV_W:2 * KV_W].T.astype(BF16))
    for kv in range(N_KV_HEADS):
        kt_prev[kv] = kts[nblk][kv]
    vT_prev[...] = vTs[nblk]

    units = []
    for i in range(nblk):
        qb = (q_ref[0, i * BLOCK:(i + 1) * BLOCK, :] * SCALE).astype(BF16)
        first = jnp.where(j == 0, 0, 1) if i == 0 else 1
        for kv in range(N_KV_HEADS):
            qe = _expand(qb[:, kv * QG_W:(kv + 1) * QG_W], lane_half)
            kcat = jnp.concatenate([kts[i][kv], kts[i + 1][kv]], axis=0)
            units.append(dict(i=i, kv=kv, s=_mm_nt(kcat, qe), first=first))
    for u in units:
        b = bias_ref[u["first"], u["kv"]]
        s = jnp.where(b > 0.5 * NEG, u["s"] + b, NEG)
        sink_row = sink_ref[u["kv"]]
        m = jnp.maximum(jnp.max(s, 0, keepdims=True), sink_row)
        e = jnp.exp(s - m)
        den = jnp.sum(e, 0, keepdims=True) + jnp.exp(sink_row - m)
        u["p"] = (e * (1.0 / den)).astype(BF16)
    pieces = [[None] * N_Q_HEADS for _ in range(nblk)]
    for u in units:
        i, kv = u["i"], u["kv"]
        hd = slice(kv * HEAD_DIM, (kv + 1) * HEAD_DIM)
        vcatT = jnp.concatenate([vTs[i][hd], vTs[i + 1][hd]], axis=1)
        oT = jnp.dot(vcatT, u["p"], preferred_element_type=F32)
        for g in range(GQA):
            pieces[i][kv * GQA + g] = oT[:, g * BLOCK:(g + 1) * BLOCK]
    for i in range(nblk):
        o_ref[0, i * BLOCK:(i + 1) * BLOCK, :] = jnp.concatenate(pieces[i], axis=0).T


def _attn_prompt(za, bias, sink_rows, *, nblk):
    b, t, _ = za.shape
    rows = nblk * BLOCK
    full = lambda a: pl.BlockSpec(a.shape, lambda i, j: (0,) * a.ndim)
    return pl.pallas_call(
        functools.partial(_attn_prompt_body, nblk=nblk),
        grid=(b, t // rows),
        in_specs=[pl.BlockSpec((1, rows, ATT_W), lambda i, j: (i, j, 0)),
                  pl.BlockSpec((1, rows, 2 * KV_W), lambda i, j: (i, j, ATT_W // (2 * KV_W))),
                  full(bias), full(sink_rows)],
        out_specs=pl.BlockSpec((1, rows, ATT_W), lambda i, j: (i, j, 0)),
        out_shape=jax.ShapeDtypeStruct((b, t, ATT_W), F32),
        scratch_shapes=[pltpu.VMEM((N_KV_HEADS, BLOCK, QG_W), BF16),
                        pltpu.VMEM((KV_W, BLOCK), BF16)],
        compiler_params=_params(2),
        name="attn_prompt",
    )(za, za, bias, sink_rows)


def _attn_sample_body(z_ref, ck_ref, cv_ref, bias_c_ref, bias_n_ref, sink_ref,
                      o_ref, kb_ref, vb_ref, *, n_new, ns):
    units = []
    for s in range(ns):
        z = jnp.concatenate([z_ref[s], jnp.zeros((SAMPLE_PAD - n_new, ATT_IN_W), F32)], axis=0)
        q = z[:, :ATT_W]
        kn = z[:, ATT_W:ATT_W + KV_W]
        vn = z[:, ATT_W + KV_W:]
        ck, cv = ck_ref[s], cv_ref[s]
        keep = WINDOW - n_new
        kb_ref[s, 0:keep, :] = ck_ref[s, n_new:WINDOW, :]
        kb_ref[s, keep:WINDOW, :] = kn[0:n_new]
        vb_ref[s, 0:keep, :] = cv_ref[s, n_new:WINDOW, :]
        vb_ref[s, keep:WINDOW, :] = vn[0:n_new]
        for kv in range(N_KV_HEADS):
            sl = slice(kv * HEAD_DIM, (kv + 1) * HEAD_DIM)
            qg = jnp.concatenate(
                [q[:, (kv * GQA + g) * HEAD_DIM:(kv * GQA + g + 1) * HEAD_DIM] for g in range(GQA)], axis=0)
            units.append(dict(s=s, kv=kv, qg=qg, ck=ck[:, sl], kn=kn[:, sl], cv=cv[:, sl], vn=vn[:, sl]))
    for u in units:
        u["sc"] = _mm_nt(u["qg"], u["ck"]) * SCALE
        u["sn"] = _mm_nt(u["qg"], u["kn"]) * SCALE
    for u in units:
        bc, bn = bias_c_ref[u["kv"]], bias_n_ref[u["kv"]]
        sc = jnp.where(bc > 0.5 * NEG, u["sc"] + bc, NEG)
        sn = jnp.where(bn > 0.5 * NEG, u["sn"] + bn, NEG)
        u["pc"], u["pn"] = _sink_softmax([sc, sn], sink_ref[u["kv"]])
    outs = [[] for _ in range(ns)]
    for u in units:
        o = _mm(u["pc"], u["cv"]) + _mm(u["pn"], u["vn"])
        outs[u["s"]] += [o[g * SAMPLE_PAD:(g + 1) * SAMPLE_PAD] for g in range(GQA)]
    for s in range(ns):
        o_ref[s] = jnp.concatenate(outs[s], axis=1)[:n_new]


def _attn_sample(za, ck, cv, bias_c, bias_n, sink_rows, *, ns):
    db, n_new, _ = za.shape
    full = lambda a: pl.BlockSpec(a.shape, lambda i: (0,) * a.ndim)
    return pl.pallas_call(
        functools.partial(_attn_sample_body, n_new=n_new, ns=ns),
        grid=(db // ns,),
        in_specs=[pl.BlockSpec((ns, n_new, ATT_IN_W), lambda i: (i, 0, 0)),
                  pl.BlockSpec((ns, WINDOW, KV_W), lambda i: (i, 0, 0)),
                  pl.BlockSpec((ns, WINDOW, KV_W), lambda i: (i, 0, 0)),
                  full(bias_c), full(bias_n), full(sink_rows)],
        out_specs=[pl.BlockSpec((ns, n_new, ATT_W), lambda i: (i, 0, 0)),
                   pl.BlockSpec((ns, WINDOW, KV_W), lambda i: (i, 0, 0)),
                   pl.BlockSpec((ns, WINDOW, KV_W), lambda i: (i, 0, 0))],
        out_shape=[jax.ShapeDtypeStruct((db, n_new, ATT_W), F32),
                   jax.ShapeDtypeStruct((db, WINDOW, KV_W), F32),
                   jax.ShapeDtypeStruct((db, WINDOW, KV_W), F32)],
        compiler_params=_params(1),
        name="attn_sample",
    )(za, ck, cv, bias_c, bias_n, sink_rows)


GROUP_HEADS = 4
GROUP_W = GROUP_HEADS * HEAD_DIM
N_GROUPS = RW_HEADS // GROUP_HEADS


def _head_sums(x, ones_blk):
    m = x.shape[0]
    hi = x.astype(BF16)
    lo = (x - hi.astype(F32)).astype(BF16)
    hl = jnp.concatenate([hi, lo], axis=0)
    y = jnp.concatenate(
        [jnp.dot(hl[:, g * GROUP_W:(g + 1) * GROUP_W], ones_blk, preferred_element_type=F32)
         for g in range(N_GROUPS)], axis=1)
    return y[:m] + y[m:]


def _expand(x, lane_half):
    xb = x.astype(BF16)
    c = xb.shape[0]
    zero = jnp.zeros((c, 128), BF16)
    blocks = []
    for h in range(GROUP_HEADS):
        col = h // 2
        piece = xb[:, col * 128:(col + 1) * 128] * lane_half[h % 2]
        blocks.append(jnp.concatenate([piece, zero] if col == 0 else [zero, piece], axis=1))
    return jnp.concatenate(blocks, axis=0)


def _pack_state(s, g):
    return jnp.concatenate([s[g * GROUP_HEADS + h] for h in range(GROUP_HEADS)], axis=1)


def _heads_to_lanes(x, c):
    heads = x.shape[0] // HEAD_DIM
    return jnp.concatenate([x[h * HEAD_DIM:(h + 1) * HEAD_DIM, :c] for h in range(heads)], axis=1)


def _rwkv_body(z_ref, zpf_ref, s0_ref, mu_ref, vec_ref, w2_ref, a2_ref, g2_ref,
               blkp_ref, strict_ref, incl_ref, ones_ref,
               o_ref, sout_ref, state, zbuf, ar_s, bk_s, vt_s, gl_s, bonus_s, gate_s,
               *, chunk, n_valid, solve_steps, nb, pipelined):
    C = chunk
    R = GROUP_HEADS * C
    j = pl.program_id(1)
    last = pl.num_programs(1) - 1
    wslot = j % 2 if pipelined else 0
    rslot = 1 - wslot if pipelined else 0

    @pl.when(j == 0)
    def _():
        for b in range(nb):
            s0 = s0_ref[b]
            for g in range(N_GROUPS):
                state[b, g] = _pack_state(s0, g)
            zbuf[b, 7:8, :] = zpf_ref[b]
        if pipelined:
            ar_s[1] = jnp.zeros(ar_s.shape[1:], BF16)
            bk_s[1] = jnp.zeros(bk_s.shape[1:], BF16)
            vt_s[1] = jnp.zeros(vt_s.shape[1:], BF16)
            gl_s[1] = jnp.ones(gl_s.shape[1:], F32)
            bonus_s[1] = jnp.zeros(bonus_s.shape[1:], F32)
            gate_s[1] = jnp.zeros(gate_s.shape[1:], F32)

    @pl.when(j > 0)
    def _():
        for b in range(nb):
            zbuf[b, 7:8, :] = zbuf[b, C + 7:C + 8, :]

    w0, a0, k_k, k_a = vec_ref[0:1, :], vec_ref[1:2, :], vec_ref[2:3, :], vec_ref[3:4, :]
    r_k, gn_w, gn_b = vec_ref[4:5, :], vec_ref[5:6, :], vec_ref[6:7, :]
    ones_blk = ones_ref[...]
    ri = lax.broadcasted_iota(jnp.int32, (C, C), 0)
    ci = lax.broadcasted_iota(jnp.int32, (C, C), 1)
    tri = (ri >= ci).astype(BF16)
    lane = lax.broadcasted_iota(jnp.int32, (1, 128), 1)
    blk = ((lane < HEAD_DIM).astype(BF16), (lane >= HEAD_DIM).astype(BF16))

    def prep_stages(b):
        p = {}

        def s1():
            zc = z_ref[b]
            if zc.shape[0] < C:
                zc = jnp.concatenate([zc, jnp.zeros((C - zc.shape[0], RW_IN_W), F32)], axis=0)
            zbuf[b, 8:8 + C, :] = zc
            zp = zbuf[b, 7:7 + C, :]
            zm = zc + (zp - zc) * mu_ref[...]
            p["r"] = zm[:, 0:RW_W]
            p["k"] = zm[:, RW_W:2 * RW_W]
            p["v"] = zm[:, 2 * RW_W:3 * RW_W]
            o1 = 3 * RW_W
            p["zw"] = zm[:, o1:o1 + DECAY_LORA]
            p["za"] = zm[:, o1 + DECAY_LORA:o1 + DECAY_LORA + ICLR_LORA]
            p["zg"] = zm[:, o1 + DECAY_LORA + ICLR_LORA:]

        def s2():
            y = -(w0 + _mm(jnp.tanh(p["zw"]), w2_ref[...]))
            softplus = jnp.maximum(y, 0.0) + jnp.log(1.0 + jnp.exp(-jnp.abs(y)))
            p["logdecay"] = -jnp.exp(-softplus - 0.5)
            p["a"] = jax.nn.sigmoid(a0 + _mm(p["za"], a2_ref[...]))
            gate_s[wslot, b] = _mm(jax.nn.sigmoid(p["zg"]), g2_ref[...])

        def s3():
            k, v, a = p["k"], p["v"], p["a"]
            kk_raw = k * k_k
            kmod = k * (1.0 + (a - 1.0) * k_a)
            vv = v
            if n_valid < C:
                rows = lax.broadcasted_iota(jnp.int32, (C, 1), 0) < n_valid
                p["logdecay"] = jnp.where(rows, p["logdecay"], 0.0)
                kk_raw = jnp.where(rows, kk_raw, 0.0)
                kmod = jnp.where(rows, kmod, 0.0)
                vv = jnp.where(rows, v, 0.0)
            vpad = jnp.concatenate([vv, jnp.zeros((128 - C, RW_W), F32)], axis=0) if C < 128 else vv
            vt_s[wslot, b] = _heads_to_lanes(vpad.T, C).astype(BF16)
            sums = _head_sums(jnp.concatenate([kk_raw * kk_raw, p["r"] * kmod * r_k], axis=0), ones_blk)
            p["kk"] = kk_raw / jnp.maximum(jnp.sqrt(sums[:C]), 1e-12)
            bonus_s[wslot, b] = sums[C:] * v
            p["kmod"] = kmod

        def s4():
            logdecay = p["logdecay"]
            hi = logdecay.astype(BF16)
            lo = (logdecay - hi.astype(F32)).astype(BF16)
            cum = jnp.dot(tri, jnp.concatenate([hi, lo], axis=1), preferred_element_type=F32)
            cum = cum[:, :RW_W] + cum[:, RW_W:]
            p["cum"] = cum
            gam = jnp.exp(cum)
            p["gam"] = gam
            gl_s[wslot, b] = jnp.broadcast_to(gam[C - 1:C, :], (8, RW_W))

        def s5():
            cum, gam, kk = p["cum"], p["gam"], p["kk"]
            gam_inv = jnp.exp(-cum)
            At = -kk * jnp.exp(cum - p["logdecay"])
            Rt = p["r"] * gam
            Bt = kk * p["a"] * gam_inv
            Kt = p["kmod"] * gam_inv
            ar_s[wslot, b] = jnp.concatenate([At, Rt], axis=0).astype(BF16)
            bk_s[wslot, b] = jnp.concatenate([Bt, Kt], axis=0).astype(BF16)

        return [s1, s2, s3, s4, s5]

    strict = strict_ref[...]
    incl = incl_ref[...]
    blk_p = blkp_ref[...]
    chains = [dict(b=b, grp=grp, sl=slice(grp * GROUP_W, (grp + 1) * GROUP_W),
                   slt=slice(grp * R, (grp + 1) * R))
              for b in range(nb) for grp in range(N_GROUPS)]
    on_parts = {b: [None] * N_GROUPS for b in range(nb)}

    def expand_t(x):
        if C == HEAD_DIM:
            return _expand(x, blk)
        return jnp.concatenate([x.astype(BF16)] * GROUP_HEADS, axis=0) * blk_p

    def c_load():
        for ch in chains:
            b, sl = ch["b"], ch["sl"]
            ar = ar_s[rslot, b, :, sl]
            ch["BK"] = bk_s[rslot, b, :, sl]
            ch["VT"] = vt_s[rslot, b, :, ch["slt"]]
            ch["ARe"] = jnp.concatenate([_expand(ar[:C], blk), _expand(ar[C:], blk)], axis=0)
            ch["BKe"] = jnp.concatenate([_expand(ch["BK"][:C], blk), _expand(ch["BK"][C:], blk)], axis=0)

    def c_gram():
        for ch in chains:
            ch["S"] = state[ch["b"], ch["grp"]]
            lhs = jnp.concatenate([ch["BK"], ch["S"].astype(BF16)], axis=0)
            GT = _mm_nt(lhs, ch["ARe"])
            ch["Q"] = (GT[:C, :R] * strict).astype(BF16)
            ch["LakT"] = (GT[C:2 * C, :R] * strict).astype(BF16)
            ch["MT"] = jnp.concatenate([expand_t(GT[:C, R:] * incl), expand_t(GT[C:2 * C, R:] * incl)], axis=0)
            ch["P0T"] = GT[2 * C:, :]

    def c_rhs():
        for ch in chains:
            ch["XT"] = ch["P0T"][:, :R] + jnp.dot(ch["VT"], expand_t(ch["LakT"]), preferred_element_type=F32)

    def c_solve(i):
        def step():
            for ch in chains:
                qe = expand_t(ch["Q"])
                xb = ch["XT"].astype(BF16)
                if i + 1 < solve_steps:
                    res = jnp.dot(jnp.concatenate([xb, ch["Q"]], axis=0), qe, preferred_element_type=F32)
                    ch["XT"] = ch["XT"] + res[:HEAD_DIM]
                    ch["Q"] = res[HEAD_DIM:].astype(BF16)
                else:
                    ch["XT"] = ch["XT"] + jnp.dot(xb, qe, preferred_element_type=F32)
        return step

    def c_out():
        for ch in chains:
            b, grp, sl = ch["b"], ch["grp"], ch["sl"]
            uv = jnp.concatenate([ch["XT"].astype(BF16), ch["VT"]], axis=1)
            rhs = jnp.concatenate([ch["MT"], ch["BKe"]], axis=1)
            res = jnp.dot(uv, rhs, preferred_element_type=F32)
            state[b, grp] = (ch["S"] + res[:, R:]) * gl_s[rslot, b, 0:1, sl]
            OT = ch["P0T"][:, R:] + res[:, :R]
            mean = jnp.mean(OT, 0, keepdims=True)
            d = OT - mean
            var = jnp.mean(d * d, 0, keepdims=True)
            onT = d * lax.rsqrt(var + GN_EPS)
            rows = jnp.concatenate([onT[:, h * C:(h + 1) * C] for h in range(GROUP_HEADS)], axis=0)
            if C < 128:
                rows = jnp.concatenate([rows, jnp.zeros((GROUP_W, 128 - C), F32)], axis=1)
            on_parts[b][grp] = rows.T[:C]

    def c_tail():
        for b in range(nb):
            on = jnp.concatenate(on_parts[b], axis=1)
            out = (on * gn_w + gn_b + bonus_s[rslot, b]) * gate_s[rslot, b]
            o_ref[b] = out[:o_ref.shape[1]]

    chain = [c_load, c_gram, c_rhs] + [c_solve(i) for i in range(solve_steps)] + [c_out, c_tail]
    prep = [s for b in range(nb) for s in prep_stages(b)]
    if pipelined:
        done = 0
        for n, stage in enumerate(chain):
            stage()
            upto = (len(prep) * (n + 1)) // len(chain)
            for s in prep[done:upto]:
                s()
            done = upto
    else:
        for s in prep + chain:
            s()

    @pl.when(j == last)
    def _():
        for b in range(nb):
            for g in range(N_GROUPS):
                s = state[b, g]
                for h in range(GROUP_HEADS):
                    sout_ref[b, g * GROUP_HEADS + h] = s[:, h * HEAD_DIM:(h + 1) * HEAD_DIM]


def _rwkv_masks(chunk):
    R = GROUP_HEADS * chunk
    row_h = np.arange(R) // chunk
    lane_h = np.arange(GROUP_W) // HEAD_DIM
    blk_p = row_h[:, None] == row_h[None, :]
    ones_blk = lane_h[:, None] == lane_h[None, :]
    s, t = np.arange(chunk)[:, None], (np.arange(R) % chunk)[None, :]
    return (jnp.asarray(blk_p, BF16), jnp.asarray(t > s, F32), jnp.asarray(t >= s, F32),
            jnp.asarray(ones_blk, BF16))


def _rwkv(zr, zp_first, s0, mu, vecs, w2, a2, g2, *, chunk, n_valid, nb):
    b, t, _ = zr.shape
    rows = min(t, chunk)
    nc = t // rows
    pipelined = nc > 1
    lag = 1 if pipelined else 0
    solve_steps = max(1, (n_valid - 1).bit_length())
    masks = _rwkv_masks(chunk)
    full = lambda a: pl.BlockSpec(a.shape, lambda i, j: (0,) * a.ndim)
    return pl.pallas_call(
        functools.partial(_rwkv_body, chunk=chunk, n_valid=n_valid, solve_steps=solve_steps, nb=nb,
                          pipelined=pipelined),
        grid=(b // nb, nc + lag),
        in_specs=[pl.BlockSpec((nb, rows, RW_IN_W), lambda i, j: (i, jnp.minimum(j, nc - 1), 0)),
                  pl.BlockSpec((nb, 1, RW_IN_W), lambda i, j: (i, 0, 0)),
                  pl.BlockSpec((nb, RW_HEADS, HEAD_DIM, HEAD_DIM), lambda i, j: (i, 0, 0, 0)),
                  full(mu), full(vecs), full(w2), full(a2), full(g2)] + [full(m) for m in masks],
        out_specs=[pl.BlockSpec((nb, rows, RW_W), lambda i, j: (i, jnp.maximum(j - lag, 0), 0)),
                   pl.BlockSpec((nb, RW_HEADS, HEAD_DIM, HEAD_DIM), lambda i, j: (i, 0, 0, 0))],
        out_shape=[jax.ShapeDtypeStruct((b, t, RW_W), F32),
                   jax.ShapeDtypeStruct((b, RW_HEADS, HEAD_DIM, HEAD_DIM), F32)],
        scratch_shapes=[pltpu.VMEM((nb, N_GROUPS, HEAD_DIM, GROUP_W), F32),
                        pltpu.VMEM((nb, chunk + 8, RW_IN_W), F32),
                        pltpu.VMEM((2, nb, 2 * chunk, RW_W), BF16),
                        pltpu.VMEM((2, nb, 2 * chunk, RW_W), BF16),
                        pltpu.VMEM((2, nb, HEAD_DIM, RW_HEADS * chunk), BF16),
                        pltpu.VMEM((2, nb, 8, RW_W), F32),
                        pltpu.VMEM((2, nb, chunk, RW_W), F32),
                        pltpu.VMEM((2, nb, chunk, RW_W), F32)],
        compiler_params=_params(2),
        name="rwkv_chunk%d" % chunk,
    )(zr, zp_first, s0, mu, vecs, w2, a2, g2, *masks)


def _outffn_body(x_ref, att_ref, rw_ref, wo_ref, n2_ref, wg_ref, wu_ref, wd_ref, nf_ref, xo_ref, *, final):
    mix = jnp.concatenate([att_ref[...], rw_ref[...]], axis=-1).astype(BF16)
    x1 = x_ref[...] + jnp.dot(mix, wo_ref[...], preferred_element_type=F32)
    hb = _rmsnorm(x1, n2_ref[...]).astype(BF16)
    gate = jnp.dot(hb, wg_ref[...], preferred_element_type=F32)
    up = jnp.dot(hb, wu_ref[...], preferred_element_type=F32)
    act = (gate * jax.nn.sigmoid(gate) * up).astype(BF16)
    x2 = x1 + jnp.dot(act, wd_ref[...], preferred_element_type=F32)
    xo_ref[...] = _rmsnorm(x2, nf_ref[...]) if final else x2


def _outffn(x, att, rw, wo, n2, wg, wu, wd, nf, *, final, tm):
    n = x.shape[0]
    resident = lambda a: pl.BlockSpec(a.shape, lambda i: (0,) * a.ndim, pipeline_mode=pl.Buffered(1))
    return pl.pallas_call(
        functools.partial(_outffn_body, final=final),
        grid=(n // tm,),
        in_specs=[pl.BlockSpec((tm, D_MODEL), lambda i: (i, 0)),
                  pl.BlockSpec((tm, ATT_W), lambda i: (i, 0)),
                  pl.BlockSpec((tm, RW_W), lambda i: (i, 0)),
                  resident(wo), resident(n2), resident(wg), resident(wu), resident(wd), resident(nf)],
        out_specs=pl.BlockSpec((tm, D_MODEL), lambda i: (i, 0)),
        out_shape=jax.ShapeDtypeStruct((n, D_MODEL), F32),
        compiler_params=_params(1),
        name="outffn_final" if final else "outffn",
    )(x, att, rw, wo, n2, wg, wu, wd, nf)


def _t5_bucket_np(d):
    max_exact = NUM_BUCKETS // 2
    n = np.maximum(d, 0)
    large = max_exact + (np.log(np.maximum(n, 1) / max_exact) / np.log(MAX_DISTANCE / max_exact)
                         * (NUM_BUCKETS - max_exact)).astype(np.int32)
    large = np.minimum(large, NUM_BUCKETS - 1)
    return np.where(n < max_exact, n, large).astype(np.int32)


def _bias_body(tab_ref, bucket_ref, inside_ref, o_ref, *, keys_major):
    bucket = bucket_ref[...]
    inside = inside_ref[...] > 0
    for h in range(N_Q_HEADS):
        acc = jnp.zeros(bucket.shape, F32)
        for bk in range(NUM_BUCKETS):
            acc = jnp.where(bucket == bk, tab_ref[bk, h], acc)
        acc = jnp.where(inside, acc, NEG)
        if keys_major:
            kv, g, q = h // GQA, h % GQA, bucket.shape[1]
            o_ref[kv, :, g * q:(g + 1) * q] = acc
        else:
            o_ref[h] = acc


def _masked_bias(rel_table, d, inside, rows_per_head, keys_major=False):
    q, s = d.shape
    bucket = np.zeros((rows_per_head, s), np.int32)
    bucket[:q] = _t5_bucket_np(d)
    ins = np.zeros((rows_per_head, s), np.int32)
    ins[:q] = inside
    if keys_major:
        bucket, ins = bucket.T.copy(), ins.T.copy()
        out_shape = (N_KV_HEADS, s, GQA * rows_per_head)
    else:
        out_shape = (N_Q_HEADS, rows_per_head, s)
    vm = lambda shape: pl.BlockSpec(shape, lambda: (0,) * len(shape))
    b = pl.pallas_call(
        functools.partial(_bias_body, keys_major=keys_major),
        in_specs=[pl.BlockSpec(memory_space=pltpu.SMEM), vm(bucket.shape), vm(ins.shape)],
        out_specs=vm(out_shape),
        out_shape=jax.ShapeDtypeStruct(out_shape, F32),
        name="rel_bias",
    )(rel_table.astype(F32), jnp.asarray(bucket), jnp.asarray(ins))
    return b if keys_major else b.reshape(N_KV_HEADS, GQA * rows_per_head, s)


def _sink_rows(sink_l, rows_per_head):
    return jnp.repeat(sink_l.astype(F32).reshape(N_KV_HEADS, GQA, 1), rows_per_head, axis=1).reshape(
        N_KV_HEADS, GQA * rows_per_head, 1)


def kernel(x_prompt, x_sample, cache_k, cache_v, state_shift, state_wkv, norm1, w_in, w_out, sink,
           rel_table, mu, w0, w2, a0, a2, g2, k_k, k_a, r_k, gn_w, gn_b, norm2, w_gate, w_up,
           w_down, norm_f):
    depth = w_in.shape[0]
    B, T, _ = x_prompt.shape
    DB, S, _ = x_sample.shape
    wc = cache_k.shape[2]

    qi = np.arange(BLOCK)[:, None]
    sj = np.arange(2 * BLOCK)[None, :]
    dp = qi + BLOCK - sj
    inside_p = (dp >= 0) & (dp <= WINDOW)
    bias_p = jnp.stack([_masked_bias(rel_table, dp, inside_p & (sj >= BLOCK), BLOCK, keys_major=True),
                        _masked_bias(rel_table, dp, inside_p, BLOCK, keys_major=True)])
    ds_full = (wc + np.arange(S))[:, None] - np.arange(wc + S)[None, :]
    inside_s = (ds_full >= 0) & (ds_full <= WINDOW)
    bias_c = _masked_bias(rel_table, ds_full[:, :wc], inside_s[:, :wc], SAMPLE_PAD)
    dn = np.full((S, SAMPLE_PAD), -1, np.int64)
    dn[:, :S] = ds_full[:, wc:]
    inn = np.zeros((S, SAMPLE_PAD), bool)
    inn[:, :S] = inside_s[:, wc:]
    bias_n = _masked_bias(rel_table, dn, inn, SAMPLE_PAD)

    xp = x_prompt.reshape(B * T, D_MODEL)
    xs = x_sample.reshape(DB * S, D_MODEL)
    zeros_shift = jnp.zeros((B, 1, RW_IN_W), F32)
    zeros_state = jnp.zeros((B, RW_HEADS, HEAD_DIM, HEAD_DIM), F32)
    nf = norm_f.reshape(1, D_MODEL)
    ones = jnp.ones((1, D_MODEL), F32)

    kp, vp, sp, wp, ksl, vsl, ssl, wsl = [], [], [], [], [], [], [], []
    for l in range(depth):
        w_in_b = w_in[l].astype(BF16)
        wo_b, wg_b, wu_b, wd_b = (w_out[l].astype(BF16), w_gate[l].astype(BF16),
                                  w_up[l].astype(BF16), w_down[l].astype(BF16))
        n1 = norm1[l].reshape(1, D_MODEL)
        n2 = norm2[l].reshape(1, D_MODEL)
        mu_l = mu[l].reshape(1, RW_IN_W)
        vecs = jnp.stack([w0[l], a0[l], k_k[l], k_a[l], r_k[l], gn_w[l], gn_b[l], jnp.zeros_like(w0[l])])
        w2_b, a2_b, g2_b = w2[l].astype(BF16), a2[l].astype(BF16), g2[l].astype(BF16)
        final = l == depth - 1

        za, zr, h_last, kv_last = _inproj(xp, n1, w_in_b, norm=True, tm=1024, seq_len=T)
        za3 = za.reshape(B, T, ATT_IN_W)
        kp.append(kv_last[:, :, :KV_W].reshape(B, WINDOW, N_KV_HEADS, HEAD_DIM))
        vp.append(kv_last[:, :, KV_W:].reshape(B, WINDOW, N_KV_HEADS, HEAD_DIM))
        sp.append(h_last.reshape(B, D_MODEL))
        sink_lanes = jnp.swapaxes(_sink_rows(sink[l], BLOCK), 1, 2)
        att = _attn_prompt(za3, bias_p, sink_lanes, nblk=4)
        rw, wkv = _rwkv(zr.reshape(B, T, RW_IN_W), zeros_shift, zeros_state, mu_l, vecs, w2_b, a2_b, g2_b,
                        chunk=PROMPT_CHUNK, n_valid=PROMPT_CHUNK, nb=8)
        wp.append(wkv)
        xp = _outffn(xp, att.reshape(B * T, ATT_W), rw.reshape(B * T, RW_W), wo_b, n2, wg_b, wu_b, wd_b, nf,
                     final=final, tm=512)

        za_s, zr_s, h_s = _inproj(xs, n1, w_in_b, norm=True, tm=DB * S)
        _, zr_prev, _ = _inproj(state_shift[l], ones, w_in_b, norm=False, tm=DB)
        ssl.append(h_s.reshape(DB, S, D_MODEL)[:, -1])
        att_s, kb, vb = _attn_sample(za_s.reshape(DB, S, ATT_IN_W), cache_k[l].reshape(DB, wc, KV_W),
                                     cache_v[l].reshape(DB, wc, KV_W),
                                     bias_c, bias_n, _sink_rows(sink[l], SAMPLE_PAD), ns=8)
        ksl.append(kb.reshape(DB, wc, N_KV_HEADS, HEAD_DIM))
        vsl.append(vb.reshape(DB, wc, N_KV_HEADS, HEAD_DIM))
        rw_s, wkv_s = _rwkv(zr_s.reshape(DB, S, RW_IN_W), zr_prev.reshape(DB, 1, RW_IN_W), state_wkv[l],
                            mu_l, vecs, w2_b, a2_b, g2_b, chunk=SAMPLE_CHUNK, n_valid=S, nb=4)
        wsl.append(wkv_s)
        xs = _outffn(xs, att_s.reshape(DB * S, ATT_W), rw_s.reshape(DB * S, RW_W),
                     wo_b, n2, wg_b, wu_b, wd_b, nf, final=final, tm=DB * S)

    y_prompt = xp.reshape(B, T, D_MODEL)
    y_sample = xs.reshape(DB, S, D_MODEL)
    return (y_prompt, y_sample, jnp.stack(kp), jnp.stack(vp), jnp.stack(sp), jnp.stack(wp),
            jnp.stack(ksl), jnp.stack(vsl), jnp.stack(ssl), jnp.stack(wsl))
```

```python
import functools

import numpy as np
import jax
import jax.numpy as jnp
from jax import lax
from jax.experimental import pallas as pl
from jax.experimental.pallas import tpu as pltpu

D_MODEL = 1024
HEAD_DIM = 64
N_Q_HEADS = 8
N_KV_HEADS = 2
GQA = N_Q_HEADS // N_KV_HEADS
ATT_W = N_Q_HEADS * HEAD_DIM
KV_W = N_KV_HEADS * HEAD_DIM
WINDOW = 128
BLOCK = 128
NUM_BUCKETS = 32
MAX_DISTANCE = 128
RW_HEADS = 8
RW_W = RW_HEADS * HEAD_DIM
DECAY_LORA = 64
ICLR_LORA = 64
GATE_LORA = 128
RW_IN_W = 3 * RW_W + DECAY_LORA + ICLR_LORA + GATE_LORA
ATT_IN_W = ATT_W + 2 * KV_W
NORM_EPS = 1e-6
GN_EPS = 64e-5
NEG = -1e30
SCALE = HEAD_DIM ** -0.5

V7X_VMEM_LIMIT_BYTES = 56 * 1024 * 1024

PROMPT_CHUNK = 64
SAMPLE_CHUNK = 16
SAMPLE_PAD = 8

BF16 = jnp.bfloat16
F32 = jnp.float32


def _mm(a, b):
    return jnp.dot(a.astype(BF16), b.astype(BF16), preferred_element_type=F32)


def _mm_nt(a, b):
    return lax.dot_general(a.astype(BF16), b.astype(BF16), (((1,), (1,)), ((), ())),
                           preferred_element_type=F32)


def _rmsnorm(x, g):
    return x * lax.rsqrt(jnp.mean(x * x, -1, keepdims=True) + NORM_EPS) * g


def _params(n_axes):
    return pltpu.CompilerParams(dimension_semantics=("arbitrary",) * n_axes,
                                vmem_limit_bytes=V7X_VMEM_LIMIT_BYTES)


def _inproj_body(x_ref, g_ref, w_ref, za_ref, zr_ref, h_ref, *tail_refs, norm, seq_len):
    x = x_ref[...]
    h = _rmsnorm(x, g_ref[...]) if norm else x
    z = jnp.dot(h.astype(BF16), w_ref[...], preferred_element_type=F32)
    za_ref[...] = z[:, :ATT_IN_W]
    zr_ref[...] = z[:, ATT_IN_W:]
    if seq_len is None:
        h_ref[...] = h
    else:
        (kv_ref,) = tail_refs
        tm = x.shape[0]
        h_ref[0] = h[tm - 1:, :]
        kv_ref[0] = z[tm - WINDOW:, ATT_W:ATT_IN_W]


def _inproj(x, g, w, *, norm, tm, seq_len=None):
    n = x.shape[0]
    in_w = w.shape[1]
    if seq_len is None:
        tail_specs = [pl.BlockSpec((tm, D_MODEL), lambda i: (i, 0))]
        tail_shapes = [jax.ShapeDtypeStruct((n, D_MODEL), F32)]
    else:
        tiles = seq_len // tm
        tail_specs = [pl.BlockSpec((1, 1, D_MODEL), lambda i: (i // tiles, 0, 0)),
                      pl.BlockSpec((1, WINDOW, 2 * KV_W), lambda i: (i // tiles, 0, 0))]
        tail_shapes = [jax.ShapeDtypeStruct((n // seq_len, 1, D_MODEL), F32),
                       jax.ShapeDtypeStruct((n // seq_len, WINDOW, 2 * KV_W), F32)]
    return pl.pallas_call(
        functools.partial(_inproj_body, norm=norm, seq_len=seq_len),
        grid=(n // tm,),
        in_specs=[pl.BlockSpec((tm, D_MODEL), lambda i: (i, 0)),
                  pl.BlockSpec((1, D_MODEL), lambda i: (0, 0)),
                  pl.BlockSpec((D_MODEL, in_w), lambda i: (0, 0))],
        out_specs=[pl.BlockSpec((tm, ATT_IN_W), lambda i: (i, 0)),
                   pl.BlockSpec((tm, RW_IN_W), lambda i: (i, 0))] + tail_specs,
        out_shape=[jax.ShapeDtypeStruct((n, ATT_IN_W), F32),
                   jax.ShapeDtypeStruct((n, RW_IN_W), F32)] + tail_shapes,
        compiler_params=_params(1),
        name="inproj_norm" if norm else "inproj_plain",
    )(x, g, w)


def _sink_softmax(score_parts, sink_col):
    m = sink_col
    for s in score_parts:
        m = jnp.maximum(m, jnp.max(s, -1, keepdims=True))
    es = [jnp.exp(s - m) for s in score_parts]
    den = jnp.exp(sink_col - m)
    for e in es:
        den = den + jnp.sum(e, -1, keepdims=True)
    return [e / den for e in es]


QG_W = GQA * HEAD_DIM


def _tile_kv_heads(x):
    rot = pltpu.roll(x, HEAD_DIM, 1)
    first = lax.broadcasted_iota(jnp.int32, x.shape, 1) < HEAD_DIM
    halves = (jnp.where(first, x, rot).astype(BF16), jnp.where(first, rot, x).astype(BF16))
    return [jnp.concatenate([h] * (QG_W // KV_W), axis=1) for h in halves]


def _attn_prompt_body(q_ref, kv_ref, bias_ref, sink_ref, o_ref, kt_prev, vT_prev, *, nblk):
    j = pl.program_id(1)

    @pl.when(j == 0)
    def _():
        kt_prev[...] = jnp.zeros_like(kt_prev)
        vT_prev[...] = jnp.zeros_like(vT_prev)

    lane = lax.broadcasted_iota(jnp.int32, (1, 128), 1)
    lane_half = ((lane < HEAD_DIM).astype(BF16), (lane >= HEAD_DIM).astype(BF16))
    kts = [[kt_prev[kv] for kv in range(N_KV_HEADS)]]
    vTs = [vT_prev[...]]
    for i in range(nblk):
        rows = slice(i * BLOCK, (i + 1) * BLOCK)
        kts.append(_tile_kv_heads(kv_ref[0, rows, 0:KV_W]))
        vTs.append(kv_ref[0, rows, KV_W:2 * KV_W].T.astype(BF16))
    for kv in range(N_KV_HEADS):
        kt_prev[kv] = kts[nblk][kv]
    vT_prev[...] = vTs[nblk]

    units = []
    for i in range(nblk):
        qb = (q_ref[0, i * BLOCK:(i + 1) * BLOCK, :] * SCALE).astype(BF16)
        first = jnp.where(j == 0, 0, 1) if i == 0 else 1
        for kv in range(N_KV_HEADS):
            qe = _expand(qb[:, kv * QG_W:(kv + 1) * QG_W], lane_half)
            kcat = jnp.concatenate([kts[i][kv], kts[i + 1][kv]], axis=0)
            units.append(dict(i=i, kv=kv, s=_mm_nt(kcat, qe), first=first))
    for u in units:
        b = bias_ref[u["first"], u["kv"]]
        s = jnp.where(b > 0.5 * NEG, u["s"] + b, NEG)
        sink_row = sink_ref[u["kv"]]
        m = jnp.maximum(jnp.max(s, 0, keepdims=True), sink_row)
        e = jnp.exp(s - m)
        den = jnp.sum(e, 0, keepdims=True) + jnp.exp(sink_row - m)
        u["p"] = (e * (1.0 / den)).astype(BF16)
    pieces = [[None] * N_Q_HEADS for _ in range(nblk)]
    for u in units:
        i, kv = u["i"], u["kv"]
        hd = slice(kv * HEAD_DIM, (kv + 1) * HEAD_DIM)
        vcatT = jnp.concatenate([vTs[i][hd], vTs[i + 1][hd]], axis=1)
        oT = jnp.dot(vcatT, u["p"], preferred_element_type=F32)
        for g in range(GQA):
            pieces[i][kv * GQA + g] = oT[:, g * BLOCK:(g + 1) * BLOCK]
    for i in range(nblk):
        o_ref[0, i * BLOCK:(i + 1) * BLOCK, :] = jnp.concatenate(pieces[i], axis=0).T


def _attn_prompt(za, bias, sink_rows, *, nblk):
    b, t, _ = za.shape
    rows = nblk * BLOCK
    full = lambda a: pl.BlockSpec(a.shape, lambda i, j: (0,) * a.ndim)
    return pl.pallas_call(
        functools.partial(_attn_prompt_body, nblk=nblk),
        grid=(b, t // rows),
        in_specs=[pl.BlockSpec((1, rows, ATT_W), lambda i, j: (i, j, 0)),
                  pl.BlockSpec((1, rows, 2 * KV_W), lambda i, j: (i, j, ATT_W // (2 * KV_W))),
                  full(bias), full(sink_rows)],
        out_specs=pl.BlockSpec((1, rows, ATT_W), lambda i, j: (i, j, 0)),
        out_shape=jax.ShapeDtypeStruct((b, t, ATT_W), F32),
        scratch_shapes=[pltpu.VMEM((N_KV_HEADS, BLOCK, QG_W), BF16),
                        pltpu.VMEM((KV_W, BLOCK), BF16)],
        compiler_params=_params(2),
        name="attn_prompt",
    )(za, za, bias, sink_rows)


def _attn_sample_body(z_ref, ck_ref, cv_ref, bias_c_ref, bias_n_ref, sink_ref,
                      o_ref, kb_ref, vb_ref, *, n_new, ns):
    units = []
    for s in range(ns):
        z = jnp.concatenate([z_ref[s], jnp.zeros((SAMPLE_PAD - n_new, ATT_IN_W), F32)], axis=0)
        q = z[:, :ATT_W]
        kn = z[:, ATT_W:ATT_W + KV_W]
        vn = z[:, ATT_W + KV_W:]
        ck, cv = ck_ref[s], cv_ref[s]
        keep = WINDOW - n_new
        kb_ref[s, 0:keep, :] = ck_ref[s, n_new:WINDOW, :]
        kb_ref[s, keep:WINDOW, :] = kn[0:n_new]
        vb_ref[s, 0:keep, :] = cv_ref[s, n_new:WINDOW, :]
        vb_ref[s, keep:WINDOW, :] = vn[0:n_new]
        for kv in range(N_KV_HEADS):
            sl = slice(kv * HEAD_DIM, (kv + 1) * HEAD_DIM)
            qg = jnp.concatenate(
                [q[:, (kv * GQA + g) * HEAD_DIM:(kv * GQA + g + 1) * HEAD_DIM] for g in range(GQA)], axis=0)
            units.append(dict(s=s, kv=kv, qg=qg, ck=ck[:, sl], kn=kn[:, sl], cv=cv[:, sl], vn=vn[:, sl]))
    for u in units:
        u["sc"] = _mm_nt(u["qg"], u["ck"]) * SCALE
        u["sn"] = _mm_nt(u["qg"], u["kn"]) * SCALE
    for u in units:
        bc, bn = bias_c_ref[u["kv"]], bias_n_ref[u["kv"]]
        sc = jnp.where(bc > 0.5 * NEG, u["sc"] + bc, NEG)
        sn = jnp.where(bn > 0.5 * NEG, u["sn"] + bn, NEG)
        u["pc"], u["pn"] = _sink_softmax([sc, sn], sink_ref[u["kv"]])
    outs = [[] for _ in range(ns)]
    for u in units:
        o = _mm(u["pc"], u["cv"]) + _mm(u["pn"], u["vn"])
        outs[u["s"]] += [o[g * SAMPLE_PAD:(g + 1) * SAMPLE_PAD] for g in range(GQA)]
    for s in range(ns):
        o_ref[s] = jnp.concatenate(outs[s], axis=1)[:n_new]


def _attn_sample(za, ck, cv, bias_c, bias_n, sink_rows, *, layer, ns):
    db, n_new, _ = za.shape
    full = lambda a: pl.BlockSpec(a.shape, lambda i: (0,) * a.ndim)
    return pl.pallas_call(
        functools.partial(_attn_sample_body, n_new=n_new, ns=ns),
        grid=(db // ns,),
        in_specs=[pl.BlockSpec((ns, n_new, ATT_IN_W), lambda i: (i, 0, 0)),
                  pl.BlockSpec((None, ns, WINDOW, KV_W), lambda i: (layer, i, 0, 0)),
                  pl.BlockSpec((None, ns, WINDOW, KV_W), lambda i: (layer, i, 0, 0)),
                  full(bias_c), full(bias_n), full(sink_rows)],
        out_specs=[pl.BlockSpec((ns, n_new, ATT_W), lambda i: (i, 0, 0)),
                   pl.BlockSpec((ns, WINDOW, KV_W), lambda i: (i, 0, 0)),
                   pl.BlockSpec((ns, WINDOW, KV_W), lambda i: (i, 0, 0))],
        out_shape=[jax.ShapeDtypeStruct((db, n_new, ATT_W), F32),
                   jax.ShapeDtypeStruct((db, WINDOW, KV_W), F32),
                   jax.ShapeDtypeStruct((db, WINDOW, KV_W), F32)],
        compiler_params=_params(1),
        name="attn_sample",
    )(za, ck, cv, bias_c, bias_n, sink_rows)


GROUP_HEADS = 4
GROUP_W = GROUP_HEADS * HEAD_DIM
N_GROUPS = RW_HEADS // GROUP_HEADS


def _head_sums(x, ones_blk):
    m = x.shape[0]
    hi = x.astype(BF16)
    lo = (x - hi.astype(F32)).astype(BF16)
    hl = jnp.concatenate([hi, lo], axis=0)
    y = jnp.concatenate(
        [jnp.dot(hl[:, g * GROUP_W:(g + 1) * GROUP_W], ones_blk, preferred_element_type=F32)
         for g in range(N_GROUPS)], axis=1)
    return y[:m] + y[m:]


def _expand(x, lane_half):
    xb = x.astype(BF16)
    c = xb.shape[0]
    zero = jnp.zeros((c, 128), BF16)
    blocks = []
    for h in range(GROUP_HEADS):
        col = h // 2
        piece = xb[:, col * 128:(col + 1) * 128] * lane_half[h % 2]
        blocks.append(jnp.concatenate([piece, zero] if col == 0 else [zero, piece], axis=1))
    return jnp.concatenate(blocks, axis=0)


def _pack_state(s, g):
    return jnp.concatenate([s[g * GROUP_HEADS + h] for h in range(GROUP_HEADS)], axis=1)


def _heads_to_lanes(x, c):
    heads = x.shape[0] // HEAD_DIM
    return jnp.concatenate([x[h * HEAD_DIM:(h + 1) * HEAD_DIM, :c] for h in range(heads)], axis=1)


def _rwkv_body(z_ref, zpf_ref, s0_ref, mu_ref, vec_ref, w2_ref, a2_ref, g2_ref,
               blkp_ref, strict_ref, incl_ref, ones_ref,
               o_ref, sout_ref, state, zbuf, ar_s, bk_s, vt_s, gl_s, bonus_s, gate_s,
               *, chunk, n_valid, solve_steps, nb, pipelined):
    C = chunk
    R = GROUP_HEADS * C
    j = pl.program_id(1)
    last = pl.num_programs(1) - 1
    wslot = j % 2 if pipelined else 0
    rslot = 1 - wslot if pipelined else 0

    @pl.when(j == 0)
    def _():
        for b in range(nb):
            s0 = s0_ref[b]
            for g in range(N_GROUPS):
                state[b, g] = _pack_state(s0, g)
            zbuf[b, 7:8, :] = zpf_ref[b]
        if pipelined:
            ar_s[1] = jnp.zeros(ar_s.shape[1:], BF16)
            bk_s[1] = jnp.zeros(bk_s.shape[1:], BF16)
            vt_s[1] = jnp.zeros(vt_s.shape[1:], BF16)
            gl_s[1] = jnp.ones(gl_s.shape[1:], F32)
            bonus_s[1] = jnp.zeros(bonus_s.shape[1:], F32)
            gate_s[1] = jnp.zeros(gate_s.shape[1:], F32)

    @pl.when(j > 0)
    def _():
        for b in range(nb):
            zbuf[b, 7:8, :] = zbuf[b, C + 7:C + 8, :]

    w0, a0, k_k, k_a = vec_ref[0:1, :], vec_ref[1:2, :], vec_ref[2:3, :], vec_ref[3:4, :]
    r_k, gn_w, gn_b = vec_ref[4:5, :], vec_ref[5:6, :], vec_ref[6:7, :]
    ones_blk = ones_ref[...]
    ri = lax.broadcasted_iota(jnp.int32, (C, C), 0)
    ci = lax.broadcasted_iota(jnp.int32, (C, C), 1)
    tri = (ri >= ci).astype(BF16)
    lane = lax.broadcasted_iota(jnp.int32, (1, 128), 1)
    blk = ((lane < HEAD_DIM).astype(BF16), (lane >= HEAD_DIM).astype(BF16))

    def prep_stages(b):
        p = {}

        def s1():
            zc = z_ref[b]
            if zc.shape[0] < C:
                zc = jnp.concatenate([zc, jnp.zeros((C - zc.shape[0], RW_IN_W), F32)], axis=0)
            zbuf[b, 8:8 + C, :] = zc
            zp = zbuf[b, 7:7 + C, :]
            zm = zc + (zp - zc) * mu_ref[...]
            p["r"] = zm[:, 0:RW_W]
            p["k"] = zm[:, RW_W:2 * RW_W]
            p["v"] = zm[:, 2 * RW_W:3 * RW_W]
            o1 = 3 * RW_W
            p["zw"] = zm[:, o1:o1 + DECAY_LORA]
            p["za"] = zm[:, o1 + DECAY_LORA:o1 + DECAY_LORA + ICLR_LORA]
            p["zg"] = zm[:, o1 + DECAY_LORA + ICLR_LORA:]

        def s2():
            y = -(w0 + _mm(jnp.tanh(p["zw"]), w2_ref[...]))
            softplus = jnp.maximum(y, 0.0) + jnp.log(1.0 + jnp.exp(-jnp.abs(y)))
            p["logdecay"] = -jnp.exp(-softplus - 0.5)
            p["a"] = jax.nn.sigmoid(a0 + _mm(p["za"], a2_ref[...]))
            gate_s[wslot, b] = _mm(jax.nn.sigmoid(p["zg"]), g2_ref[...])

        def s3():
            k, v, a = p["k"], p["v"], p["a"]
            kk_raw = k * k_k
            kmod = k * (1.0 + (a - 1.0) * k_a)
            vv = v
            if n_valid < C:
                rows = lax.broadcasted_iota(jnp.int32, (C, 1), 0) < n_valid
                p["logdecay"] = jnp.where(rows, p["logdecay"], 0.0)
                kk_raw = jnp.where(rows, kk_raw, 0.0)
                kmod = jnp.where(rows, kmod, 0.0)
                vv = jnp.where(rows, v, 0.0)
            vpad = jnp.concatenate([vv, jnp.zeros((128 - C, RW_W), F32)], axis=0) if C < 128 else vv
            vt_s[wslot, b] = _heads_to_lanes(vpad.T, C).astype(BF16)
            sums = _head_sums(jnp.concatenate([kk_raw * kk_raw, p["r"] * kmod * r_k], axis=0), ones_blk)
            p["kk"] = kk_raw * lax.rsqrt(jnp.maximum(sums[:C], 1e-24))
            bonus_s[wslot, b] = sums[C:] * v
            p["kmod"] = kmod

        def s4():
            logdecay = p["logdecay"]
            hi = logdecay.astype(BF16)
            lo = (logdecay - hi.astype(F32)).astype(BF16)
            cum = jnp.dot(tri, jnp.concatenate([hi, lo], axis=1), preferred_element_type=F32)
            cum = cum[:, :RW_W] + cum[:, RW_W:]
            p["cum"] = cum
            gam = jnp.exp(cum)
            p["gam"] = gam
            gl_s[wslot, b] = jnp.broadcast_to(gam[C - 1:C, :], (8, RW_W))

        def s5():
            cum, gam, kk = p["cum"], p["gam"], p["kk"]
            gam_inv = jnp.exp(-cum)
            At = -kk * jnp.exp(cum - p["logdecay"])
            Rt = p["r"] * gam
            Bt = kk * p["a"] * gam_inv
            Kt = p["kmod"] * gam_inv
            ar_s[wslot, b] = jnp.concatenate([At, Rt], axis=0).astype(BF16)
            bk_s[wslot, b] = jnp.concatenate([Bt, Kt], axis=0).astype(BF16)

        return [s1, s2, s3, s4, s5]

    strict = strict_ref[...]
    incl = incl_ref[...]
    blk_p = blkp_ref[...]
    chains = [dict(b=b, grp=grp, sl=slice(grp * GROUP_W, (grp + 1) * GROUP_W),
                   slt=slice(grp * R, (grp + 1) * R))
              for b in range(nb) for grp in range(N_GROUPS)]
    on_parts = {b: [None] * N_GROUPS for b in range(nb)}

    def expand_t(x):
        if C == HEAD_DIM:
            return _expand(x, blk)
        return jnp.concatenate([x.astype(BF16)] * GROUP_HEADS, axis=0) * blk_p

    def c_load():
        for ch in chains:
            b, sl = ch["b"], ch["sl"]
            ar = ar_s[rslot, b, :, sl]
            ch["BK"] = bk_s[rslot, b, :, sl]
            ch["VT"] = vt_s[rslot, b, :, ch["slt"]]
            ch["ARe"] = jnp.concatenate([_expand(ar[:C], blk), _expand(ar[C:], blk)], axis=0)
            ch["BKe"] = jnp.concatenate([_expand(ch["BK"][:C], blk), _expand(ch["BK"][C:], blk)], axis=0)

    def c_gram():
        for ch in chains:
            ch["S"] = state[ch["b"], ch["grp"]]
            lhs = jnp.concatenate([ch["BK"], ch["S"].astype(BF16)], axis=0)
            GT = _mm_nt(lhs, ch["ARe"])
            ch["Q"] = (GT[:C, :R] * strict).astype(BF16)
            ch["LakT"] = (GT[C:2 * C, :R] * strict).astype(BF16)
            ch["MT"] = jnp.concatenate([expand_t(GT[:C, R:] * incl), expand_t(GT[C:2 * C, R:] * incl)], axis=0)
            ch["P0T"] = GT[2 * C:, :]

    def c_rhs():
        for ch in chains:
            ch["XT"] = ch["P0T"][:, :R] + jnp.dot(ch["VT"], expand_t(ch["LakT"]), preferred_element_type=F32)

    def c_solve(i):
        def step():
            for ch in chains:
                qe = expand_t(ch["Q"])
                xb = ch["XT"].astype(BF16)
                if i + 1 < solve_steps:
                    res = jnp.dot(jnp.concatenate([xb, ch["Q"]], axis=0), qe, preferred_element_type=F32)
                    ch["XT"] = ch["XT"] + res[:HEAD_DIM]
                    ch["Q"] = res[HEAD_DIM:].astype(BF16)
                else:
                    ch["XT"] = ch["XT"] + jnp.dot(xb, qe, preferred_element_type=F32)
        return step

    def c_out():
        for ch in chains:
            b, grp, sl = ch["b"], ch["grp"], ch["sl"]
            uv = jnp.concatenate([ch["XT"].astype(BF16), ch["VT"]], axis=1)
            rhs = jnp.concatenate([ch["MT"], ch["BKe"]], axis=1)
            res = jnp.dot(uv, rhs, preferred_element_type=F32)
            state[b, grp] = (ch["S"] + res[:, R:]) * gl_s[rslot, b, 0:1, sl]
            OT = ch["P0T"][:, R:] + res[:, :R]
            mean = jnp.mean(OT, 0, keepdims=True)
            d = OT - mean
            var = jnp.mean(d * d, 0, keepdims=True)
            onT = d * lax.rsqrt(var + GN_EPS)
            rows = jnp.concatenate([onT[:, h * C:(h + 1) * C] for h in range(GROUP_HEADS)], axis=0)
            if C < 128:
                rows = jnp.concatenate([rows, jnp.zeros((GROUP_W, 128 - C), F32)], axis=1)
            on_parts[b][grp] = rows.T[:C]

    def c_tail():
        for b in range(nb):
            on = jnp.concatenate(on_parts[b], axis=1)
            out = (on * gn_w + gn_b + bonus_s[rslot, b]) * gate_s[rslot, b]
            o_ref[b] = out[:o_ref.shape[1]]

    chain = [c_load, c_gram, c_rhs] + [c_solve(i) for i in range(solve_steps)] + [c_out, c_tail]
    prep = [s for b in range(nb) for s in prep_stages(b)]
    if pipelined:
        done = 0
        for n, stage in enumerate(chain):
            stage()
            upto = (len(prep) * (n + 1)) // len(chain)
            for s in prep[done:upto]:
                s()
            done = upto
    else:
        for s in prep + chain:
            s()

    @pl.when(j == last)
    def _():
        for b in range(nb):
            for g in range(N_GROUPS):
                s = state[b, g]
                for h in range(GROUP_HEADS):
                    sout_ref[b, g * GROUP_HEADS + h] = s[:, h * HEAD_DIM:(h + 1) * HEAD_DIM]


def _rwkv_masks(chunk):
    R = GROUP_HEADS * chunk
    row_h = np.arange(R) // chunk
    lane_h = np.arange(GROUP_W) // HEAD_DIM
    blk_p = row_h[:, None] == row_h[None, :]
    ones_blk = lane_h[:, None] == lane_h[None, :]
    s, t = np.arange(chunk)[:, None], (np.arange(R) % chunk)[None, :]
    return (jnp.asarray(blk_p, BF16), jnp.asarray(t > s, F32), jnp.asarray(t >= s, F32),
            jnp.asarray(ones_blk, BF16))


def _rwkv(zr, zp_first, s0, mu, vecs, w2, a2, g2, *, layer, chunk, n_valid, nb):
    b, t, _ = zr.shape
    rows = min(t, chunk)
    nc = t // rows
    pipelined = nc > 1
    lag = 1 if pipelined else 0
    solve_steps = max(1, (n_valid - 1).bit_length())
    masks = _rwkv_masks(chunk)
    full = lambda a: pl.BlockSpec(a.shape, lambda i, j: (0,) * a.ndim)
    return pl.pallas_call(
        functools.partial(_rwkv_body, chunk=chunk, n_valid=n_valid, solve_steps=solve_steps, nb=nb,
                          pipelined=pipelined),
        grid=(b // nb, nc + lag),
        in_specs=[pl.BlockSpec((nb, rows, RW_IN_W), lambda i, j: (i, jnp.minimum(j, nc - 1), 0)),
                  pl.BlockSpec((nb, 1, RW_IN_W), lambda i, j: (i, 0, 0)),
                  pl.BlockSpec((None, nb, RW_HEADS, HEAD_DIM, HEAD_DIM), lambda i, j: (layer, i, 0, 0, 0)),
                  full(mu), full(vecs), full(w2), full(a2), full(g2)] + [full(m) for m in masks],
        out_specs=[pl.BlockSpec((nb, rows, RW_W), lambda i, j: (i, jnp.maximum(j - lag, 0), 0)),
                   pl.BlockSpec((nb, RW_HEADS, HEAD_DIM, HEAD_DIM), lambda i, j: (i, 0, 0, 0))],
        out_shape=[jax.ShapeDtypeStruct((b, t, RW_W), F32),
                   jax.ShapeDtypeStruct((b, RW_HEADS, HEAD_DIM, HEAD_DIM), F32)],
        scratch_shapes=[pltpu.VMEM((nb, N_GROUPS, HEAD_DIM, GROUP_W), F32),
                        pltpu.VMEM((nb, chunk + 8, RW_IN_W), F32),
                        pltpu.VMEM((2, nb, 2 * chunk, RW_W), BF16),
                        pltpu.VMEM((2, nb, 2 * chunk, RW_W), BF16),
                        pltpu.VMEM((2, nb, HEAD_DIM, RW_HEADS * chunk), BF16),
                        pltpu.VMEM((2, nb, 8, RW_W), F32),
                        pltpu.VMEM((2, nb, chunk, RW_W), F32),
                        pltpu.VMEM((2, nb, chunk, RW_W), F32)],
        compiler_params=_params(2),
        name="rwkv_chunk%d" % chunk,
    )(zr, zp_first, s0, mu, vecs, w2, a2, g2, *masks)


def _outffn_body(x_ref, att_ref, rw_ref, wo_ref, n2_ref, wg_ref, wu_ref, wd_ref, nf_ref, xo_ref, *, final):
    mix = jnp.concatenate([att_ref[...], rw_ref[...]], axis=-1).astype(BF16)
    x1 = x_ref[...] + jnp.dot(mix, wo_ref[...], preferred_element_type=F32)
    hb = _rmsnorm(x1, n2_ref[...]).astype(BF16)
    gate = jnp.dot(hb, wg_ref[...], preferred_element_type=F32)
    up = jnp.dot(hb, wu_ref[...], preferred_element_type=F32)
    act = (gate * jax.nn.sigmoid(gate) * up).astype(BF16)
    x2 = x1 + jnp.dot(act, wd_ref[...], preferred_element_type=F32)
    xo_ref[...] = _rmsnorm(x2, nf_ref[...]) if final else x2


def _outffn(x, att, rw, wo, n2, wg, wu, wd, nf, *, final, tm):
    n = x.shape[0]
    resident = lambda a: pl.BlockSpec(a.shape, lambda i: (0,) * a.ndim, pipeline_mode=pl.Buffered(1))
    return pl.pallas_call(
        functools.partial(_outffn_body, final=final),
        grid=(n // tm,),
        in_specs=[pl.BlockSpec((tm, D_MODEL), lambda i: (i, 0)),
                  pl.BlockSpec((tm, ATT_W), lambda i: (i, 0)),
                  pl.BlockSpec((tm, RW_W), lambda i: (i, 0)),
                  resident(wo), resident(n2), resident(wg), resident(wu), resident(wd), resident(nf)],
        out_specs=pl.BlockSpec((tm, D_MODEL), lambda i: (i, 0)),
        out_shape=jax.ShapeDtypeStruct((n, D_MODEL), F32),
        compiler_params=_params(1),
        name="outffn_final" if final else "outffn",
    )(x, att, rw, wo, n2, wg, wu, wd, nf)


def _t5_bucket_np(d):
    max_exact = NUM_BUCKETS // 2
    n = np.maximum(d, 0)
    large = max_exact + (np.log(np.maximum(n, 1) / max_exact) / np.log(MAX_DISTANCE / max_exact)
                         * (NUM_BUCKETS - max_exact)).astype(np.int32)
    large = np.minimum(large, NUM_BUCKETS - 1)
    return np.where(n < max_exact, n, large).astype(np.int32)


def _bias_body(tab_ref, bucket_ref, inside_ref, o_ref, *, keys_major):
    bucket = bucket_ref[...]
    inside = inside_ref[...] > 0
    for h in range(N_Q_HEADS):
        acc = jnp.zeros(bucket.shape, F32)
        for bk in range(NUM_BUCKETS):
            acc = jnp.where(bucket == bk, tab_ref[bk, h], acc)
        acc = jnp.where(inside, acc, NEG)
        if keys_major:
            kv, g, q = h // GQA, h % GQA, bucket.shape[1]
            o_ref[kv, :, g * q:(g + 1) * q] = acc
        else:
            o_ref[h] = acc


def _masked_bias(rel_table, d, inside, rows_per_head, keys_major=False):
    q, s = d.shape
    bucket = np.zeros((rows_per_head, s), np.int32)
    bucket[:q] = _t5_bucket_np(d)
    ins = np.zeros((rows_per_head, s), np.int32)
    ins[:q] = inside
    if keys_major:
        bucket, ins = bucket.T.copy(), ins.T.copy()
        out_shape = (N_KV_HEADS, s, GQA * rows_per_head)
    else:
        out_shape = (N_Q_HEADS, rows_per_head, s)
    vm = lambda shape: pl.BlockSpec(shape, lambda: (0,) * len(shape))
    b = pl.pallas_call(
        functools.partial(_bias_body, keys_major=keys_major),
        in_specs=[pl.BlockSpec(memory_space=pltpu.SMEM), vm(bucket.shape), vm(ins.shape)],
        out_specs=vm(out_shape),
        out_shape=jax.ShapeDtypeStruct(out_shape, F32),
        name="rel_bias",
    )(rel_table.astype(F32), jnp.asarray(bucket), jnp.asarray(ins))
    return b if keys_major else b.reshape(N_KV_HEADS, GQA * rows_per_head, s)


def _sink_rows(sink_l, rows_per_head):
    return jnp.repeat(sink_l.astype(F32).reshape(N_KV_HEADS, GQA, 1), rows_per_head, axis=1).reshape(
        N_KV_HEADS, GQA * rows_per_head, 1)


def kernel(x_prompt, x_sample, cache_k, cache_v, state_shift, state_wkv, norm1, w_in, w_out, sink,
           rel_table, mu, w0, w2, a0, a2, g2, k_k, k_a, r_k, gn_w, gn_b, norm2, w_gate, w_up,
           w_down, norm_f):
    depth = w_in.shape[0]
    B, T, _ = x_prompt.shape
    DB, S, _ = x_sample.shape
    wc = cache_k.shape[2]

    qi = np.arange(BLOCK)[:, None]
    sj = np.arange(2 * BLOCK)[None, :]
    dp = qi + BLOCK - sj
    inside_p = (dp >= 0) & (dp <= WINDOW)
    bias_p = jnp.stack([_masked_bias(rel_table, dp, inside_p & (sj >= BLOCK), BLOCK, keys_major=True),
                        _masked_bias(rel_table, dp, inside_p, BLOCK, keys_major=True)])
    ds_full = (wc + np.arange(S))[:, None] - np.arange(wc + S)[None, :]
    inside_s = (ds_full >= 0) & (ds_full <= WINDOW)
    bias_c = _masked_bias(rel_table, ds_full[:, :wc], inside_s[:, :wc], SAMPLE_PAD)
    dn = np.full((S, SAMPLE_PAD), -1, np.int64)
    dn[:, :S] = ds_full[:, wc:]
    inn = np.zeros((S, SAMPLE_PAD), bool)
    inn[:, :S] = inside_s[:, wc:]
    bias_n = _masked_bias(rel_table, dn, inn, SAMPLE_PAD)

    xp = x_prompt.reshape(B * T, D_MODEL)
    xs = x_sample.reshape(DB * S, D_MODEL)
    zeros_shift = jnp.zeros((B, 1, RW_IN_W), F32)
    zeros_state = jnp.zeros((1, B, RW_HEADS, HEAD_DIM, HEAD_DIM), F32)
    cache_k2 = cache_k.reshape(depth, DB, wc, KV_W)
    cache_v2 = cache_v.reshape(depth, DB, wc, KV_W)
    nf = norm_f.reshape(1, D_MODEL)
    ones = jnp.ones((1, D_MODEL), F32)

    kp, vp, sp, wp, ksl, vsl, ssl, wsl = [], [], [], [], [], [], [], []
    for l in range(depth):
        w_in_b = w_in[l].astype(BF16)
        wo_b, wg_b, wu_b, wd_b = (w_out[l].astype(BF16), w_gate[l].astype(BF16),
                                  w_up[l].astype(BF16), w_down[l].astype(BF16))
        n1 = norm1[l].reshape(1, D_MODEL)
        n2 = norm2[l].reshape(1, D_MODEL)
        mu_l = mu[l].reshape(1, RW_IN_W)
        vecs = jnp.stack([w0[l], a0[l], k_k[l], k_a[l], r_k[l], gn_w[l], gn_b[l], jnp.zeros_like(w0[l])])
        w2_b, a2_b, g2_b = w2[l].astype(BF16), a2[l].astype(BF16), g2[l].astype(BF16)
        final = l == depth - 1

        za, zr, h_last, kv_last = _inproj(xp, n1, w_in_b, norm=True, tm=1024, seq_len=T)
        za3 = za.reshape(B, T, ATT_IN_W)
        kp.append(kv_last[:, :, :KV_W].reshape(B, WINDOW, N_KV_HEADS, HEAD_DIM))
        vp.append(kv_last[:, :, KV_W:].reshape(B, WINDOW, N_KV_HEADS, HEAD_DIM))
        sp.append(h_last.reshape(B, D_MODEL))
        sink_lanes = jnp.swapaxes(_sink_rows(sink[l], BLOCK), 1, 2)
        att = _attn_prompt(za3, bias_p, sink_lanes, nblk=4)
        rw, wkv = _rwkv(zr.reshape(B, T, RW_IN_W), zeros_shift, zeros_state, mu_l, vecs, w2_b, a2_b, g2_b,
                        layer=0, chunk=PROMPT_CHUNK, n_valid=PROMPT_CHUNK, nb=8)
        wp.append(wkv)
        xp = _outffn(xp, att.reshape(B * T, ATT_W), rw.reshape(B * T, RW_W), wo_b, n2, wg_b, wu_b, wd_b, nf,
                     final=final, tm=512)

        za_s, zr_s, h_s = _inproj(xs, n1, w_in_b, norm=True, tm=DB * S)
        _, zr_prev, _ = _inproj(state_shift[l], ones, w_in_b, norm=False, tm=DB)
        ssl.append(h_s.reshape(DB, S, D_MODEL)[:, -1])
        att_s, kb, vb = _attn_sample(za_s.reshape(DB, S, ATT_IN_W), cache_k2, cache_v2,
                                     bias_c, bias_n, _sink_rows(sink[l], SAMPLE_PAD), layer=l, ns=8)
        ksl.append(kb.reshape(DB, wc, N_KV_HEADS, HEAD_DIM))
        vsl.append(vb.reshape(DB, wc, N_KV_HEADS, HEAD_DIM))
        rw_s, wkv_s = _rwkv(zr_s.reshape(DB, S, RW_IN_W), zr_prev.reshape(DB, 1, RW_IN_W), state_wkv,
                            mu_l, vecs, w2_b, a2_b, g2_b, layer=l, chunk=SAMPLE_CHUNK, n_valid=S, nb=4)
        wsl.append(wkv_s)
        xs = _outffn(xs, att_s.reshape(DB * S, ATT_W), rw_s.reshape(DB * S, RW_W),
                     wo_b, n2, wg_b, wu_b, wd_b, nf, final=final, tm=DB * S)

    y_prompt = xp.reshape(B, T, D_MODEL)
    y_sample = xs.reshape(DB, S, D_MODEL)
    return (y_prompt, y_sample, jnp.stack(kp), jnp.stack(vp), jnp.stack(sp), jnp.stack(wp),
            jnp.stack(ksl), jnp.stack(vsl), jnp.stack(ssl), jnp.stack(wsl))
```

```python
import functools

import numpy as np
import jax
import jax.numpy as jnp
from jax import lax
from jax.experimental import pallas as pl
from jax.experimental.pallas import tpu as pltpu

D_MODEL = 1024
HEAD_DIM = 64
N_Q_HEADS = 8
N_KV_HEADS = 2
GQA = N_Q_HEADS // N_KV_HEADS
ATT_W = N_Q_HEADS * HEAD_DIM
KV_W = N_KV_HEADS * HEAD_DIM
WINDOW = 128
BLOCK = 128
NUM_BUCKETS = 32
MAX_DISTANCE = 128
RW_HEADS = 8
RW_W = RW_HEADS * HEAD_DIM
DECAY_LORA = 64
ICLR_LORA = 64
GATE_LORA = 128
RW_IN_W = 3 * RW_W + DECAY_LORA + ICLR_LORA + GATE_LORA
ATT_IN_W = ATT_W + 2 * KV_W
NORM_EPS = 1e-6
GN_EPS = 64e-5
NEG = -1e30
SCALE = HEAD_DIM ** -0.5

V7X_VMEM_LIMIT_BYTES = 56 * 1024 * 1024

V7X_LANES = 128
V7X_SUBLANES = 8

PROMPT_CHUNK = 64
SAMPLE_CHUNK = 16
SAMPLE_PAD = V7X_SUBLANES
INPROJ_ROWS = 1024
OUTFFN_ROWS = 512
ATTN_BLOCKS = 4
ATTN_SAMPLE_SEQS = 8
RWKV_PROMPT_SEQS = 8
RWKV_SAMPLE_SEQS = 4
RWKV_CHAIN_PARTS = 4

BF16 = jnp.bfloat16
F32 = jnp.float32


def _mm(a, b):
    return jnp.dot(a.astype(BF16), b.astype(BF16), preferred_element_type=F32)


def _mm_nt(a, b):
    return lax.dot_general(a.astype(BF16), b.astype(BF16), (((1,), (1,)), ((), ())),
                           preferred_element_type=F32)


def _rmsnorm(x, g):
    return x * lax.rsqrt(jnp.mean(x * x, -1, keepdims=True) + NORM_EPS) * g


def _params(n_axes):
    return pltpu.CompilerParams(dimension_semantics=("arbitrary",) * n_axes,
                                vmem_limit_bytes=V7X_VMEM_LIMIT_BYTES)


def _inproj_body(x_ref, g_ref, w_ref, za_ref, zr_ref, h_ref, *tail_refs, norm, seq_len):
    x = x_ref[...]
    h = _rmsnorm(x, g_ref[...]) if norm else x
    z = jnp.dot(h.astype(BF16), w_ref[...], preferred_element_type=F32)
    za_ref[...] = z[:, :ATT_IN_W]
    zr_ref[...] = z[:, ATT_IN_W:]
    if seq_len is None:
        h_ref[...] = h
    else:
        (kv_ref,) = tail_refs
        tm = x.shape[0]
        h_ref[0] = h[tm - 1:, :]
        kv_ref[0] = z[tm - WINDOW:, ATT_W:ATT_IN_W]


def _inproj(x, g, w, *, norm, tm, seq_len=None):
    n = x.shape[0]
    in_w = w.shape[1]
    if seq_len is None:
        tail_specs = [pl.BlockSpec((tm, D_MODEL), lambda i: (i, 0))]
        tail_shapes = [jax.ShapeDtypeStruct((n, D_MODEL), F32)]
    else:
        tiles = seq_len // tm
        tail_specs = [pl.BlockSpec((1, 1, D_MODEL), lambda i: (i // tiles, 0, 0)),
                      pl.BlockSpec((1, WINDOW, 2 * KV_W), lambda i: (i // tiles, 0, 0))]
        tail_shapes = [jax.ShapeDtypeStruct((n // seq_len, 1, D_MODEL), F32),
                       jax.ShapeDtypeStruct((n // seq_len, WINDOW, 2 * KV_W), F32)]
    return pl.pallas_call(
        functools.partial(_inproj_body, norm=norm, seq_len=seq_len),
        grid=(n // tm,),
        in_specs=[pl.BlockSpec((tm, D_MODEL), lambda i: (i, 0)),
                  pl.BlockSpec((1, D_MODEL), lambda i: (0, 0)),
                  pl.BlockSpec((D_MODEL, in_w), lambda i: (0, 0))],
        out_specs=[pl.BlockSpec((tm, ATT_IN_W), lambda i: (i, 0)),
                   pl.BlockSpec((tm, RW_IN_W), lambda i: (i, 0))] + tail_specs,
        out_shape=[jax.ShapeDtypeStruct((n, ATT_IN_W), F32),
                   jax.ShapeDtypeStruct((n, RW_IN_W), F32)] + tail_shapes,
        compiler_params=_params(1),
        name="inproj_norm" if norm else "inproj_plain",
    )(x, g, w)


def _sink_softmax(score_parts, sink_col):
    m = sink_col
    for s in score_parts:
        m = jnp.maximum(m, jnp.max(s, -1, keepdims=True))
    es = [jnp.exp(s - m) for s in score_parts]
    den = jnp.exp(sink_col - m)
    for e in es:
        den = den + jnp.sum(e, -1, keepdims=True)
    return [e / den for e in es]


QG_W = GQA * HEAD_DIM


def _tile_kv_heads(x):
    rot = pltpu.roll(x, HEAD_DIM, 1)
    first = lax.broadcasted_iota(jnp.int32, x.shape, 1) < HEAD_DIM
    halves = (jnp.where(first, x, rot).astype(BF16), jnp.where(first, rot, x).astype(BF16))
    return [jnp.concatenate([h] * (QG_W // KV_W), axis=1) for h in halves]


def _attn_prompt_body(q_ref, kv_ref, bias_ref, sink_ref, o_ref, kt_prev, vT_prev, *, nblk):
    j = pl.program_id(1)

    @pl.when(j == 0)
    def _():
        kt_prev[...] = jnp.zeros_like(kt_prev)
        vT_prev[...] = jnp.zeros_like(vT_prev)

    lane = lax.broadcasted_iota(jnp.int32, (1, V7X_LANES), 1)
    lane_half = ((lane < HEAD_DIM).astype(BF16), (lane >= HEAD_DIM).astype(BF16))
    kts = [[kt_prev[kv] for kv in range(N_KV_HEADS)]]
    vTs = [vT_prev[...]]
    for i in range(nblk):
        rows = slice(i * BLOCK, (i + 1) * BLOCK)
        kts.append(_tile_kv_heads(kv_ref[0, rows, 0:KV_W]))
        vTs.append(kv_ref[0, rows, KV_W:2 * KV_W].T.astype(BF16))
    for kv in range(N_KV_HEADS):
        kt_prev[kv] = kts[nblk][kv]
    vT_prev[...] = vTs[nblk]

    units = []
    for i in range(nblk):
        qb = (q_ref[0, i * BLOCK:(i + 1) * BLOCK, :] * SCALE).astype(BF16)
        first = jnp.where(j == 0, 0, 1) if i == 0 else 1
        for kv in range(N_KV_HEADS):
            qe = _expand(qb[:, kv * QG_W:(kv + 1) * QG_W], lane_half)
            kcat = jnp.concatenate([kts[i][kv], kts[i + 1][kv]], axis=0)
            units.append(dict(i=i, kv=kv, s=_mm_nt(kcat, qe), first=first))
    for u in units:
        b = bias_ref[u["first"], u["kv"]]
        s = jnp.where(b > 0.5 * NEG, u["s"] + b, NEG)
        sink_row = sink_ref[u["kv"]]
        m = jnp.maximum(jnp.max(s, 0, keepdims=True), sink_row)
        e = jnp.exp(s - m)
        den = jnp.sum(e, 0, keepdims=True) + jnp.exp(sink_row - m)
        u["p"] = (e * (1.0 / den)).astype(BF16)
    pieces = [[None] * N_Q_HEADS for _ in range(nblk)]
    for u in units:
        i, kv = u["i"], u["kv"]
        hd = slice(kv * HEAD_DIM, (kv + 1) * HEAD_DIM)
        vcatT = jnp.concatenate([vTs[i][hd], vTs[i + 1][hd]], axis=1)
        oT = jnp.dot(vcatT, u["p"], preferred_element_type=F32)
        for g in range(GQA):
            pieces[i][kv * GQA + g] = oT[:, g * BLOCK:(g + 1) * BLOCK]
    for i in range(nblk):
        o_ref[0, i * BLOCK:(i + 1) * BLOCK, :] = jnp.concatenate(pieces[i], axis=0).T


def _attn_prompt(za, bias, sink_rows, *, nblk):
    b, t, _ = za.shape
    rows = nblk * BLOCK
    full = lambda a: pl.BlockSpec(a.shape, lambda i, j: (0,) * a.ndim)
    return pl.pallas_call(
        functools.partial(_attn_prompt_body, nblk=nblk),
        grid=(b, t // rows),
        in_specs=[pl.BlockSpec((1, rows, ATT_W), lambda i, j: (i, j, 0)),
                  pl.BlockSpec((1, rows, 2 * KV_W), lambda i, j: (i, j, ATT_W // (2 * KV_W))),
                  full(bias), full(sink_rows)],
        out_specs=pl.BlockSpec((1, rows, ATT_W), lambda i, j: (i, j, 0)),
        out_shape=jax.ShapeDtypeStruct((b, t, ATT_W), F32),
        scratch_shapes=[pltpu.VMEM((N_KV_HEADS, BLOCK, QG_W), BF16),
                        pltpu.VMEM((KV_W, BLOCK), BF16)],
        compiler_params=_params(2),
        name="attn_prompt",
    )(za, za, bias, sink_rows)


def _attn_sample_body(z_ref, ck_ref, cv_ref, bias_c_ref, bias_n_ref, sink_ref,
                      o_ref, kb_ref, vb_ref, *, n_new, ns):
    units = []
    for s in range(ns):
        z = jnp.concatenate([z_ref[s], jnp.zeros((SAMPLE_PAD - n_new, ATT_IN_W), F32)], axis=0)
        q = z[:, :ATT_W]
        kn = z[:, ATT_W:ATT_W + KV_W]
        vn = z[:, ATT_W + KV_W:]
        ck, cv = ck_ref[s], cv_ref[s]
        keep = WINDOW - n_new
        kb_ref[s, 0:keep, :] = ck_ref[s, n_new:WINDOW, :]
        kb_ref[s, keep:WINDOW, :] = kn[0:n_new]
        vb_ref[s, 0:keep, :] = cv_ref[s, n_new:WINDOW, :]
        vb_ref[s, keep:WINDOW, :] = vn[0:n_new]
        for kv in range(N_KV_HEADS):
            sl = slice(kv * HEAD_DIM, (kv + 1) * HEAD_DIM)
            qg = jnp.concatenate(
                [q[:, (kv * GQA + g) * HEAD_DIM:(kv * GQA + g + 1) * HEAD_DIM] for g in range(GQA)], axis=0)
            units.append(dict(s=s, kv=kv, qg=qg, ck=ck[:, sl], kn=kn[:, sl], cv=cv[:, sl], vn=vn[:, sl]))
    for u in units:
        u["sc"] = _mm_nt(u["qg"], u["ck"]) * SCALE
        u["sn"] = _mm_nt(u["qg"], u["kn"]) * SCALE
    for u in units:
        bc, bn = bias_c_ref[u["kv"]], bias_n_ref[u["kv"]]
        sc = jnp.where(bc > 0.5 * NEG, u["sc"] + bc, NEG)
        sn = jnp.where(bn > 0.5 * NEG, u["sn"] + bn, NEG)
        u["pc"], u["pn"] = _sink_softmax([sc, sn], sink_ref[u["kv"]])
    outs = [[] for _ in range(ns)]
    for u in units:
        o = _mm(u["pc"], u["cv"]) + _mm(u["pn"], u["vn"])
        outs[u["s"]] += [o[g * SAMPLE_PAD:(g + 1) * SAMPLE_PAD] for g in range(GQA)]
    for s in range(ns):
        o_ref[s] = jnp.concatenate(outs[s], axis=1)[:n_new]


def _attn_sample(za, ck, cv, bias_c, bias_n, sink_rows, *, layer, ns):
    db, n_new, _ = za.shape
    full = lambda a: pl.BlockSpec(a.shape, lambda i: (0,) * a.ndim)
    return pl.pallas_call(
        functools.partial(_attn_sample_body, n_new=n_new, ns=ns),
        grid=(db // ns,),
        in_specs=[pl.BlockSpec((ns, n_new, ATT_IN_W), lambda i: (i, 0, 0)),
                  pl.BlockSpec((None, ns, WINDOW, KV_W), lambda i: (layer, i, 0, 0)),
                  pl.BlockSpec((None, ns, WINDOW, KV_W), lambda i: (layer, i, 0, 0)),
                  full(bias_c), full(bias_n), full(sink_rows)],
        out_specs=[pl.BlockSpec((ns, n_new, ATT_W), lambda i: (i, 0, 0)),
                   pl.BlockSpec((ns, WINDOW, KV_W), lambda i: (i, 0, 0)),
                   pl.BlockSpec((ns, WINDOW, KV_W), lambda i: (i, 0, 0))],
        out_shape=[jax.ShapeDtypeStruct((db, n_new, ATT_W), F32),
                   jax.ShapeDtypeStruct((db, WINDOW, KV_W), F32),
                   jax.ShapeDtypeStruct((db, WINDOW, KV_W), F32)],
        compiler_params=_params(1),
        name="attn_sample",
    )(za, ck, cv, bias_c, bias_n, sink_rows)


GROUP_HEADS = 4
GROUP_W = GROUP_HEADS * HEAD_DIM
N_GROUPS = RW_HEADS // GROUP_HEADS


def _head_sums(x, ones_blk):
    m = x.shape[0]
    hi = x.astype(BF16)
    lo = (x - hi.astype(F32)).astype(BF16)
    hl = jnp.concatenate([hi, lo], axis=0)
    y = jnp.concatenate(
        [jnp.dot(hl[:, g * GROUP_W:(g + 1) * GROUP_W], ones_blk, preferred_element_type=F32)
         for g in range(N_GROUPS)], axis=1)
    return y[:m] + y[m:]


def _expand(x, lane_half):
    xb = x.astype(BF16)
    c = xb.shape[0]
    zero = jnp.zeros((c, V7X_LANES), BF16)
    blocks = []
    for h in range(GROUP_HEADS):
        col = h // 2
        piece = xb[:, col * V7X_LANES:(col + 1) * V7X_LANES] * lane_half[h % 2]
        blocks.append(jnp.concatenate([piece, zero] if col == 0 else [zero, piece], axis=1))
    return jnp.concatenate(blocks, axis=0)


def _pack_state(s, g):
    return jnp.concatenate([s[g * GROUP_HEADS + h] for h in range(GROUP_HEADS)], axis=1)


def _heads_to_lanes(x, c):
    heads = x.shape[0] // HEAD_DIM
    return jnp.concatenate([x[h * HEAD_DIM:(h + 1) * HEAD_DIM, :c] for h in range(heads)], axis=1)


def _rwkv_body(z_ref, zpf_ref, s0_ref, mu_ref, vec_ref, w2_ref, a2_ref, g2_ref,
               blkp_ref, strict_ref, incl_ref, ones_ref,
               o_ref, sout_ref, state, zbuf, ar_s, bk_s, vt_s, gl_s, bonus_s, gate_s,
               *, chunk, n_valid, solve_steps, nb, pipelined, chain_parts):
    C = chunk
    R = GROUP_HEADS * C
    j = pl.program_id(1)
    last = pl.num_programs(1) - 1
    wslot = j % 2 if pipelined else 0
    rslot = 1 - wslot if pipelined else 0

    @pl.when(j == 0)
    def _():
        for b in range(nb):
            s0 = s0_ref[b]
            for g in range(N_GROUPS):
                state[b, g] = _pack_state(s0, g)
            zbuf[b, 7:8, :] = zpf_ref[b]
        if pipelined:
            ar_s[1] = jnp.zeros(ar_s.shape[1:], BF16)
            bk_s[1] = jnp.zeros(bk_s.shape[1:], BF16)
            vt_s[1] = jnp.zeros(vt_s.shape[1:], BF16)
            gl_s[1] = jnp.ones(gl_s.shape[1:], F32)
            bonus_s[1] = jnp.zeros(bonus_s.shape[1:], F32)
            gate_s[1] = jnp.zeros(gate_s.shape[1:], F32)

    @pl.when(j > 0)
    def _():
        for b in range(nb):
            zbuf[b, 7:8, :] = zbuf[b, C + 7:C + 8, :]

    w0, a0, k_k, k_a = vec_ref[0:1, :], vec_ref[1:2, :], vec_ref[2:3, :], vec_ref[3:4, :]
    r_k, gn_w, gn_b = vec_ref[4:5, :], vec_ref[5:6, :], vec_ref[6:7, :]
    ones_blk = ones_ref[...]
    ri = lax.broadcasted_iota(jnp.int32, (C, C), 0)
    ci = lax.broadcasted_iota(jnp.int32, (C, C), 1)
    tri = (ri >= ci).astype(BF16)
    lane = lax.broadcasted_iota(jnp.int32, (1, V7X_LANES), 1)
    blk = ((lane < HEAD_DIM).astype(BF16), (lane >= HEAD_DIM).astype(BF16))

    def prep_stages(b):
        p = {}

        def s1():
            zc = z_ref[b]
            if zc.shape[0] < C:
                zc = jnp.concatenate([zc, jnp.zeros((C - zc.shape[0], RW_IN_W), F32)], axis=0)
            zbuf[b, 8:8 + C, :] = zc
            zp = zbuf[b, 7:7 + C, :]
            zm = zc + (zp - zc) * mu_ref[...]
            p["r"] = zm[:, 0:RW_W]
            p["k"] = zm[:, RW_W:2 * RW_W]
            p["v"] = zm[:, 2 * RW_W:3 * RW_W]
            o1 = 3 * RW_W
            p["zw"] = zm[:, o1:o1 + DECAY_LORA]
            p["za"] = zm[:, o1 + DECAY_LORA:o1 + DECAY_LORA + ICLR_LORA]
            p["zg"] = zm[:, o1 + DECAY_LORA + ICLR_LORA:]

        def s2():
            y = -(w0 + _mm(jnp.tanh(p["zw"]), w2_ref[...]))
            softplus = jnp.maximum(y, 0.0) + jnp.log(1.0 + jnp.exp(-jnp.abs(y)))
            p["logdecay"] = -jnp.exp(-softplus - 0.5)
            p["a"] = jax.nn.sigmoid(a0 + _mm(p["za"], a2_ref[...]))
            gate_s[wslot, b] = _mm(jax.nn.sigmoid(p["zg"]), g2_ref[...])

        def s3():
            k, v, a = p["k"], p["v"], p["a"]
            kk_raw = k * k_k
            kmod = k * (1.0 + (a - 1.0) * k_a)
            vv = v
            if n_valid < C:
                rows = lax.broadcasted_iota(jnp.int32, (C, 1), 0) < n_valid
                p["logdecay"] = jnp.where(rows, p["logdecay"], 0.0)
                kk_raw = jnp.where(rows, kk_raw, 0.0)
                kmod = jnp.where(rows, kmod, 0.0)
                vv = jnp.where(rows, v, 0.0)
            vpad = vv
            if C < V7X_LANES:
                vpad = jnp.concatenate([vv, jnp.zeros((V7X_LANES - C, RW_W), F32)], axis=0)
            vt_s[wslot, b] = _heads_to_lanes(vpad.T, C).astype(BF16)
            sums = _head_sums(jnp.concatenate([kk_raw * kk_raw, p["r"] * kmod * r_k], axis=0), ones_blk)
            p["kk"] = kk_raw * lax.rsqrt(jnp.maximum(sums[:C], 1e-24))
            bonus_s[wslot, b] = sums[C:] * v
            p["kmod"] = kmod

        def s4():
            logdecay = p["logdecay"]
            hi = logdecay.astype(BF16)
            lo = (logdecay - hi.astype(F32)).astype(BF16)
            cum = jnp.dot(tri, jnp.concatenate([hi, lo], axis=1), preferred_element_type=F32)
            cum = cum[:, :RW_W] + cum[:, RW_W:]
            p["cum"] = cum
            gam = jnp.exp(cum)
            p["gam"] = gam
            gl_s[wslot, b] = jnp.broadcast_to(gam[C - 1:C, :], (8, RW_W))

        def s5():
            cum, gam, kk = p["cum"], p["gam"], p["kk"]
            gam_inv = jnp.exp(-cum)
            At = -kk * jnp.exp(cum - p["logdecay"])
            Rt = p["r"] * gam
            Bt = kk * p["a"] * gam_inv
            Kt = p["kmod"] * gam_inv
            ar_s[wslot, b] = jnp.concatenate([At, Rt], axis=0).astype(BF16)
            bk_s[wslot, b] = jnp.concatenate([Bt, Kt], axis=0).astype(BF16)

        return [s1, s2, s3, s4, s5]

    strict = strict_ref[...]
    incl = incl_ref[...]
    blk_p = blkp_ref[...]
    chains = [dict(b=b, grp=grp, sl=slice(grp * GROUP_W, (grp + 1) * GROUP_W),
                   slt=slice(grp * R, (grp + 1) * R))
              for b in range(nb) for grp in range(N_GROUPS)]
    on_parts = {b: [None] * N_GROUPS for b in range(nb)}

    def expand_t(x):
        if C == HEAD_DIM:
            return _expand(x, blk)
        return jnp.concatenate([x.astype(BF16)] * GROUP_HEADS, axis=0) * blk_p

    def c_load(chs):
        for ch in chs:
            b, sl = ch["b"], ch["sl"]
            ar = ar_s[rslot, b, :, sl]
            ch["BK"] = bk_s[rslot, b, :, sl]
            ch["VT"] = vt_s[rslot, b, :, ch["slt"]]
            ch["ARe"] = jnp.concatenate([_expand(ar[:C], blk), _expand(ar[C:], blk)], axis=0)
            ch["BKe"] = jnp.concatenate([_expand(ch["BK"][:C], blk), _expand(ch["BK"][C:], blk)], axis=0)

    def c_gram(chs):
        for ch in chs:
            ch["S"] = state[ch["b"], ch["grp"]]
            lhs = jnp.concatenate([ch["BK"], ch["S"].astype(BF16)], axis=0)
            GT = _mm_nt(lhs, ch["ARe"])
            ch["Q"] = (GT[:C, :R] * strict).astype(BF16)
            ch["LakT"] = (GT[C:2 * C, :R] * strict).astype(BF16)
            ch["MT"] = jnp.concatenate([expand_t(GT[:C, R:] * incl), expand_t(GT[C:2 * C, R:] * incl)], axis=0)
            ch["P0T"] = GT[2 * C:, :]

    def c_rhs(chs):
        for ch in chs:
            ch["XT"] = ch["P0T"][:, :R] + jnp.dot(ch["VT"], expand_t(ch["LakT"]), preferred_element_type=F32)

    def c_solve(i):
        def step(chs):
            for ch in chs:
                qe = expand_t(ch["Q"])
                xb = ch["XT"].astype(BF16)
                if i + 1 < solve_steps:
                    res = jnp.dot(jnp.concatenate([xb, ch["Q"]], axis=0), qe, preferred_element_type=F32)
                    ch["XT"] = ch["XT"] + res[:HEAD_DIM]
                    ch["Q"] = res[HEAD_DIM:].astype(BF16)
                else:
                    ch["XT"] = ch["XT"] + jnp.dot(xb, qe, preferred_element_type=F32)
        return step

    def c_out(chs):
        for ch in chs:
            b, grp, sl = ch["b"], ch["grp"], ch["sl"]
            uv = jnp.concatenate([ch["XT"].astype(BF16), ch["VT"]], axis=1)
            rhs = jnp.concatenate([ch["MT"], ch["BKe"]], axis=1)
            res = jnp.dot(uv, rhs, preferred_element_type=F32)
            state[b, grp] = (ch["S"] + res[:, R:]) * gl_s[rslot, b, 0:1, sl]
            OT = ch["P0T"][:, R:] + res[:, :R]
            mean = jnp.mean(OT, 0, keepdims=True)
            d = OT - mean
            var = jnp.mean(d * d, 0, keepdims=True)
            onT = d * lax.rsqrt(var + GN_EPS)
            rows = jnp.concatenate([onT[:, h * C:(h + 1) * C] for h in range(GROUP_HEADS)], axis=0)
            if C < V7X_LANES:
                rows = jnp.concatenate([rows, jnp.zeros((GROUP_W, V7X_LANES - C), F32)], axis=1)
            on_parts[b][grp] = rows.T[:C]

    def c_tail(chs):
        for b in sorted({ch["b"] for ch in chs}):
            on = jnp.concatenate(on_parts[b], axis=1)
            out = (on * gn_w + gn_b + bonus_s[rslot, b]) * gate_s[rslot, b]
            o_ref[b] = out[:o_ref.shape[1]]

    chain = [c_load, c_gram, c_rhs] + [c_solve(i) for i in range(solve_steps)] + [c_out, c_tail]
    prep = [s for b in range(nb) for s in prep_stages(b)]
    if pipelined:
        per = -(-len(chains) // chain_parts)
        slots = [(stage, chains[i:i + per]) for stage in chain for i in range(0, len(chains), per)]
        done = 0
        for n, (stage, chs) in enumerate(slots):
            stage(chs)
            upto = (len(prep) * (n + 1)) // len(slots)
            for s in prep[done:upto]:
                s()
            done = upto
    else:
        for s in prep:
            s()
        for stage in chain:
            stage(chains)

    @pl.when(j == last)
    def _():
        for b in range(nb):
            for g in range(N_GROUPS):
                s = state[b, g]
                for h in range(GROUP_HEADS):
                    sout_ref[b, g * GROUP_HEADS + h] = s[:, h * HEAD_DIM:(h + 1) * HEAD_DIM]


def _rwkv_masks(chunk):
    R = GROUP_HEADS * chunk
    row_h = np.arange(R) // chunk
    lane_h = np.arange(GROUP_W) // HEAD_DIM
    blk_p = row_h[:, None] == row_h[None, :]
    ones_blk = lane_h[:, None] == lane_h[None, :]
    s, t = np.arange(chunk)[:, None], (np.arange(R) % chunk)[None, :]
    return (jnp.asarray(blk_p, BF16), jnp.asarray(t > s, F32), jnp.asarray(t >= s, F32),
            jnp.asarray(ones_blk, BF16))


def _rwkv(zr, zp_first, s0, mu, vecs, w2, a2, g2, *, layer, chunk, n_valid, nb, chain_parts=1):
    b, t, _ = zr.shape
    rows = min(t, chunk)
    nc = t // rows
    pipelined = nc > 1
    lag = 1 if pipelined else 0
    solve_steps = max(1, (n_valid - 1).bit_length())
    masks = _rwkv_masks(chunk)
    full = lambda a: pl.BlockSpec(a.shape, lambda i, j: (0,) * a.ndim)
    return pl.pallas_call(
        functools.partial(_rwkv_body, chunk=chunk, n_valid=n_valid, solve_steps=solve_steps, nb=nb,
                          pipelined=pipelined, chain_parts=chain_parts),
        grid=(b // nb, nc + lag),
        in_specs=[pl.BlockSpec((nb, rows, RW_IN_W), lambda i, j: (i, jnp.minimum(j, nc - 1), 0)),
                  pl.BlockSpec((nb, 1, RW_IN_W), lambda i, j: (i, 0, 0)),
                  pl.BlockSpec((None, nb, RW_HEADS, HEAD_DIM, HEAD_DIM), lambda i, j: (layer, i, 0, 0, 0)),
                  full(mu), full(vecs), full(w2), full(a2), full(g2)] + [full(m) for m in masks],
        out_specs=[pl.BlockSpec((nb, rows, RW_W), lambda i, j: (i, jnp.maximum(j - lag, 0), 0)),
                   pl.BlockSpec((nb, RW_HEADS, HEAD_DIM, HEAD_DIM), lambda i, j: (i, 0, 0, 0))],
        out_shape=[jax.ShapeDtypeStruct((b, t, RW_W), F32),
                   jax.ShapeDtypeStruct((b, RW_HEADS, HEAD_DIM, HEAD_DIM), F32)],
        scratch_shapes=[pltpu.VMEM((nb, N_GROUPS, HEAD_DIM, GROUP_W), F32),
                        pltpu.VMEM((nb, chunk + 8, RW_IN_W), F32),
                        pltpu.VMEM((2, nb, 2 * chunk, RW_W), BF16),
                        pltpu.VMEM((2, nb, 2 * chunk, RW_W), BF16),
                        pltpu.VMEM((2, nb, HEAD_DIM, RW_HEADS * chunk), BF16),
                        pltpu.VMEM((2, nb, 8, RW_W), F32),
                        pltpu.VMEM((2, nb, chunk, RW_W), F32),
                        pltpu.VMEM((2, nb, chunk, RW_W), F32)],
        compiler_params=_params(2),
        name="rwkv_chunk%d" % chunk,
    )(zr, zp_first, s0, mu, vecs, w2, a2, g2, *masks)


def _outffn_body(x_ref, att_ref, rw_ref, wo_ref, n2_ref, wg_ref, wu_ref, wd_ref, nf_ref, xo_ref, *, final):
    mix = jnp.concatenate([att_ref[...], rw_ref[...]], axis=-1).astype(BF16)
    x1 = x_ref[...] + jnp.dot(mix, wo_ref[...], preferred_element_type=F32)
    hb = _rmsnorm(x1, n2_ref[...]).astype(BF16)
    gate = jnp.dot(hb, wg_ref[...], preferred_element_type=F32)
    up = jnp.dot(hb, wu_ref[...], preferred_element_type=F32)
    act = (gate * jax.nn.sigmoid(gate) * up).astype(BF16)
    x2 = x1 + jnp.dot(act, wd_ref[...], preferred_element_type=F32)
    xo_ref[...] = _rmsnorm(x2, nf_ref[...]) if final else x2


def _outffn(x, att, rw, wo, n2, wg, wu, wd, nf, *, final, tm):
    n = x.shape[0]
    resident = lambda a: pl.BlockSpec(a.shape, lambda i: (0,) * a.ndim, pipeline_mode=pl.Buffered(1))
    return pl.pallas_call(
        functools.partial(_outffn_body, final=final),
        grid=(n // tm,),
        in_specs=[pl.BlockSpec((tm, D_MODEL), lambda i: (i, 0)),
                  pl.BlockSpec((tm, ATT_W), lambda i: (i, 0)),
                  pl.BlockSpec((tm, RW_W), lambda i: (i, 0)),
                  resident(wo), resident(n2), resident(wg), resident(wu), resident(wd), resident(nf)],
        out_specs=pl.BlockSpec((tm, D_MODEL), lambda i: (i, 0)),
        out_shape=jax.ShapeDtypeStruct((n, D_MODEL), F32),
        compiler_params=_params(1),
        name="outffn_final" if final else "outffn",
    )(x, att, rw, wo, n2, wg, wu, wd, nf)


def _t5_bucket_np(d):
    max_exact = NUM_BUCKETS // 2
    n = np.maximum(d, 0)
    large = max_exact + (np.log(np.maximum(n, 1) / max_exact) / np.log(MAX_DISTANCE / max_exact)
                         * (NUM_BUCKETS - max_exact)).astype(np.int32)
    large = np.minimum(large, NUM_BUCKETS - 1)
    return np.where(n < max_exact, n, large).astype(np.int32)


def _bias_body(tab_ref, bucket_ref, inside_ref, o_ref, *, keys_major):
    bucket = bucket_ref[...]
    inside = inside_ref[...] > 0
    for h in range(N_Q_HEADS):
        acc = jnp.zeros(bucket.shape, F32)
        for bk in range(NUM_BUCKETS):
            acc = jnp.where(bucket == bk, tab_ref[bk, h], acc)
        acc = jnp.where(inside, acc, NEG)
        if keys_major:
            kv, g, q = h // GQA, h % GQA, bucket.shape[1]
            o_ref[kv, :, g * q:(g + 1) * q] = acc
        else:
            o_ref[h] = acc


def _masked_bias(rel_table, d, inside, rows_per_head, keys_major=False):
    q, s = d.shape
    bucket = np.zeros((rows_per_head, s), np.int32)
    bucket[:q] = _t5_bucket_np(d)
    ins = np.zeros((rows_per_head, s), np.int32)
    ins[:q] = inside
    if keys_major:
        bucket, ins = bucket.T.copy(), ins.T.copy()
        out_shape = (N_KV_HEADS, s, GQA * rows_per_head)
    else:
        out_shape = (N_Q_HEADS, rows_per_head, s)
    vm = lambda shape: pl.BlockSpec(shape, lambda: (0,) * len(shape))
    b = pl.pallas_call(
        functools.partial(_bias_body, keys_major=keys_major),
        in_specs=[pl.BlockSpec(memory_space=pltpu.SMEM), vm(bucket.shape), vm(ins.shape)],
        out_specs=vm(out_shape),
        out_shape=jax.ShapeDtypeStruct(out_shape, F32),
        name="rel_bias",
    )(rel_table.astype(F32), jnp.asarray(bucket), jnp.asarray(ins))
    return b if keys_major else b.reshape(N_KV_HEADS, GQA * rows_per_head, s)


def _sink_rows(sink_l, rows_per_head):
    return jnp.repeat(sink_l.astype(F32).reshape(N_KV_HEADS, GQA, 1), rows_per_head, axis=1).reshape(
        N_KV_HEADS, GQA * rows_per_head, 1)


def kernel(x_prompt, x_sample, cache_k, cache_v, state_shift, state_wkv, norm1, w_in, w_out, sink,
           rel_table, mu, w0, w2, a0, a2, g2, k_k, k_a, r_k, gn_w, gn_b, norm2, w_gate, w_up,
           w_down, norm_f):
    depth = w_in.shape[0]
    B, T, _ = x_prompt.shape
    DB, S, _ = x_sample.shape
    wc = cache_k.shape[2]
    assert wc == WINDOW and T >= WINDOW and S <= SAMPLE_PAD
    assert T % INPROJ_ROWS == 0 and (B * T) % OUTFFN_ROWS == 0 and T % (ATTN_BLOCKS * BLOCK) == 0
    assert T % PROMPT_CHUNK == 0 and B % RWKV_PROMPT_SEQS == 0
    assert DB % ATTN_SAMPLE_SEQS == 0 and DB % RWKV_SAMPLE_SEQS == 0

    qi = np.arange(BLOCK)[:, None]
    sj = np.arange(2 * BLOCK)[None, :]
    dp = qi + BLOCK - sj
    inside_p = (dp >= 0) & (dp <= WINDOW)
    bias_p = jnp.stack([_masked_bias(rel_table, dp, inside_p & (sj >= BLOCK), BLOCK, keys_major=True),
                        _masked_bias(rel_table, dp, inside_p, BLOCK, keys_major=True)])
    ds_full = (wc + np.arange(S))[:, None] - np.arange(wc + S)[None, :]
    inside_s = (ds_full >= 0) & (ds_full <= WINDOW)
    bias_c = _masked_bias(rel_table, ds_full[:, :wc], inside_s[:, :wc], SAMPLE_PAD)
    dn = np.full((S, SAMPLE_PAD), -1, np.int64)
    dn[:, :S] = ds_full[:, wc:]
    inn = np.zeros((S, SAMPLE_PAD), bool)
    inn[:, :S] = inside_s[:, wc:]
    bias_n = _masked_bias(rel_table, dn, inn, SAMPLE_PAD)

    xp = x_prompt.reshape(B * T, D_MODEL)
    xs = x_sample.reshape(DB * S, D_MODEL)
    zeros_shift = jnp.zeros((B, 1, RW_IN_W), F32)
    zeros_state = jnp.zeros((1, B, RW_HEADS, HEAD_DIM, HEAD_DIM), F32)
    cache_k2 = cache_k.reshape(depth, DB, wc, KV_W)
    cache_v2 = cache_v.reshape(depth, DB, wc, KV_W)
    nf = norm_f.reshape(1, D_MODEL)
    ones = jnp.ones((1, D_MODEL), F32)

    kp, vp, sp, wp, ksl, vsl, ssl, wsl = [], [], [], [], [], [], [], []
    for l in range(depth):
        w_in_b = w_in[l].astype(BF16)
        wo_b, wg_b, wu_b, wd_b = (w_out[l].astype(BF16), w_gate[l].astype(BF16),
                                  w_up[l].astype(BF16), w_down[l].astype(BF16))
        n1 = norm1[l].reshape(1, D_MODEL)
        n2 = norm2[l].reshape(1, D_MODEL)
        mu_l = mu[l].reshape(1, RW_IN_W)
        vecs = jnp.stack([w0[l], a0[l], k_k[l], k_a[l], r_k[l], gn_w[l], gn_b[l], jnp.zeros_like(w0[l])])
        w2_b, a2_b, g2_b = w2[l].astype(BF16), a2[l].astype(BF16), g2[l].astype(BF16)
        final = l == depth - 1

        za, zr, h_last, kv_last = _inproj(xp, n1, w_in_b, norm=True, tm=INPROJ_ROWS, seq_len=T)
        za3 = za.reshape(B, T, ATT_IN_W)
        kp.append(kv_last[:, :, :KV_W].reshape(B, WINDOW, N_KV_HEADS, HEAD_DIM))
        vp.append(kv_last[:, :, KV_W:].reshape(B, WINDOW, N_KV_HEADS, HEAD_DIM))
        sp.append(h_last.reshape(B, D_MODEL))
        sink_lanes = jnp.swapaxes(_sink_rows(sink[l], BLOCK), 1, 2)
        att = _attn_prompt(za3, bias_p, sink_lanes, nblk=ATTN_BLOCKS)
        rw, wkv = _rwkv(zr.reshape(B, T, RW_IN_W), zeros_shift, zeros_state, mu_l, vecs, w2_b, a2_b, g2_b,
                        layer=0, chunk=PROMPT_CHUNK, n_valid=PROMPT_CHUNK, nb=RWKV_PROMPT_SEQS,
                        chain_parts=RWKV_CHAIN_PARTS)
        wp.append(wkv)
        xp = _outffn(xp, att.reshape(B * T, ATT_W), rw.reshape(B * T, RW_W), wo_b, n2, wg_b, wu_b, wd_b, nf,
                     final=final, tm=OUTFFN_ROWS)

        za_s, zr_s, h_s = _inproj(xs, n1, w_in_b, norm=True, tm=DB * S)
        _, zr_prev, _ = _inproj(state_shift[l], ones, w_in_b, norm=False, tm=DB)
        ssl.append(h_s.reshape(DB, S, D_MODEL)[:, -1])
        att_s, kb, vb = _attn_sample(za_s.reshape(DB, S, ATT_IN_W), cache_k2, cache_v2,
                                     bias_c, bias_n, _sink_rows(sink[l], SAMPLE_PAD), layer=l, ns=ATTN_SAMPLE_SEQS)
        ksl.append(kb.reshape(DB, wc, N_KV_HEADS, HEAD_DIM))
        vsl.append(vb.reshape(DB, wc, N_KV_HEADS, HEAD_DIM))
        rw_s, wkv_s = _rwkv(zr_s.reshape(DB, S, RW_IN_W), zr_prev.reshape(DB, 1, RW_IN_W), state_wkv,
                            mu_l, vecs, w2_b, a2_b, g2_b, layer=l, chunk=SAMPLE_CHUNK, n_valid=S,
                            nb=RWKV_SAMPLE_SEQS)
        wsl.append(wkv_s)
        xs = _outffn(xs, att_s.reshape(DB * S, ATT_W), rw_s.reshape(DB * S, RW_W),
                     wo_b, n2, wg_b, wu_b, wd_b, nf, final=final, tm=DB * S)

    y_prompt = xp.reshape(B, T, D_MODEL)
    y_sample = xs.reshape(DB, S, D_MODEL)
    return (y_prompt, y_sample, jnp.stack(kp), jnp.stack(vp), jnp.stack(sp), jnp.stack(wp),
            jnp.stack(ksl), jnp.stack(vsl), jnp.stack(ssl), jnp.stack(wsl))
```

```python
import functools

import numpy as np
import jax
import jax.numpy as jnp
from jax import lax
from jax.experimental import pallas as pl
from jax.experimental.pallas import tpu as pltpu

D_MODEL = 1024
HEAD_DIM = 64
N_Q_HEADS = 8
N_KV_HEADS = 2
GQA = N_Q_HEADS // N_KV_HEADS
ATT_W = N_Q_HEADS * HEAD_DIM
KV_W = N_KV_HEADS * HEAD_DIM
WINDOW = 128
BLOCK = 128
NUM_BUCKETS = 32
MAX_DISTANCE = 128
RW_HEADS = 8
RW_W = RW_HEADS * HEAD_DIM
DECAY_LORA = 64
ICLR_LORA = 64
GATE_LORA = 128
RW_IN_W = 3 * RW_W + DECAY_LORA + ICLR_LORA + GATE_LORA
ATT_IN_W = ATT_W + 2 * KV_W
NORM_EPS = 1e-6
GN_EPS = 64e-5
NEG = -1e30
SCALE = HEAD_DIM ** -0.5

V7X_VMEM_LIMIT_BYTES = 56 * 1024 * 1024

V7X_LANES = 128
V7X_SUBLANES = 8

PROMPT_CHUNK = 64
SAMPLE_CHUNK = 16
SAMPLE_PAD = V7X_SUBLANES
INPROJ_ROWS = 1024
OUTFFN_ROWS = 512
ATTN_BLOCKS = 4
ATTN_SAMPLE_SEQS = 8
RWKV_PROMPT_SEQS = 8
RWKV_SAMPLE_SEQS = 4
CAST_ROWS = 256

BF16 = jnp.bfloat16
F32 = jnp.float32


def _mm(a, b):
    return jnp.dot(a.astype(BF16), b.astype(BF16), preferred_element_type=F32)


def _mm_nt(a, b):
    return lax.dot_general(a.astype(BF16), b.astype(BF16), (((1,), (1,)), ((), ())),
                           preferred_element_type=F32)


def _rmsnorm(x, g):
    return x * lax.rsqrt(jnp.mean(x * x, -1, keepdims=True) + NORM_EPS) * g


def _params(n_axes):
    return pltpu.CompilerParams(dimension_semantics=("arbitrary",) * n_axes,
                                vmem_limit_bytes=V7X_VMEM_LIMIT_BYTES)


def _cast_body(w_ref, o_ref):
    o_ref[...] = w_ref[...].astype(BF16)


def _to_bf16(w, layer):
    _, rows, cols = w.shape
    tr = CAST_ROWS if rows % CAST_ROWS == 0 else rows
    return pl.pallas_call(
        _cast_body,
        grid=(rows // tr,),
        in_specs=[pl.BlockSpec((None, tr, cols), lambda i: (layer, i, 0))],
        out_specs=pl.BlockSpec((tr, cols), lambda i: (i, 0)),
        out_shape=jax.ShapeDtypeStruct((rows, cols), BF16),
        compiler_params=_params(1),
        name="cast_bf16",
    )(w)


def _inproj_body(x_ref, g_ref, w_ref, za_ref, zr_ref, h_ref, *tail_refs, norm, seq_len):
    x = x_ref[...]
    h = _rmsnorm(x, g_ref[...]) if norm else x
    z = jnp.dot(h.astype(BF16), w_ref[...], preferred_element_type=F32)
    za_ref[...] = z[:, :ATT_IN_W]
    zr_ref[...] = z[:, ATT_IN_W:]
    if seq_len is None:
        h_ref[...] = h
    else:
        (kv_ref,) = tail_refs
        tm = x.shape[0]
        h_ref[0] = h[tm - 1:, :]
        kv_ref[0] = z[tm - WINDOW:, ATT_W:ATT_IN_W]


def _inproj(x, g, w, *, norm, tm, seq_len=None):
    n = x.shape[0]
    in_w = w.shape[1]
    if seq_len is None:
        tail_specs = [pl.BlockSpec((tm, D_MODEL), lambda i: (i, 0))]
        tail_shapes = [jax.ShapeDtypeStruct((n, D_MODEL), F32)]
    else:
        tiles = seq_len // tm
        tail_specs = [pl.BlockSpec((1, 1, D_MODEL), lambda i: (i // tiles, 0, 0)),
                      pl.BlockSpec((1, WINDOW, 2 * KV_W), lambda i: (i // tiles, 0, 0))]
        tail_shapes = [jax.ShapeDtypeStruct((n // seq_len, 1, D_MODEL), F32),
                       jax.ShapeDtypeStruct((n // seq_len, WINDOW, 2 * KV_W), F32)]
    return pl.pallas_call(
        functools.partial(_inproj_body, norm=norm, seq_len=seq_len),
        grid=(n // tm,),
        in_specs=[pl.BlockSpec((tm, D_MODEL), lambda i: (i, 0)),
                  pl.BlockSpec((1, D_MODEL), lambda i: (0, 0)),
                  pl.BlockSpec((D_MODEL, in_w), lambda i: (0, 0))],
        out_specs=[pl.BlockSpec((tm, ATT_IN_W), lambda i: (i, 0)),
                   pl.BlockSpec((tm, RW_IN_W), lambda i: (i, 0))] + tail_specs,
        out_shape=[jax.ShapeDtypeStruct((n, ATT_IN_W), F32),
                   jax.ShapeDtypeStruct((n, RW_IN_W), F32)] + tail_shapes,
        compiler_params=_params(1),
        name="inproj_norm" if norm else "inproj_plain",
    )(x, g, w)


def _sink_softmax(score_parts, sink_col):
    m = sink_col
    for s in score_parts:
        m = jnp.maximum(m, jnp.max(s, -1, keepdims=True))
    es = [jnp.exp(s - m) for s in score_parts]
    den = jnp.exp(sink_col - m)
    for e in es:
        den = den + jnp.sum(e, -1, keepdims=True)
    return [e / den for e in es]


QG_W = GQA * HEAD_DIM


def _tile_kv_heads(x):
    rot = pltpu.roll(x, HEAD_DIM, 1)
    first = lax.broadcasted_iota(jnp.int32, x.shape, 1) < HEAD_DIM
    halves = (jnp.where(first, x, rot).astype(BF16), jnp.where(first, rot, x).astype(BF16))
    return [jnp.concatenate([h] * (QG_W // KV_W), axis=1) for h in halves]


def _attn_prompt_body(q_ref, kv_ref, bias_ref, sink_ref, o_ref, kt_prev, vT_prev, *, nblk):
    j = pl.program_id(1)

    @pl.when(j == 0)
    def _():
        kt_prev[...] = jnp.zeros_like(kt_prev)
        vT_prev[...] = jnp.zeros_like(vT_prev)

    lane = lax.broadcasted_iota(jnp.int32, (1, V7X_LANES), 1)
    lane_half = ((lane < HEAD_DIM).astype(BF16), (lane >= HEAD_DIM).astype(BF16))
    kts = [[kt_prev[kv] for kv in range(N_KV_HEADS)]]
    vTs = [vT_prev[...]]
    for i in range(nblk):
        rows = slice(i * BLOCK, (i + 1) * BLOCK)
        kts.append(_tile_kv_heads(kv_ref[0, rows, 0:KV_W]))
        vTs.append(kv_ref[0, rows, KV_W:2 * KV_W].T.astype(BF16))
    for kv in range(N_KV_HEADS):
        kt_prev[kv] = kts[nblk][kv]
    vT_prev[...] = vTs[nblk]

    units = []
    for i in range(nblk):
        qb = (q_ref[0, i * BLOCK:(i + 1) * BLOCK, :] * SCALE).astype(BF16)
        first = jnp.where(j == 0, 0, 1) if i == 0 else 1
        for kv in range(N_KV_HEADS):
            qe = _expand(qb[:, kv * QG_W:(kv + 1) * QG_W], lane_half)
            kcat = jnp.concatenate([kts[i][kv], kts[i + 1][kv]], axis=0)
            units.append(dict(i=i, kv=kv, s=_mm_nt(kcat, qe), first=first))
    for u in units:
        b = bias_ref[u["first"], u["kv"]]
        s = jnp.where(b > 0.5 * NEG, u["s"] + b, NEG)
        sink_row = sink_ref[u["kv"]]
        m = jnp.maximum(jnp.max(s, 0, keepdims=True), sink_row)
        e = jnp.exp(s - m)
        den = jnp.sum(e, 0, keepdims=True) + jnp.exp(sink_row - m)
        u["p"] = (e * (1.0 / den)).astype(BF16)
    pieces = [[None] * N_Q_HEADS for _ in range(nblk)]
    for u in units:
        i, kv = u["i"], u["kv"]
        hd = slice(kv * HEAD_DIM, (kv + 1) * HEAD_DIM)
        vcatT = jnp.concatenate([vTs[i][hd], vTs[i + 1][hd]], axis=1)
        oT = jnp.dot(vcatT, u["p"], preferred_element_type=F32)
        for g in range(GQA):
            pieces[i][kv * GQA + g] = oT[:, g * BLOCK:(g + 1) * BLOCK]
    for i in range(nblk):
        o_ref[0, i * BLOCK:(i + 1) * BLOCK, :] = jnp.concatenate(pieces[i], axis=0).T


def _attn_prompt(za, bias, sink_rows, *, nblk):
    b, t, _ = za.shape
    rows = nblk * BLOCK
    full = lambda a: pl.BlockSpec(a.shape, lambda i, j: (0,) * a.ndim)
    return pl.pallas_call(
        functools.partial(_attn_prompt_body, nblk=nblk),
        grid=(b, t // rows),
        in_specs=[pl.BlockSpec((1, rows, ATT_W), lambda i, j: (i, j, 0)),
                  pl.BlockSpec((1, rows, 2 * KV_W), lambda i, j: (i, j, ATT_W // (2 * KV_W))),
                  full(bias), full(sink_rows)],
        out_specs=pl.BlockSpec((1, rows, ATT_W), lambda i, j: (i, j, 0)),
        out_shape=jax.ShapeDtypeStruct((b, t, ATT_W), F32),
        scratch_shapes=[pltpu.VMEM((N_KV_HEADS, BLOCK, QG_W), BF16),
                        pltpu.VMEM((KV_W, BLOCK), BF16)],
        compiler_params=_params(2),
        name="attn_prompt",
    )(za, za, bias, sink_rows)


def _attn_sample_body(z_ref, ck_ref, cv_ref, bias_c_ref, bias_n_ref, sink_ref,
                      o_ref, kb_ref, vb_ref, *, n_new, ns):
    units = []
    for s in range(ns):
        z = jnp.concatenate([z_ref[s], jnp.zeros((SAMPLE_PAD - n_new, ATT_IN_W), F32)], axis=0)
        q = z[:, :ATT_W]
        kn = z[:, ATT_W:ATT_W + KV_W]
        vn = z[:, ATT_W + KV_W:]
        ck, cv = ck_ref[s], cv_ref[s]
        keep = WINDOW - n_new
        kb_ref[s, 0:keep, :] = ck_ref[s, n_new:WINDOW, :]
        kb_ref[s, keep:WINDOW, :] = kn[0:n_new]
        vb_ref[s, 0:keep, :] = cv_ref[s, n_new:WINDOW, :]
        vb_ref[s, keep:WINDOW, :] = vn[0:n_new]
        for kv in range(N_KV_HEADS):
            sl = slice(kv * HEAD_DIM, (kv + 1) * HEAD_DIM)
            qg = jnp.concatenate(
                [q[:, (kv * GQA + g) * HEAD_DIM:(kv * GQA + g + 1) * HEAD_DIM] for g in range(GQA)], axis=0)
            units.append(dict(s=s, kv=kv, qg=qg, ck=ck[:, sl], kn=kn[:, sl], cv=cv[:, sl], vn=vn[:, sl]))
    for u in units:
        u["sc"] = _mm_nt(u["qg"], u["ck"]) * SCALE
        u["sn"] = _mm_nt(u["qg"], u["kn"]) * SCALE
    for u in units:
        bc, bn = bias_c_ref[u["kv"]], bias_n_ref[u["kv"]]
        sc = jnp.where(bc > 0.5 * NEG, u["sc"] + bc, NEG)
        sn = jnp.where(bn > 0.5 * NEG, u["sn"] + bn, NEG)
        u["pc"], u["pn"] = _sink_softmax([sc, sn], sink_ref[u["kv"]])
    outs = [[] for _ in range(ns)]
    for u in units:
        o = _mm(u["pc"], u["cv"]) + _mm(u["pn"], u["vn"])
        outs[u["s"]] += [o[g * SAMPLE_PAD:(g + 1) * SAMPLE_PAD] for g in range(GQA)]
    for s in range(ns):
        o_ref[s] = jnp.concatenate(outs[s], axis=1)[:n_new]


def _attn_sample(za, ck, cv, bias_c, bias_n, sink_rows, *, layer, ns):
    db, n_new, _ = za.shape
    full = lambda a: pl.BlockSpec(a.shape, lambda i: (0,) * a.ndim)
    return pl.pallas_call(
        functools.partial(_attn_sample_body, n_new=n_new, ns=ns),
        grid=(db // ns,),
        in_specs=[pl.BlockSpec((ns, n_new, ATT_IN_W), lambda i: (i, 0, 0)),
                  pl.BlockSpec((None, ns, WINDOW, KV_W), lambda i: (layer, i, 0, 0)),
                  pl.BlockSpec((None, ns, WINDOW, KV_W), lambda i: (layer, i, 0, 0)),
                  full(bias_c), full(bias_n), full(sink_rows)],
        out_specs=[pl.BlockSpec((ns, n_new, ATT_W), lambda i: (i, 0, 0)),
                   pl.BlockSpec((ns, WINDOW, KV_W), lambda i: (i, 0, 0)),
                   pl.BlockSpec((ns, WINDOW, KV_W), lambda i: (i, 0, 0))],
        out_shape=[jax.ShapeDtypeStruct((db, n_new, ATT_W), F32),
                   jax.ShapeDtypeStruct((db, WINDOW, KV_W), F32),
                   jax.ShapeDtypeStruct((db, WINDOW, KV_W), F32)],
        compiler_params=_params(1),
        name="attn_sample",
    )(za, ck, cv, bias_c, bias_n, sink_rows)


GROUP_HEADS = 4
GROUP_W = GROUP_HEADS * HEAD_DIM
N_GROUPS = RW_HEADS // GROUP_HEADS


def _head_sums(x, ones_blk):
    m = x.shape[0]
    hi = x.astype(BF16)
    lo = (x - hi.astype(F32)).astype(BF16)
    hl = jnp.concatenate([hi, lo], axis=0)
    y = jnp.concatenate(
        [jnp.dot(hl[:, g * GROUP_W:(g + 1) * GROUP_W], ones_blk, preferred_element_type=F32)
         for g in range(N_GROUPS)], axis=1)
    return y[:m] + y[m:]


def _expand(x, lane_half):
    xb = x.astype(BF16)
    c = xb.shape[0]
    zero = jnp.zeros((c, V7X_LANES), BF16)
    blocks = []
    for h in range(GROUP_HEADS):
        col = h // 2
        piece = xb[:, col * V7X_LANES:(col + 1) * V7X_LANES] * lane_half[h % 2]
        blocks.append(jnp.concatenate([piece, zero] if col == 0 else [zero, piece], axis=1))
    return jnp.concatenate(blocks, axis=0)


def _pack_state(s, g):
    return jnp.concatenate([s[g * GROUP_HEADS + h] for h in range(GROUP_HEADS)], axis=1)


def _heads_to_lanes(x, c):
    heads = x.shape[0] // HEAD_DIM
    return jnp.concatenate([x[h * HEAD_DIM:(h + 1) * HEAD_DIM, :c] for h in range(heads)], axis=1)


def _rwkv_body(z_ref, zpf_ref, s0_ref, mu_ref, vec_ref, w2_ref, a2_ref, g2_ref,
               blkp_ref, strict_ref, incl_ref, ones_ref,
               o_ref, sout_ref, state, zbuf, ar_s, bk_s, vt_s, gl_s, bonus_s, gate_s,
               *, chunk, n_valid, solve_steps, nb, pipelined):
    C = chunk
    R = GROUP_HEADS * C
    j = pl.program_id(1)
    last = pl.num_programs(1) - 1
    wslot = j % 2 if pipelined else 0
    rslot = 1 - wslot if pipelined else 0

    @pl.when(j == 0)
    def _():
        for b in range(nb):
            s0 = s0_ref[b]
            for g in range(N_GROUPS):
                state[b, g] = _pack_state(s0, g)
            zbuf[b, 7:8, :] = zpf_ref[b]
        if pipelined:
            ar_s[1] = jnp.zeros(ar_s.shape[1:], BF16)
            bk_s[1] = jnp.zeros(bk_s.shape[1:], BF16)
            vt_s[1] = jnp.zeros(vt_s.shape[1:], BF16)
            gl_s[1] = jnp.ones(gl_s.shape[1:], F32)
            bonus_s[1] = jnp.zeros(bonus_s.shape[1:], F32)
            gate_s[1] = jnp.zeros(gate_s.shape[1:], F32)

    @pl.when(j > 0)
    def _():
        for b in range(nb):
            zbuf[b, 7:8, :] = zbuf[b, C + 7:C + 8, :]

    w0, a0, k_k, k_a = vec_ref[0:1, :], vec_ref[1:2, :], vec_ref[2:3, :], vec_ref[3:4, :]
    r_k, gn_w, gn_b = vec_ref[4:5, :], vec_ref[5:6, :], vec_ref[6:7, :]
    ones_blk = ones_ref[...]
    ri = lax.broadcasted_iota(jnp.int32, (C, C), 0)
    ci = lax.broadcasted_iota(jnp.int32, (C, C), 1)
    tri = (ri >= ci).astype(BF16)
    lane = lax.broadcasted_iota(jnp.int32, (1, V7X_LANES), 1)
    blk = ((lane < HEAD_DIM).astype(BF16), (lane >= HEAD_DIM).astype(BF16))

    def prep_stages(b):
        p = {}

        def s1():
            zc = z_ref[b]
            if zc.shape[0] < C:
                zc = jnp.concatenate([zc, jnp.zeros((C - zc.shape[0], RW_IN_W), F32)], axis=0)
            zbuf[b, 8:8 + C, :] = zc
            zp = zbuf[b, 7:7 + C, :]
            zm = zc + (zp - zc) * mu_ref[...]
            p["r"] = zm[:, 0:RW_W]
            p["k"] = zm[:, RW_W:2 * RW_W]
            p["v"] = zm[:, 2 * RW_W:3 * RW_W]
            o1 = 3 * RW_W
            p["zw"] = zm[:, o1:o1 + DECAY_LORA]
            p["za"] = zm[:, o1 + DECAY_LORA:o1 + DECAY_LORA + ICLR_LORA]
            p["zg"] = zm[:, o1 + DECAY_LORA + ICLR_LORA:]

        def s2():
            y = -(w0 + _mm(jnp.tanh(p["zw"]), w2_ref[...]))
            softplus = jnp.maximum(y, 0.0) + jnp.log(1.0 + jnp.exp(-jnp.abs(y)))
            p["logdecay"] = -jnp.exp(-softplus - 0.5)
            p["a"] = jax.nn.sigmoid(a0 + _mm(p["za"], a2_ref[...]))
            gate_s[wslot, b] = _mm(jax.nn.sigmoid(p["zg"]), g2_ref[...])

        def s3():
            k, v, a = p["k"], p["v"], p["a"]
            kk_raw = k * k_k
            kmod = k * (1.0 + (a - 1.0) * k_a)
            vv = v
            if n_valid < C:
                rows = lax.broadcasted_iota(jnp.int32, (C, 1), 0) < n_valid
                p["logdecay"] = jnp.where(rows, p["logdecay"], 0.0)
                kk_raw = jnp.where(rows, kk_raw, 0.0)
                kmod = jnp.where(rows, kmod, 0.0)
                vv = jnp.where(rows, v, 0.0)
            vpad = vv
            if C < V7X_LANES:
                vpad = jnp.concatenate([vv, jnp.zeros((V7X_LANES - C, RW_W), F32)], axis=0)
            vt_s[wslot, b] = _heads_to_lanes(vpad.T, C).astype(BF16)
            sums = _head_sums(jnp.concatenate([kk_raw * kk_raw, p["r"] * kmod * r_k], axis=0), ones_blk)
            p["kk"] = kk_raw * lax.rsqrt(jnp.maximum(sums[:C], 1e-24))
            bonus_s[wslot, b] = sums[C:] * v
            p["kmod"] = kmod

        def s4():
            logdecay = p["logdecay"]
            hi = logdecay.astype(BF16)
            lo = (logdecay - hi.astype(F32)).astype(BF16)
            cum = jnp.dot(tri, jnp.concatenate([hi, lo], axis=1), preferred_element_type=F32)
            cum = cum[:, :RW_W] + cum[:, RW_W:]
            p["cum"] = cum
            gam = jnp.exp(cum)
            p["gam"] = gam
            gl_s[wslot, b] = jnp.broadcast_to(gam[C - 1:C, :], (8, RW_W))

        def s5():
            cum, gam, kk = p["cum"], p["gam"], p["kk"]
            gam_inv = jnp.exp(-cum)
            At = -kk * jnp.exp(cum - p["logdecay"])
            Rt = p["r"] * gam
            Bt = kk * p["a"] * gam_inv
            Kt = p["kmod"] * gam_inv
            ar_s[wslot, b] = jnp.concatenate([At, Rt], axis=0).astype(BF16)
            bk_s[wslot, b] = jnp.concatenate([Bt, Kt], axis=0).astype(BF16)

        return [s1, s2, s3, s4, s5]

    strict = strict_ref[...]
    incl = incl_ref[...]
    blk_p = blkp_ref[...]
    chains = [dict(b=b, grp=grp, sl=slice(grp * GROUP_W, (grp + 1) * GROUP_W),
                   slt=slice(grp * R, (grp + 1) * R))
              for b in range(nb) for grp in range(N_GROUPS)]
    on_parts = {b: [None] * N_GROUPS for b in range(nb)}

    def expand_t(x):
        if C == HEAD_DIM:
            return _expand(x, blk)
        return jnp.concatenate([x.astype(BF16)] * GROUP_HEADS, axis=0) * blk_p

    def c_load(chs):
        for ch in chs:
            b, sl = ch["b"], ch["sl"]
            ar = ar_s[rslot, b, :, sl]
            ch["BK"] = bk_s[rslot, b, :, sl]
            ch["VT"] = vt_s[rslot, b, :, ch["slt"]]
            ch["ARe"] = jnp.concatenate([_expand(ar[:C], blk), _expand(ar[C:], blk)], axis=0)
            ch["BKe"] = jnp.concatenate([_expand(ch["BK"][:C], blk), _expand(ch["BK"][C:], blk)], axis=0)

    def c_gram(chs):
        for ch in chs:
            ch["S"] = state[ch["b"], ch["grp"]]
            lhs = jnp.concatenate([ch["BK"], ch["S"].astype(BF16)], axis=0)
            GT = _mm_nt(lhs, ch["ARe"])
            ch["Q"] = (GT[:C, :R] * strict).astype(BF16)
            ch["LakT"] = (GT[C:2 * C, :R] * strict).astype(BF16)
            ch["MT"] = jnp.concatenate([expand_t(GT[:C, R:] * incl), expand_t(GT[C:2 * C, R:] * incl)], axis=0)
            ch["P0T"] = GT[2 * C:, :]

    def c_rhs(chs):
        for ch in chs:
            ch["XT"] = ch["P0T"][:, :R] + jnp.dot(ch["VT"], expand_t(ch["LakT"]), preferred_element_type=F32)

    def c_solve(i):
        def step(chs):
            for ch in chs:
                qe = expand_t(ch["Q"])
                xb = ch["XT"].astype(BF16)
                if i + 1 < solve_steps:
                    res = jnp.dot(jnp.concatenate([xb, ch["Q"]], axis=0), qe, preferred_element_type=F32)
                    ch["XT"] = ch["XT"] + res[:HEAD_DIM]
                    ch["Q"] = res[HEAD_DIM:].astype(BF16)
                else:
                    ch["XT"] = ch["XT"] + jnp.dot(xb, qe, preferred_element_type=F32)
        return step

    def c_out(chs):
        for ch in chs:
            b, grp, sl = ch["b"], ch["grp"], ch["sl"]
            uv = jnp.concatenate([ch["XT"].astype(BF16), ch["VT"]], axis=1)
            rhs = jnp.concatenate([ch["MT"], ch["BKe"]], axis=1)
            res = jnp.dot(uv, rhs, preferred_element_type=F32)
            state[b, grp] = (ch["S"] + res[:, R:]) * gl_s[rslot, b, 0:1, sl]
            OT = ch["P0T"][:, R:] + res[:, :R]
            mean = jnp.mean(OT, 0, keepdims=True)
            d = OT - mean
            var = jnp.mean(d * d, 0, keepdims=True)
            onT = d * lax.rsqrt(var + GN_EPS)
            rows = jnp.concatenate([onT[:, h * C:(h + 1) * C] for h in range(GROUP_HEADS)], axis=0)
            if C < V7X_LANES:
                rows = jnp.concatenate([rows, jnp.zeros((GROUP_W, V7X_LANES - C), F32)], axis=1)
            on_parts[b][grp] = rows.T[:C]

    def c_tail(chs):
        for b in sorted({ch["b"] for ch in chs}):
            on = jnp.concatenate(on_parts[b], axis=1)
            out = (on * gn_w + gn_b + bonus_s[rslot, b]) * gate_s[rslot, b]
            o_ref[b] = out[:o_ref.shape[1]]

    chain = [c_load, c_gram, c_rhs] + [c_solve(i) for i in range(solve_steps)] + [c_out, c_tail]
    prep = [s for b in range(nb) for s in prep_stages(b)]
    if pipelined:
        done = 0
        for n, stage in enumerate(chain):
            stage(chains)
            upto = (len(prep) * (n + 1)) // len(chain)
            for s in prep[done:upto]:
                s()
            done = upto
    else:
        for s in prep:
            s()
        for stage in chain:
            stage(chains)

    @pl.when(j == last)
    def _():
        for b in range(nb):
            for g in range(N_GROUPS):
                s = state[b, g]
                for h in range(GROUP_HEADS):
                    sout_ref[b, g * GROUP_HEADS + h] = s[:, h * HEAD_DIM:(h + 1) * HEAD_DIM]


def _rwkv_masks(chunk):
    R = GROUP_HEADS * chunk
    row_h = np.arange(R) // chunk
    lane_h = np.arange(GROUP_W) // HEAD_DIM
    blk_p = row_h[:, None] == row_h[None, :]
    ones_blk = lane_h[:, None] == lane_h[None, :]
    s, t = np.arange(chunk)[:, None], (np.arange(R) % chunk)[None, :]
    return (jnp.asarray(blk_p, BF16), jnp.asarray(t > s, F32), jnp.asarray(t >= s, F32),
            jnp.asarray(ones_blk, BF16))


def _rwkv(zr, zp_first, s0, mu, vecs, w2, a2, g2, *, layer, chunk, n_valid, nb):
    b, t, _ = zr.shape
    rows = min(t, chunk)
    nc = t // rows
    pipelined = nc > 1
    lag = 1 if pipelined else 0
    solve_steps = max(1, (n_valid - 1).bit_length())
    masks = _rwkv_masks(chunk)
    full = lambda a: pl.BlockSpec(a.shape, lambda i, j: (0,) * a.ndim)
    return pl.pallas_call(
        functools.partial(_rwkv_body, chunk=chunk, n_valid=n_valid, solve_steps=solve_steps, nb=nb,
                          pipelined=pipelined),
        grid=(b // nb, nc + lag),
        in_specs=[pl.BlockSpec((nb, rows, RW_IN_W), lambda i, j: (i, jnp.minimum(j, nc - 1), 0)),
                  pl.BlockSpec((nb, 1, RW_IN_W), lambda i, j: (i, 0, 0)),
                  pl.BlockSpec((None, nb, RW_HEADS, HEAD_DIM, HEAD_DIM), lambda i, j: (layer, i, 0, 0, 0)),
                  full(mu), full(vecs), full(w2), full(a2), full(g2)] + [full(m) for m in masks],
        out_specs=[pl.BlockSpec((nb, rows, RW_W), lambda i, j: (i, jnp.maximum(j - lag, 0), 0)),
                   pl.BlockSpec((nb, RW_HEADS, HEAD_DIM, HEAD_DIM), lambda i, j: (i, 0, 0, 0))],
        out_shape=[jax.ShapeDtypeStruct((b, t, RW_W), F32),
                   jax.ShapeDtypeStruct((b, RW_HEADS, HEAD_DIM, HEAD_DIM), F32)],
        scratch_shapes=[pltpu.VMEM((nb, N_GROUPS, HEAD_DIM, GROUP_W), F32),
                        pltpu.VMEM((nb, chunk + 8, RW_IN_W), F32),
                        pltpu.VMEM((2, nb, 2 * chunk, RW_W), BF16),
                        pltpu.VMEM((2, nb, 2 * chunk, RW_W), BF16),
                        pltpu.VMEM((2, nb, HEAD_DIM, RW_HEADS * chunk), BF16),
                        pltpu.VMEM((2, nb, 8, RW_W), F32),
                        pltpu.VMEM((2, nb, chunk, RW_W), F32),
                        pltpu.VMEM((2, nb, chunk, RW_W), F32)],
        compiler_params=_params(2),
        name="rwkv_chunk%d" % chunk,
    )(zr, zp_first, s0, mu, vecs, w2, a2, g2, *masks)


def _outffn_body(x_ref, att_ref, rw_ref, wo_ref, n2_ref, wg_ref, wu_ref, wd_ref, nf_ref, xo_ref, *, final):
    mix = jnp.concatenate([att_ref[...], rw_ref[...]], axis=-1).astype(BF16)
    x1 = x_ref[...] + jnp.dot(mix, wo_ref[...], preferred_element_type=F32)
    hb = _rmsnorm(x1, n2_ref[...]).astype(BF16)
    gate = jnp.dot(hb, wg_ref[...], preferred_element_type=F32)
    up = jnp.dot(hb, wu_ref[...], preferred_element_type=F32)
    act = (gate * jax.nn.sigmoid(gate) * up).astype(BF16)
    x2 = x1 + jnp.dot(act, wd_ref[...], preferred_element_type=F32)
    xo_ref[...] = _rmsnorm(x2, nf_ref[...]) if final else x2


def _outffn(x, att, rw, wo, n2, wg, wu, wd, nf, *, final, tm):
    n = x.shape[0]
    resident = lambda a: pl.BlockSpec(a.shape, lambda i: (0,) * a.ndim, pipeline_mode=pl.Buffered(1))
    return pl.pallas_call(
        functools.partial(_outffn_body, final=final),
        grid=(n // tm,),
        in_specs=[pl.BlockSpec((tm, D_MODEL), lambda i: (i, 0)),
                  pl.BlockSpec((tm, ATT_W), lambda i: (i, 0)),
                  pl.BlockSpec((tm, RW_W), lambda i: (i, 0)),
                  resident(wo), resident(n2), resident(wg), resident(wu), resident(wd), resident(nf)],
        out_specs=pl.BlockSpec((tm, D_MODEL), lambda i: (i, 0)),
        out_shape=jax.ShapeDtypeStruct((n, D_MODEL), F32),
        compiler_params=_params(1),
        name="outffn_final" if final else "outffn",
    )(x, att, rw, wo, n2, wg, wu, wd, nf)


def _t5_bucket_np(d):
    max_exact = NUM_BUCKETS // 2
    n = np.maximum(d, 0)
    large = max_exact + (np.log(np.maximum(n, 1) / max_exact) / np.log(MAX_DISTANCE / max_exact)
                         * (NUM_BUCKETS - max_exact)).astype(np.int32)
    large = np.minimum(large, NUM_BUCKETS - 1)
    return np.where(n < max_exact, n, large).astype(np.int32)


def _bias_body(tab_ref, bucket_ref, inside_ref, o_ref, *, keys_major):
    bucket = bucket_ref[...]
    inside = inside_ref[...] > 0
    for h in range(N_Q_HEADS):
        acc = jnp.zeros(bucket.shape, F32)
        for bk in range(NUM_BUCKETS):
            acc = jnp.where(bucket == bk, tab_ref[bk, h], acc)
        acc = jnp.where(inside, acc, NEG)
        if keys_major:
            kv, g, q = h // GQA, h % GQA, bucket.shape[1]
            o_ref[kv, :, g * q:(g + 1) * q] = acc
        else:
            o_ref[h] = acc


def _masked_bias(rel_table, d, inside, rows_per_head, keys_major=False):
    q, s = d.shape
    bucket = np.zeros((rows_per_head, s), np.int32)
    bucket[:q] = _t5_bucket_np(d)
    ins = np.zeros((rows_per_head, s), np.int32)
    ins[:q] = inside
    if keys_major:
        bucket, ins = bucket.T.copy(), ins.T.copy()
        out_shape = (N_KV_HEADS, s, GQA * rows_per_head)
    else:
        out_shape = (N_Q_HEADS, rows_per_head, s)
    vm = lambda shape: pl.BlockSpec(shape, lambda: (0,) * len(shape))
    b = pl.pallas_call(
        functools.partial(_bias_body, keys_major=keys_major),
        in_specs=[pl.BlockSpec(memory_space=pltpu.SMEM), vm(bucket.shape), vm(ins.shape)],
        out_specs=vm(out_shape),
        out_shape=jax.ShapeDtypeStruct(out_shape, F32),
        name="rel_bias",
    )(rel_table.astype(F32), jnp.asarray(bucket), jnp.asarray(ins))
    return b if keys_major else b.reshape(N_KV_HEADS, GQA * rows_per_head, s)


def _sink_rows(sink_l, rows_per_head):
    return jnp.repeat(sink_l.astype(F32).reshape(N_KV_HEADS, GQA, 1), rows_per_head, axis=1).reshape(
        N_KV_HEADS, GQA * rows_per_head, 1)


def kernel(x_prompt, x_sample, cache_k, cache_v, state_shift, state_wkv, norm1, w_in, w_out, sink,
           rel_table, mu, w0, w2, a0, a2, g2, k_k, k_a, r_k, gn_w, gn_b, norm2, w_gate, w_up,
           w_down, norm_f):
    depth = w_in.shape[0]
    B, T, _ = x_prompt.shape
    DB, S, _ = x_sample.shape
    wc = cache_k.shape[2]
    assert wc == WINDOW and T >= WINDOW and S <= SAMPLE_PAD
    assert T % INPROJ_ROWS == 0 and (B * T) % OUTFFN_ROWS == 0 and T % (ATTN_BLOCKS * BLOCK) == 0
    assert T % PROMPT_CHUNK == 0 and B % RWKV_PROMPT_SEQS == 0
    assert DB % ATTN_SAMPLE_SEQS == 0 and DB % RWKV_SAMPLE_SEQS == 0

    qi = np.arange(BLOCK)[:, None]
    sj = np.arange(2 * BLOCK)[None, :]
    dp = qi + BLOCK - sj
    inside_p = (dp >= 0) & (dp <= WINDOW)
    bias_p = jnp.stack([_masked_bias(rel_table, dp, inside_p & (sj >= BLOCK), BLOCK, keys_major=True),
                        _masked_bias(rel_table, dp, inside_p, BLOCK, keys_major=True)])
    ds_full = (wc + np.arange(S))[:, None] - np.arange(wc + S)[None, :]
    inside_s = (ds_full >= 0) & (ds_full <= WINDOW)
    bias_c = _masked_bias(rel_table, ds_full[:, :wc], inside_s[:, :wc], SAMPLE_PAD)
    dn = np.full((S, SAMPLE_PAD), -1, np.int64)
    dn[:, :S] = ds_full[:, wc:]
    inn = np.zeros((S, SAMPLE_PAD), bool)
    inn[:, :S] = inside_s[:, wc:]
    bias_n = _masked_bias(rel_table, dn, inn, SAMPLE_PAD)

    xp = x_prompt.reshape(B * T, D_MODEL)
    xs = x_sample.reshape(DB * S, D_MODEL)
    zeros_shift = jnp.zeros((B, 1, RW_IN_W), F32)
    zeros_state = jnp.zeros((1, B, RW_HEADS, HEAD_DIM, HEAD_DIM), F32)
    cache_k2 = cache_k.reshape(depth, DB, wc, KV_W)
    cache_v2 = cache_v.reshape(depth, DB, wc, KV_W)
    nf = norm_f.reshape(1, D_MODEL)
    ones = jnp.ones((1, D_MODEL), F32)

    kp, vp, sp, wp, ksl, vsl, ssl, wsl = [], [], [], [], [], [], [], []
    for l in range(depth):
        w_in_b = _to_bf16(w_in, l)
        wo_b, wg_b, wu_b, wd_b = _to_bf16(w_out, l), _to_bf16(w_gate, l), _to_bf16(w_up, l), _to_bf16(w_down, l)
        n1 = norm1[l].reshape(1, D_MODEL)
        n2 = norm2[l].reshape(1, D_MODEL)
        mu_l = mu[l].reshape(1, RW_IN_W)
        vecs = jnp.stack([w0[l], a0[l], k_k[l], k_a[l], r_k[l], gn_w[l], gn_b[l], jnp.zeros_like(w0[l])])
        w2_b, a2_b, g2_b = w2[l].astype(BF16), a2[l].astype(BF16), g2[l].astype(BF16)
        final = l == depth - 1

        za, zr, h_last, kv_last = _inproj(xp, n1, w_in_b, norm=True, tm=INPROJ_ROWS, seq_len=T)
        za3 = za.reshape(B, T, ATT_IN_W)
        kp.append(kv_last[:, :, :KV_W].reshape(B, WINDOW, N_KV_HEADS, HEAD_DIM))
        vp.append(kv_last[:, :, KV_W:].reshape(B, WINDOW, N_KV_HEADS, HEAD_DIM))
        sp.append(h_last.reshape(B, D_MODEL))
        sink_lanes = jnp.swapaxes(_sink_rows(sink[l], BLOCK), 1, 2)
        att = _attn_prompt(za3, bias_p, sink_lanes, nblk=ATTN_BLOCKS)
        rw, wkv = _rwkv(zr.reshape(B, T, RW_IN_W), zeros_shift, zeros_state, mu_l, vecs, w2_b, a2_b, g2_b,
                        layer=0, chunk=PROMPT_CHUNK, n_valid=PROMPT_CHUNK, nb=RWKV_PROMPT_SEQS)
        wp.append(wkv)
        xp = _outffn(xp, att.reshape(B * T, ATT_W), rw.reshape(B * T, RW_W), wo_b, n2, wg_b, wu_b, wd_b, nf,
                     final=final, tm=OUTFFN_ROWS)

        za_s, zr_s, h_s = _inproj(xs, n1, w_in_b, norm=True, tm=DB * S)
        _, zr_prev, _ = _inproj(state_shift[l], ones, w_in_b, norm=False, tm=DB)
        ssl.append(h_s.reshape(DB, S, D_MODEL)[:, -1])
        att_s, kb, vb = _attn_sample(za_s.reshape(DB, S, ATT_IN_W), cache_k2, cache_v2,
                                     bias_c, bias_n, _sink_rows(sink[l], SAMPLE_PAD), layer=l, ns=ATTN_SAMPLE_SEQS)
        ksl.append(kb.reshape(DB, wc, N_KV_HEADS, HEAD_DIM))
        vsl.append(vb.reshape(DB, wc, N_KV_HEADS, HEAD_DIM))
        rw_s, wkv_s = _rwkv(zr_s.reshape(DB, S, RW_IN_W), zr_prev.reshape(DB, 1, RW_IN_W), state_wkv,
                            mu_l, vecs, w2_b, a2_b, g2_b, layer=l, chunk=SAMPLE_CHUNK, n_valid=S,
                            nb=RWKV_SAMPLE_SEQS)
        wsl.append(wkv_s)
        xs = _outffn(xs, att_s.reshape(DB * S, ATT_W), rw_s.reshape(DB * S, RW_W),
                     wo_b, n2, wg_b, wu_b, wd_b, nf, final=final, tm=DB * S)

    y_prompt = xp.reshape(B, T, D_MODEL)
    y_sample = xs.reshape(DB, S, D_MODEL)
    return (y_prompt, y_sample, jnp.stack(kp), jnp.stack(vp), jnp.stack(sp), jnp.stack(wp),
            jnp.stack(ksl), jnp.stack(vsl), jnp.stack(ssl), jnp.stack(wsl))
```

```python
import functools

import numpy as np
import jax
import jax.numpy as jnp
from jax import lax
from jax.experimental import pallas as pl
from jax.experimental.pallas import tpu as pltpu

D_MODEL = 1024
HEAD_DIM = 64
N_Q_HEADS = 8
N_KV_HEADS = 2
GQA = N_Q_HEADS // N_KV_HEADS
ATT_W = N_Q_HEADS * HEAD_DIM
KV_W = N_KV_HEADS * HEAD_DIM
WINDOW = 128
BLOCK = 128
NUM_BUCKETS = 32
MAX_DISTANCE = 128
RW_HEADS = 8
RW_W = RW_HEADS * HEAD_DIM
DECAY_LORA = 64
ICLR_LORA = 64
GATE_LORA = 128
RW_IN_W = 3 * RW_W + DECAY_LORA + ICLR_LORA + GATE_LORA
ATT_IN_W = ATT_W + 2 * KV_W
NORM_EPS = 1e-6
GN_EPS = 64e-5
NEG = -1e30
SCALE = HEAD_DIM ** -0.5

V7X_VMEM_LIMIT_BYTES = 56 * 1024 * 1024

V7X_LANES = 128
V7X_SUBLANES = 8

PROMPT_CHUNK = 64
SAMPLE_CHUNK = 16
SAMPLE_PAD = V7X_SUBLANES
INPROJ_ROWS = 1024
OUTFFN_ROWS = 1024
FFN_CHUNK = 256
ATTN_BLOCKS = 4
ATTN_SAMPLE_SEQS = 8
RWKV_PROMPT_SEQS = 8
RWKV_SAMPLE_SEQS = 4
CAST_ROWS = 256

BF16 = jnp.bfloat16
F32 = jnp.float32


def _mm(a, b):
    return jnp.dot(a.astype(BF16), b.astype(BF16), preferred_element_type=F32)


def _mm_nt(a, b):
    return lax.dot_general(a.astype(BF16), b.astype(BF16), (((1,), (1,)), ((), ())),
                           preferred_element_type=F32)


def _rmsnorm(x, g):
    return x * lax.rsqrt(jnp.mean(x * x, -1, keepdims=True) + NORM_EPS) * g


def _params(n_axes):
    return pltpu.CompilerParams(dimension_semantics=("arbitrary",) * n_axes,
                                vmem_limit_bytes=V7X_VMEM_LIMIT_BYTES)


def _cast_body(w_ref, o_ref):
    o_ref[...] = w_ref[...].astype(BF16)


def _to_bf16(w, layer):
    _, rows, cols = w.shape
    tr = CAST_ROWS if rows % CAST_ROWS == 0 else rows
    return pl.pallas_call(
        _cast_body,
        grid=(rows // tr,),
        in_specs=[pl.BlockSpec((None, tr, cols), lambda i: (layer, i, 0))],
        out_specs=pl.BlockSpec((tr, cols), lambda i: (i, 0)),
        out_shape=jax.ShapeDtypeStruct((rows, cols), BF16),
        compiler_params=_params(1),
        name="cast_bf16",
    )(w)


def _inproj_body(x_ref, g_ref, w_ref, za_ref, zr_ref, h_ref, *tail_refs, norm, seq_len):
    x = x_ref[...]
    h = _rmsnorm(x, g_ref[...]) if norm else x
    z = jnp.dot(h.astype(BF16), w_ref[...], preferred_element_type=F32)
    za_ref[...] = z[:, :ATT_IN_W]
    zr_ref[...] = z[:, ATT_IN_W:]
    if seq_len is None:
        h_ref[...] = h
    else:
        (kv_ref,) = tail_refs
        tm = x.shape[0]
        h_ref[0] = h[tm - 1:, :]
        kv_ref[0] = z[tm - WINDOW:, ATT_W:ATT_IN_W]


def _inproj(x, g, w, *, norm, tm, seq_len=None):
    n = x.shape[0]
    in_w = w.shape[1]
    if seq_len is None:
        tail_specs = [pl.BlockSpec((tm, D_MODEL), lambda i: (i, 0))]
        tail_shapes = [jax.ShapeDtypeStruct((n, D_MODEL), F32)]
    else:
        tiles = seq_len // tm
        tail_specs = [pl.BlockSpec((1, 1, D_MODEL), lambda i: (i // tiles, 0, 0)),
                      pl.BlockSpec((1, WINDOW, 2 * KV_W), lambda i: (i // tiles, 0, 0))]
        tail_shapes = [jax.ShapeDtypeStruct((n // seq_len, 1, D_MODEL), F32),
                       jax.ShapeDtypeStruct((n // seq_len, WINDOW, 2 * KV_W), F32)]
    return pl.pallas_call(
        functools.partial(_inproj_body, norm=norm, seq_len=seq_len),
        grid=(n // tm,),
        in_specs=[pl.BlockSpec((tm, D_MODEL), lambda i: (i, 0)),
                  pl.BlockSpec((1, D_MODEL), lambda i: (0, 0)),
                  pl.BlockSpec((D_MODEL, in_w), lambda i: (0, 0))],
        out_specs=[pl.BlockSpec((tm, ATT_IN_W), lambda i: (i, 0)),
                   pl.BlockSpec((tm, RW_IN_W), lambda i: (i, 0))] + tail_specs,
        out_shape=[jax.ShapeDtypeStruct((n, ATT_IN_W), F32),
                   jax.ShapeDtypeStruct((n, RW_IN_W), F32)] + tail_shapes,
        compiler_params=_params(1),
        name="inproj_norm" if norm else "inproj_plain",
    )(x, g, w)


def _sink_softmax(score_parts, sink_col):
    m = sink_col
    for s in score_parts:
        m = jnp.maximum(m, jnp.max(s, -1, keepdims=True))
    es = [jnp.exp(s - m) for s in score_parts]
    den = jnp.exp(sink_col - m)
    for e in es:
        den = den + jnp.sum(e, -1, keepdims=True)
    return [e / den for e in es]


QG_W = GQA * HEAD_DIM


def _tile_kv_heads(x):
    rot = pltpu.roll(x, HEAD_DIM, 1)
    first = lax.broadcasted_iota(jnp.int32, x.shape, 1) < HEAD_DIM
    halves = (jnp.where(first, x, rot).astype(BF16), jnp.where(first, rot, x).astype(BF16))
    return [jnp.concatenate([h] * (QG_W // KV_W), axis=1) for h in halves]


def _attn_prompt_body(q_ref, kv_ref, bias_ref, sink_ref, o_ref, kt_prev, vT_prev, *, nblk):
    j = pl.program_id(1)

    @pl.when(j == 0)
    def _():
        kt_prev[...] = jnp.zeros_like(kt_prev)
        vT_prev[...] = jnp.zeros_like(vT_prev)

    lane = lax.broadcasted_iota(jnp.int32, (1, V7X_LANES), 1)
    lane_half = ((lane < HEAD_DIM).astype(BF16), (lane >= HEAD_DIM).astype(BF16))
    kts = [[kt_prev[kv] for kv in range(N_KV_HEADS)]]
    vTs = [vT_prev[...]]
    for i in range(nblk):
        rows = slice(i * BLOCK, (i + 1) * BLOCK)
        kts.append(_tile_kv_heads(kv_ref[0, rows, 0:KV_W]))
        vTs.append(kv_ref[0, rows, KV_W:2 * KV_W].T.astype(BF16))
    for kv in range(N_KV_HEADS):
        kt_prev[kv] = kts[nblk][kv]
    vT_prev[...] = vTs[nblk]

    units = []
    for i in range(nblk):
        qb = (q_ref[0, i * BLOCK:(i + 1) * BLOCK, :] * SCALE).astype(BF16)
        first = jnp.where(j == 0, 0, 1) if i == 0 else 1
        for kv in range(N_KV_HEADS):
            qe = _expand(qb[:, kv * QG_W:(kv + 1) * QG_W], lane_half)
            kcat = jnp.concatenate([kts[i][kv], kts[i + 1][kv]], axis=0)
            units.append(dict(i=i, kv=kv, s=_mm_nt(kcat, qe), first=first))
    for u in units:
        b = bias_ref[u["first"], u["kv"]]
        s = jnp.where(b > 0.5 * NEG, u["s"] + b, NEG)
        sink_row = sink_ref[u["kv"]]
        m = jnp.maximum(jnp.max(s, 0, keepdims=True), sink_row)
        e = jnp.exp(s - m)
        den = jnp.sum(e, 0, keepdims=True) + jnp.exp(sink_row - m)
        u["p"] = (e * (1.0 / den)).astype(BF16)
    pieces = [[None] * N_Q_HEADS for _ in range(nblk)]
    for u in units:
        i, kv = u["i"], u["kv"]
        hd = slice(kv * HEAD_DIM, (kv + 1) * HEAD_DIM)
        vcatT = jnp.concatenate([vTs[i][hd], vTs[i + 1][hd]], axis=1)
        oT = jnp.dot(vcatT, u["p"], preferred_element_type=F32)
        for g in range(GQA):
            pieces[i][kv * GQA + g] = oT[:, g * BLOCK:(g + 1) * BLOCK]
    for i in range(nblk):
        o_ref[0, i * BLOCK:(i + 1) * BLOCK, :] = jnp.concatenate(pieces[i], axis=0).T


def _attn_prompt(za, bias, sink_rows, *, nblk):
    b, t, _ = za.shape
    rows = nblk * BLOCK
    full = lambda a: pl.BlockSpec(a.shape, lambda i, j: (0,) * a.ndim)
    return pl.pallas_call(
        functools.partial(_attn_prompt_body, nblk=nblk),
        grid=(b, t // rows),
        in_specs=[pl.BlockSpec((1, rows, ATT_W), lambda i, j: (i, j, 0)),
                  pl.BlockSpec((1, rows, 2 * KV_W), lambda i, j: (i, j, ATT_W // (2 * KV_W))),
                  full(bias), full(sink_rows)],
        out_specs=pl.BlockSpec((1, rows, ATT_W), lambda i, j: (i, j, 0)),
        out_shape=jax.ShapeDtypeStruct((b, t, ATT_W), F32),
        scratch_shapes=[pltpu.VMEM((N_KV_HEADS, BLOCK, QG_W), BF16),
                        pltpu.VMEM((KV_W, BLOCK), BF16)],
        compiler_params=_params(2),
        name="attn_prompt",
    )(za, za, bias, sink_rows)


def _attn_sample_body(z_ref, ck_ref, cv_ref, bias_c_ref, bias_n_ref, sink_ref,
                      o_ref, kb_ref, vb_ref, *, n_new, ns):
    units = []
    for s in range(ns):
        z = jnp.concatenate([z_ref[s], jnp.zeros((SAMPLE_PAD - n_new, ATT_IN_W), F32)], axis=0)
        q = z[:, :ATT_W]
        kn = z[:, ATT_W:ATT_W + KV_W]
        vn = z[:, ATT_W + KV_W:]
        ck, cv = ck_ref[s], cv_ref[s]
        keep = WINDOW - n_new
        kb_ref[s, 0:keep, :] = ck_ref[s, n_new:WINDOW, :]
        kb_ref[s, keep:WINDOW, :] = kn[0:n_new]
        vb_ref[s, 0:keep, :] = cv_ref[s, n_new:WINDOW, :]
        vb_ref[s, keep:WINDOW, :] = vn[0:n_new]
        for kv in range(N_KV_HEADS):
            sl = slice(kv * HEAD_DIM, (kv + 1) * HEAD_DIM)
            qg = jnp.concatenate(
                [q[:, (kv * GQA + g) * HEAD_DIM:(kv * GQA + g + 1) * HEAD_DIM] for g in range(GQA)], axis=0)
            units.append(dict(s=s, kv=kv, qg=qg, ck=ck[:, sl], kn=kn[:, sl], cv=cv[:, sl], vn=vn[:, sl]))
    for u in units:
        u["sc"] = _mm_nt(u["qg"], u["ck"]) * SCALE
        u["sn"] = _mm_nt(u["qg"], u["kn"]) * SCALE
    for u in units:
        bc, bn = bias_c_ref[u["kv"]], bias_n_ref[u["kv"]]
        sc = jnp.where(bc > 0.5 * NEG, u["sc"] + bc, NEG)
        sn = jnp.where(bn > 0.5 * NEG, u["sn"] + bn, NEG)
        u["pc"], u["pn"] = _sink_softmax([sc, sn], sink_ref[u["kv"]])
    outs = [[] for _ in range(ns)]
    for u in units:
        o = _mm(u["pc"], u["cv"]) + _mm(u["pn"], u["vn"])
        outs[u["s"]] += [o[g * SAMPLE_PAD:(g + 1) * SAMPLE_PAD] for g in range(GQA)]
    for s in range(ns):
        o_ref[s] = jnp.concatenate(outs[s], axis=1)[:n_new]


def _attn_sample(za, ck, cv, bias_c, bias_n, sink_rows, *, layer, ns):
    db, n_new, _ = za.shape
    full = lambda a: pl.BlockSpec(a.shape, lambda i: (0,) * a.ndim)
    return pl.pallas_call(
        functools.partial(_attn_sample_body, n_new=n_new, ns=ns),
        grid=(db // ns,),
        in_specs=[pl.BlockSpec((ns, n_new, ATT_IN_W), lambda i: (i, 0, 0)),
                  pl.BlockSpec((None, ns, WINDOW, KV_W), lambda i: (layer, i, 0, 0)),
                  pl.BlockSpec((None, ns, WINDOW, KV_W), lambda i: (layer, i, 0, 0)),
                  full(bias_c), full(bias_n), full(sink_rows)],
        out_specs=[pl.BlockSpec((ns, n_new, ATT_W), lambda i: (i, 0, 0)),
                   pl.BlockSpec((ns, WINDOW, KV_W), lambda i: (i, 0, 0)),
                   pl.BlockSpec((ns, WINDOW, KV_W), lambda i: (i, 0, 0))],
        out_shape=[jax.ShapeDtypeStruct((db, n_new, ATT_W), F32),
                   jax.ShapeDtypeStruct((db, WINDOW, KV_W), F32),
                   jax.ShapeDtypeStruct((db, WINDOW, KV_W), F32)],
        compiler_params=_params(1),
        name="attn_sample",
    )(za, ck, cv, bias_c, bias_n, sink_rows)


GROUP_HEADS = 4
GROUP_W = GROUP_HEADS * HEAD_DIM
N_GROUPS = RW_HEADS // GROUP_HEADS


def _head_sums(x, ones_blk):
    m = x.shape[0]
    hi = x.astype(BF16)
    lo = (x - hi.astype(F32)).astype(BF16)
    hl = jnp.concatenate([hi, lo], axis=0)
    y = jnp.concatenate(
        [jnp.dot(hl[:, g * GROUP_W:(g + 1) * GROUP_W], ones_blk, preferred_element_type=F32)
         for g in range(N_GROUPS)], axis=1)
    return y[:m] + y[m:]


def _expand(x, lane_half):
    xb = x.astype(BF16)
    c = xb.shape[0]
    zero = jnp.zeros((c, V7X_LANES), BF16)
    blocks = []
    for h in range(GROUP_HEADS):
        col = h // 2
        piece = xb[:, col * V7X_LANES:(col + 1) * V7X_LANES] * lane_half[h % 2]
        blocks.append(jnp.concatenate([piece, zero] if col == 0 else [zero, piece], axis=1))
    return jnp.concatenate(blocks, axis=0)


def _pack_state(s, g):
    return jnp.concatenate([s[g * GROUP_HEADS + h] for h in range(GROUP_HEADS)], axis=1)


def _heads_to_lanes(x, c):
    heads = x.shape[0] // HEAD_DIM
    return jnp.concatenate([x[h * HEAD_DIM:(h + 1) * HEAD_DIM, :c] for h in range(heads)], axis=1)


def _rwkv_body(z_ref, zpf_ref, s0_ref, mu_ref, vec_ref, w2_ref, a2_ref, g2_ref,
               blkp_ref, strict_ref, incl_ref, ones_ref,
               o_ref, sout_ref, state, zbuf, ar_s, bk_s, vt_s, gl_s, bonus_s, gate_s,
               *, chunk, n_valid, solve_steps, nb, pipelined):
    C = chunk
    R = GROUP_HEADS * C
    j = pl.program_id(1)
    last = pl.num_programs(1) - 1
    wslot = j % 2 if pipelined else 0
    rslot = 1 - wslot if pipelined else 0

    @pl.when(j == 0)
    def _():
        for b in range(nb):
            s0 = s0_ref[b]
            for g in range(N_GROUPS):
                state[b, g] = _pack_state(s0, g)
            zbuf[b, 7:8, :] = zpf_ref[b]
        if pipelined:
            ar_s[1] = jnp.zeros(ar_s.shape[1:], BF16)
            bk_s[1] = jnp.zeros(bk_s.shape[1:], BF16)
            vt_s[1] = jnp.zeros(vt_s.shape[1:], BF16)
            gl_s[1] = jnp.ones(gl_s.shape[1:], F32)
            bonus_s[1] = jnp.zeros(bonus_s.shape[1:], F32)
            gate_s[1] = jnp.zeros(gate_s.shape[1:], F32)

    @pl.when(j > 0)
    def _():
        for b in range(nb):
            zbuf[b, 7:8, :] = zbuf[b, C + 7:C + 8, :]

    w0, a0, k_k, k_a = vec_ref[0:1, :], vec_ref[1:2, :], vec_ref[2:3, :], vec_ref[3:4, :]
    r_k, gn_w, gn_b = vec_ref[4:5, :], vec_ref[5:6, :], vec_ref[6:7, :]
    ones_blk = ones_ref[...]
    ri = lax.broadcasted_iota(jnp.int32, (C, C), 0)
    ci = lax.broadcasted_iota(jnp.int32, (C, C), 1)
    tri = (ri >= ci).astype(BF16)
    lane = lax.broadcasted_iota(jnp.int32, (1, V7X_LANES), 1)
    blk = ((lane < HEAD_DIM).astype(BF16), (lane >= HEAD_DIM).astype(BF16))

    def prep_stages(b):
        p = {}

        def s1():
            zc = z_ref[b]
            if zc.shape[0] < C:
                zc = jnp.concatenate([zc, jnp.zeros((C - zc.shape[0], RW_IN_W), F32)], axis=0)
            zbuf[b, 8:8 + C, :] = zc
            zp = zbuf[b, 7:7 + C, :]
            zm = zc + (zp - zc) * mu_ref[...]
            p["r"] = zm[:, 0:RW_W]
            p["k"] = zm[:, RW_W:2 * RW_W]
            p["v"] = zm[:, 2 * RW_W:3 * RW_W]
            o1 = 3 * RW_W
            p["zw"] = zm[:, o1:o1 + DECAY_LORA]
            p["za"] = zm[:, o1 + DECAY_LORA:o1 + DECAY_LORA + ICLR_LORA]
            p["zg"] = zm[:, o1 + DECAY_LORA + ICLR_LORA:]

        def s2():
            y = -(w0 + _mm(jnp.tanh(p["zw"]), w2_ref[...]))
            softplus = jnp.maximum(y, 0.0) + jnp.log(1.0 + jnp.exp(-jnp.abs(y)))
            p["logdecay"] = -jnp.exp(-softplus - 0.5)
            p["a"] = jax.nn.sigmoid(a0 + _mm(p["za"], a2_ref[...]))
            gate_s[wslot, b] = _mm(jax.nn.sigmoid(p["zg"]), g2_ref[...])

        def s3():
            k, v, a = p["k"], p["v"], p["a"]
            kk_raw = k * k_k
            kmod = k * (1.0 + (a - 1.0) * k_a)
            vv = v
            if n_valid < C:
                rows = lax.broadcasted_iota(jnp.int32, (C, 1), 0) < n_valid
                p["logdecay"] = jnp.where(rows, p["logdecay"], 0.0)
                kk_raw = jnp.where(rows, kk_raw, 0.0)
                kmod = jnp.where(rows, kmod, 0.0)
                vv = jnp.where(rows, v, 0.0)
            vpad = vv
            if C < V7X_LANES:
                vpad = jnp.concatenate([vv, jnp.zeros((V7X_LANES - C, RW_W), F32)], axis=0)
            vt_s[wslot, b] = _heads_to_lanes(vpad.T, C).astype(BF16)
            sums = _head_sums(jnp.concatenate([kk_raw * kk_raw, p["r"] * kmod * r_k], axis=0), ones_blk)
            p["kk"] = kk_raw * lax.rsqrt(jnp.maximum(sums[:C], 1e-24))
            bonus_s[wslot, b] = sums[C:] * v
            p["kmod"] = kmod

        def s4():
            logdecay = p["logdecay"]
            hi = logdecay.astype(BF16)
            lo = (logdecay - hi.astype(F32)).astype(BF16)
            cum = jnp.dot(tri, jnp.concatenate([hi, lo], axis=1), preferred_element_type=F32)
            cum = cum[:, :RW_W] + cum[:, RW_W:]
            p["cum"] = cum
            gam = jnp.exp(cum)
            p["gam"] = gam
            gl_s[wslot, b] = jnp.broadcast_to(gam[C - 1:C, :], (8, RW_W))

        def s5():
            cum, gam, kk = p["cum"], p["gam"], p["kk"]
            gam_inv = jnp.exp(-cum)
            At = -kk * jnp.exp(cum - p["logdecay"])
            Rt = p["r"] * gam
            Bt = kk * p["a"] * gam_inv
            Kt = p["kmod"] * gam_inv
            ar_s[wslot, b] = jnp.concatenate([At, Rt], axis=0).astype(BF16)
            bk_s[wslot, b] = jnp.concatenate([Bt, Kt], axis=0).astype(BF16)

        return [s1, s2, s3, s4, s5]

    strict = strict_ref[...]
    incl = incl_ref[...]
    blk_p = blkp_ref[...]
    chains = [dict(b=b, grp=grp, sl=slice(grp * GROUP_W, (grp + 1) * GROUP_W),
                   slt=slice(grp * R, (grp + 1) * R))
              for b in range(nb) for grp in range(N_GROUPS)]
    on_parts = {b: [None] * N_GROUPS for b in range(nb)}

    def expand_t(x):
        if C == HEAD_DIM:
            return _expand(x, blk)
        return jnp.concatenate([x.astype(BF16)] * GROUP_HEADS, axis=0) * blk_p

    def c_load(chs):
        for ch in chs:
            b, sl = ch["b"], ch["sl"]
            ar = ar_s[rslot, b, :, sl]
            ch["BK"] = bk_s[rslot, b, :, sl]
            ch["VT"] = vt_s[rslot, b, :, ch["slt"]]
            ch["ARe"] = jnp.concatenate([_expand(ar[:C], blk), _expand(ar[C:], blk)], axis=0)
            ch["BKe"] = jnp.concatenate([_expand(ch["BK"][:C], blk), _expand(ch["BK"][C:], blk)], axis=0)

    def c_gram(chs):
        for ch in chs:
            ch["S"] = state[ch["b"], ch["grp"]]
            lhs = jnp.concatenate([ch["BK"], ch["S"].astype(BF16)], axis=0)
            GT = _mm_nt(lhs, ch["ARe"])
            ch["Q"] = (GT[:C, :R] * strict).astype(BF16)
            ch["LakT"] = (GT[C:2 * C, :R] * strict).astype(BF16)
            ch["MT"] = jnp.concatenate([expand_t(GT[:C, R:] * incl), expand_t(GT[C:2 * C, R:] * incl)], axis=0)
            ch["P0T"] = GT[2 * C:, :]

    def c_rhs(chs):
        for ch in chs:
            ch["XT"] = ch["P0T"][:, :R] + jnp.dot(ch["VT"], expand_t(ch["LakT"]), preferred_element_type=F32)

    def c_solve(i):
        def step(chs):
            for ch in chs:
                qe = expand_t(ch["Q"])
                xb = ch["XT"].astype(BF16)
                if i + 1 < solve_steps:
                    res = jnp.dot(jnp.concatenate([xb, ch["Q"]], axis=0), qe, preferred_element_type=F32)
                    ch["XT"] = ch["XT"] + res[:HEAD_DIM]
                    ch["Q"] = res[HEAD_DIM:].astype(BF16)
                else:
                    ch["XT"] = ch["XT"] + jnp.dot(xb, qe, preferred_element_type=F32)
        return step

    def c_out(chs):
        for ch in chs:
            b, grp, sl = ch["b"], ch["grp"], ch["sl"]
            uv = jnp.concatenate([ch["XT"].astype(BF16), ch["VT"]], axis=1)
            rhs = jnp.concatenate([ch["MT"], ch["BKe"]], axis=1)
            res = jnp.dot(uv, rhs, preferred_element_type=F32)
            state[b, grp] = (ch["S"] + res[:, R:]) * gl_s[rslot, b, 0:1, sl]
            OT = ch["P0T"][:, R:] + res[:, :R]
            mean = jnp.mean(OT, 0, keepdims=True)
            d = OT - mean
            var = jnp.mean(d * d, 0, keepdims=True)
            onT = d * lax.rsqrt(var + GN_EPS)
            rows = jnp.concatenate([onT[:, h * C:(h + 1) * C] for h in range(GROUP_HEADS)], axis=0)
            if C < V7X_LANES:
                rows = jnp.concatenate([rows, jnp.zeros((GROUP_W, V7X_LANES - C), F32)], axis=1)
            on_parts[b][grp] = rows.T[:C]

    def c_tail(chs):
        for b in sorted({ch["b"] for ch in chs}):
            on = jnp.concatenate(on_parts[b], axis=1)
            out = (on * gn_w + gn_b + bonus_s[rslot, b]) * gate_s[rslot, b]
            o_ref[b] = out[:o_ref.shape[1]]

    chain = [c_load, c_gram, c_rhs] + [c_solve(i) for i in range(solve_steps)] + [c_out, c_tail]
    prep = [s for b in range(nb) for s in prep_stages(b)]
    if pipelined:
        done = 0
        for n, stage in enumerate(chain):
            stage(chains)
            upto = (len(prep) * (n + 1)) // len(chain)
            for s in prep[done:upto]:
                s()
            done = upto
    else:
        for s in prep:
            s()
        for stage in chain:
            stage(chains)

    @pl.when(j == last)
    def _():
        for b in range(nb):
            for g in range(N_GROUPS):
                s = state[b, g]
                for h in range(GROUP_HEADS):
                    sout_ref[b, g * GROUP_HEADS + h] = s[:, h * HEAD_DIM:(h + 1) * HEAD_DIM]


def _rwkv_masks(chunk):
    R = GROUP_HEADS * chunk
    row_h = np.arange(R) // chunk
    lane_h = np.arange(GROUP_W) // HEAD_DIM
    blk_p = row_h[:, None] == row_h[None, :]
    ones_blk = lane_h[:, None] == lane_h[None, :]
    s, t = np.arange(chunk)[:, None], (np.arange(R) % chunk)[None, :]
    return (jnp.asarray(blk_p, BF16), jnp.asarray(t > s, F32), jnp.asarray(t >= s, F32),
            jnp.asarray(ones_blk, BF16))


def _rwkv(zr, zp_first, s0, mu, vecs, w2, a2, g2, *, layer, chunk, n_valid, nb):
    b, t, _ = zr.shape
    rows = min(t, chunk)
    nc = t // rows
    pipelined = nc > 1
    lag = 1 if pipelined else 0
    solve_steps = max(1, (n_valid - 1).bit_length())
    masks = _rwkv_masks(chunk)
    full = lambda a: pl.BlockSpec(a.shape, lambda i, j: (0,) * a.ndim)
    return pl.pallas_call(
        functools.partial(_rwkv_body, chunk=chunk, n_valid=n_valid, solve_steps=solve_steps, nb=nb,
                          pipelined=pipelined),
        grid=(b // nb, nc + lag),
        in_specs=[pl.BlockSpec((nb, rows, RW_IN_W), lambda i, j: (i, jnp.minimum(j, nc - 1), 0)),
                  pl.BlockSpec((nb, 1, RW_IN_W), lambda i, j: (i, 0, 0)),
                  pl.BlockSpec((None, nb, RW_HEADS, HEAD_DIM, HEAD_DIM), lambda i, j: (layer, i, 0, 0, 0)),
                  full(mu), full(vecs), full(w2), full(a2), full(g2)] + [full(m) for m in masks],
        out_specs=[pl.BlockSpec((nb, rows, RW_W), lambda i, j: (i, jnp.maximum(j - lag, 0), 0)),
                   pl.BlockSpec((nb, RW_HEADS, HEAD_DIM, HEAD_DIM), lambda i, j: (i, 0, 0, 0))],
        out_shape=[jax.ShapeDtypeStruct((b, t, RW_W), F32),
                   jax.ShapeDtypeStruct((b, RW_HEADS, HEAD_DIM, HEAD_DIM), F32)],
        scratch_shapes=[pltpu.VMEM((nb, N_GROUPS, HEAD_DIM, GROUP_W), F32),
                        pltpu.VMEM((nb, chunk + 8, RW_IN_W), F32),
                        pltpu.VMEM((2, nb, 2 * chunk, RW_W), BF16),
                        pltpu.VMEM((2, nb, 2 * chunk, RW_W), BF16),
                        pltpu.VMEM((2, nb, HEAD_DIM, RW_HEADS * chunk), BF16),
                        pltpu.VMEM((2, nb, 8, RW_W), F32),
                        pltpu.VMEM((2, nb, chunk, RW_W), F32),
                        pltpu.VMEM((2, nb, chunk, RW_W), F32)],
        compiler_params=_params(2),
        name="rwkv_chunk%d" % chunk,
    )(zr, zp_first, s0, mu, vecs, w2, a2, g2, *masks)


def _outffn_body(x_ref, att_ref, rw_ref, wo_ref, n2_ref, wg_ref, wu_ref, wd_ref, nf_ref, xo_ref, *, final):
    mix = jnp.concatenate([att_ref[...], rw_ref[...]], axis=-1).astype(BF16)
    x1 = x_ref[...] + jnp.dot(mix, wo_ref[...], preferred_element_type=F32)
    hb = _rmsnorm(x1, n2_ref[...]).astype(BF16)
    d_ff = wg_ref.shape[1]
    x2 = x1
    for c0 in range(0, d_ff, FFN_CHUNK):
        cols = slice(c0, c0 + FFN_CHUNK)
        gate = jnp.dot(hb, wg_ref[:, cols], preferred_element_type=F32)
        up = jnp.dot(hb, wu_ref[:, cols], preferred_element_type=F32)
        act = (gate * jax.nn.sigmoid(gate) * up).astype(BF16)
        x2 = x2 + jnp.dot(act, wd_ref[cols, :], preferred_element_type=F32)
    xo_ref[...] = _rmsnorm(x2, nf_ref[...]) if final else x2


def _outffn(x, att, rw, wo, n2, wg, wu, wd, nf, *, final, tm):
    n = x.shape[0]
    resident = lambda a: pl.BlockSpec(a.shape, lambda i: (0,) * a.ndim, pipeline_mode=pl.Buffered(1))
    return pl.pallas_call(
        functools.partial(_outffn_body, final=final),
        grid=(n // tm,),
        in_specs=[pl.BlockSpec((tm, D_MODEL), lambda i: (i, 0)),
                  pl.BlockSpec((tm, ATT_W), lambda i: (i, 0)),
                  pl.BlockSpec((tm, RW_W), lambda i: (i, 0)),
                  resident(wo), resident(n2), resident(wg), resident(wu), resident(wd), resident(nf)],
        out_specs=pl.BlockSpec((tm, D_MODEL), lambda i: (i, 0)),
        out_shape=jax.ShapeDtypeStruct((n, D_MODEL), F32),
        compiler_params=_params(1),
        name="outffn_final" if final else "outffn",
    )(x, att, rw, wo, n2, wg, wu, wd, nf)


def _t5_bucket_np(d):
    max_exact = NUM_BUCKETS // 2
    n = np.maximum(d, 0)
    large = max_exact + (np.log(np.maximum(n, 1) / max_exact) / np.log(MAX_DISTANCE / max_exact)
                         * (NUM_BUCKETS - max_exact)).astype(np.int32)
    large = np.minimum(large, NUM_BUCKETS - 1)
    return np.where(n < max_exact, n, large).astype(np.int32)


def _bias_body(tab_ref, bucket_ref, inside_ref, o_ref, *, keys_major):
    bucket = bucket_ref[...]
    inside = inside_ref[...] > 0
    for h in range(N_Q_HEADS):
        acc = jnp.zeros(bucket.shape, F32)
        for bk in range(NUM_BUCKETS):
            acc = jnp.where(bucket == bk, tab_ref[bk, h], acc)
        acc = jnp.where(inside, acc, NEG)
        if keys_major:
            kv, g, q = h // GQA, h % GQA, bucket.shape[1]
            o_ref[kv, :, g * q:(g + 1) * q] = acc
        else:
            o_ref[h] = acc


def _masked_bias(rel_table, d, inside, rows_per_head, keys_major=False):
    q, s = d.shape
    bucket = np.zeros((rows_per_head, s), np.int32)
    bucket[:q] = _t5_bucket_np(d)
    ins = np.zeros((rows_per_head, s), np.int32)
    ins[:q] = inside
    if keys_major:
        bucket, ins = bucket.T.copy(), ins.T.copy()
        out_shape = (N_KV_HEADS, s, GQA * rows_per_head)
    else:
        out_shape = (N_Q_HEADS, rows_per_head, s)
    vm = lambda shape: pl.BlockSpec(shape, lambda: (0,) * len(shape))
    b = pl.pallas_call(
        functools.partial(_bias_body, keys_major=keys_major),
        in_specs=[pl.BlockSpec(memory_space=pltpu.SMEM), vm(bucket.shape), vm(ins.shape)],
        out_specs=vm(out_shape),
        out_shape=jax.ShapeDtypeStruct(out_shape, F32),
        name="rel_bias",
    )(rel_table.astype(F32), jnp.asarray(bucket), jnp.asarray(ins))
    return b if keys_major else b.reshape(N_KV_HEADS, GQA * rows_per_head, s)


def _sink_rows(sink_l, rows_per_head):
    return jnp.repeat(sink_l.astype(F32).reshape(N_KV_HEADS, GQA, 1), rows_per_head, axis=1).reshape(
        N_KV_HEADS, GQA * rows_per_head, 1)


def kernel(x_prompt, x_sample, cache_k, cache_v, state_shift, state_wkv, norm1, w_in, w_out, sink,
           rel_table, mu, w0, w2, a0, a2, g2, k_k, k_a, r_k, gn_w, gn_b, norm2, w_gate, w_up,
           w_down, norm_f):
    depth = w_in.shape[0]
    B, T, _ = x_prompt.shape
    DB, S, _ = x_sample.shape
    wc = cache_k.shape[2]
    assert wc == WINDOW and T >= WINDOW and S <= SAMPLE_PAD
    assert T % INPROJ_ROWS == 0 and (B * T) % OUTFFN_ROWS == 0 and T % (ATTN_BLOCKS * BLOCK) == 0
    assert T % PROMPT_CHUNK == 0 and B % RWKV_PROMPT_SEQS == 0
    assert DB % ATTN_SAMPLE_SEQS == 0 and DB % RWKV_SAMPLE_SEQS == 0

    qi = np.arange(BLOCK)[:, None]
    sj = np.arange(2 * BLOCK)[None, :]
    dp = qi + BLOCK - sj
    inside_p = (dp >= 0) & (dp <= WINDOW)
    bias_p = jnp.stack([_masked_bias(rel_table, dp, inside_p & (sj >= BLOCK), BLOCK, keys_major=True),
                        _masked_bias(rel_table, dp, inside_p, BLOCK, keys_major=True)])
    ds_full = (wc + np.arange(S))[:, None] - np.arange(wc + S)[None, :]
    inside_s = (ds_full >= 0) & (ds_full <= WINDOW)
    bias_c = _masked_bias(rel_table, ds_full[:, :wc], inside_s[:, :wc], SAMPLE_PAD)
    dn = np.full((S, SAMPLE_PAD), -1, np.int64)
    dn[:, :S] = ds_full[:, wc:]
    inn = np.zeros((S, SAMPLE_PAD), bool)
    inn[:, :S] = inside_s[:, wc:]
    bias_n = _masked_bias(rel_table, dn, inn, SAMPLE_PAD)

    xp = x_prompt.reshape(B * T, D_MODEL)
    xs = x_sample.reshape(DB * S, D_MODEL)
    zeros_shift = jnp.zeros((B, 1, RW_IN_W), F32)
    zeros_state = jnp.zeros((1, B, RW_HEADS, HEAD_DIM, HEAD_DIM), F32)
    cache_k2 = cache_k.reshape(depth, DB, wc, KV_W)
    cache_v2 = cache_v.reshape(depth, DB, wc, KV_W)
    nf = norm_f.reshape(1, D_MODEL)
    ones = jnp.ones((1, D_MODEL), F32)

    kp, vp, sp, wp, ksl, vsl, ssl, wsl = [], [], [], [], [], [], [], []
    for l in range(depth):
        w_in_b = _to_bf16(w_in, l)
        wo_b, wg_b, wu_b, wd_b = _to_bf16(w_out, l), _to_bf16(w_gate, l), _to_bf16(w_up, l), _to_bf16(w_down, l)
        n1 = norm1[l].reshape(1, D_MODEL)
        n2 = norm2[l].reshape(1, D_MODEL)
        mu_l = mu[l].reshape(1, RW_IN_W)
        vecs = jnp.stack([w0[l], a0[l], k_k[l], k_a[l], r_k[l], gn_w[l], gn_b[l], jnp.zeros_like(w0[l])])
        w2_b, a2_b, g2_b = w2[l].astype(BF16), a2[l].astype(BF16), g2[l].astype(BF16)
        final = l == depth - 1

        za, zr, h_last, kv_last = _inproj(xp, n1, w_in_b, norm=True, tm=INPROJ_ROWS, seq_len=T)
        za3 = za.reshape(B, T, ATT_IN_W)
        kp.append(kv_last[:, :, :KV_W].reshape(B, WINDOW, N_KV_HEADS, HEAD_DIM))
        vp.append(kv_last[:, :, KV_W:].reshape(B, WINDOW, N_KV_HEADS, HEAD_DIM))
        sp.append(h_last.reshape(B, D_MODEL))
        sink_lanes = jnp.swapaxes(_sink_rows(sink[l], BLOCK), 1, 2)
        att = _attn_prompt(za3, bias_p, sink_lanes, nblk=ATTN_BLOCKS)
        rw, wkv = _rwkv(zr.reshape(B, T, RW_IN_W), zeros_shift, zeros_state, mu_l, vecs, w2_b, a2_b, g2_b,
                        layer=0, chunk=PROMPT_CHUNK, n_valid=PROMPT_CHUNK, nb=RWKV_PROMPT_SEQS)
        wp.append(wkv)
        xp = _outffn(xp, att.reshape(B * T, ATT_W), rw.reshape(B * T, RW_W), wo_b, n2, wg_b, wu_b, wd_b, nf,
                     final=final, tm=OUTFFN_ROWS)

        za_s, zr_s, h_s = _inproj(xs, n1, w_in_b, norm=True, tm=DB * S)
        _, zr_prev, _ = _inproj(state_shift[l], ones, w_in_b, norm=False, tm=DB)
        ssl.append(h_s.reshape(DB, S, D_MODEL)[:, -1])
        att_s, kb, vb = _attn_sample(za_s.reshape(DB, S, ATT_IN_W), cache_k2, cache_v2,
                                     bias_c, bias_n, _sink_rows(sink[l], SAMPLE_PAD), layer=l, ns=ATTN_SAMPLE_SEQS)
        ksl.append(kb.reshape(DB, wc, N_KV_HEADS, HEAD_DIM))
        vsl.append(vb.reshape(DB, wc, N_KV_HEADS, HEAD_DIM))
        rw_s, wkv_s = _rwkv(zr_s.reshape(DB, S, RW_IN_W), zr_prev.reshape(DB, 1, RW_IN_W), state_wkv,
                            mu_l, vecs, w2_b, a2_b, g2_b, layer=l, chunk=SAMPLE_CHUNK, n_valid=S,
                            nb=RWKV_SAMPLE_SEQS)
        wsl.append(wkv_s)
        xs = _outffn(xs, att_s.reshape(DB * S, ATT_W), rw_s.reshape(DB * S, RW_W),
                     wo_b, n2, wg_b, wu_b, wd_b, nf, final=final, tm=DB * S)

    y_prompt = xp.reshape(B, T, D_MODEL)
    y_sample = xs.reshape(DB, S, D_MODEL)
    return (y_prompt, y_sample, jnp.stack(kp), jnp.stack(vp), jnp.stack(sp), jnp.stack(wp),
            jnp.stack(ksl), jnp.stack(vsl), jnp.stack(ssl), jnp.stack(wsl))
```

```python
import functools

import numpy as np
import jax
import jax.numpy as jnp
from jax import lax
from jax.experimental import pallas as pl
from jax.experimental.pallas import tpu as pltpu

D_MODEL = 1024
HEAD_DIM = 64
N_Q_HEADS = 8
N_KV_HEADS = 2
GQA = N_Q_HEADS // N_KV_HEADS
ATT_W = N_Q_HEADS * HEAD_DIM
KV_W = N_KV_HEADS * HEAD_DIM
WINDOW = 128
BLOCK = 128
NUM_BUCKETS = 32
MAX_DISTANCE = 128
RW_HEADS = 8
RW_W = RW_HEADS * HEAD_DIM
DECAY_LORA = 64
ICLR_LORA = 64
GATE_LORA = 128
RW_IN_W = 3 * RW_W + DECAY_LORA + ICLR_LORA + GATE_LORA
ATT_IN_W = ATT_W + 2 * KV_W
NORM_EPS = 1e-6
GN_EPS = 64e-5
NEG = -1e30
SCALE = HEAD_DIM ** -0.5

V7X_VMEM_LIMIT_BYTES = 56 * 1024 * 1024

V7X_LANES = 128
V7X_SUBLANES = 8

PROMPT_CHUNK = 64
SAMPLE_CHUNK = 16
SAMPLE_PAD = V7X_SUBLANES
INPROJ_ROWS = 1024
OUTFFN_ROWS = 1024
FFN_CHUNK = 256
ATTN_BLOCKS = 8
ATTN_SAMPLE_SEQS = 8
RWKV_PROMPT_SEQS = 8
RWKV_SAMPLE_SEQS = 4
CAST_ROWS = 1024

BF16 = jnp.bfloat16
F32 = jnp.float32


def _mm(a, b):
    return jnp.dot(a.astype(BF16), b.astype(BF16), preferred_element_type=F32)


def _mm_nt(a, b):
    return lax.dot_general(a.astype(BF16), b.astype(BF16), (((1,), (1,)), ((), ())),
                           preferred_element_type=F32)


def _rmsnorm(x, g):
    return x * lax.rsqrt(jnp.mean(x * x, -1, keepdims=True) + NORM_EPS) * g


def _params(n_axes):
    return pltpu.CompilerParams(dimension_semantics=("arbitrary",) * n_axes,
                                vmem_limit_bytes=V7X_VMEM_LIMIT_BYTES)


def _cast_body(w_ref, o_ref):
    o_ref[...] = w_ref[...].astype(BF16)


def _to_bf16(w, layer):
    _, rows, cols = w.shape
    tr = CAST_ROWS if rows % CAST_ROWS == 0 else rows
    return pl.pallas_call(
        _cast_body,
        grid=(rows // tr,),
        in_specs=[pl.BlockSpec((None, tr, cols), lambda i: (layer, i, 0))],
        out_specs=pl.BlockSpec((tr, cols), lambda i: (i, 0)),
        out_shape=jax.ShapeDtypeStruct((rows, cols), BF16),
        compiler_params=_params(1),
        name="cast_bf16",
    )(w)


def _inproj_body(x_ref, g_ref, w_ref, za_ref, zr_ref, h_ref, *tail_refs, norm, seq_len):
    x = x_ref[...]
    h = _rmsnorm(x, g_ref[...]) if norm else x
    z = jnp.dot(h.astype(BF16), w_ref[...], preferred_element_type=F32)
    za_ref[...] = z[:, :ATT_IN_W]
    zr_ref[...] = z[:, ATT_IN_W:]
    if seq_len is None:
        h_ref[...] = h
    else:
        (kv_ref,) = tail_refs
        tm = x.shape[0]
        h_ref[0] = h[tm - 1:, :]
        kv_ref[0] = z[tm - WINDOW:, ATT_W:ATT_IN_W]


def _inproj(x, g, w, *, norm, tm, seq_len=None):
    n = x.shape[0]
    in_w = w.shape[1]
    if seq_len is None:
        tail_specs = [pl.BlockSpec((tm, D_MODEL), lambda i: (i, 0))]
        tail_shapes = [jax.ShapeDtypeStruct((n, D_MODEL), F32)]
    else:
        tiles = seq_len // tm
        tail_specs = [pl.BlockSpec((1, 1, D_MODEL), lambda i: (i // tiles, 0, 0)),
                      pl.BlockSpec((1, WINDOW, 2 * KV_W), lambda i: (i // tiles, 0, 0))]
        tail_shapes = [jax.ShapeDtypeStruct((n // seq_len, 1, D_MODEL), F32),
                       jax.ShapeDtypeStruct((n // seq_len, WINDOW, 2 * KV_W), F32)]
    return pl.pallas_call(
        functools.partial(_inproj_body, norm=norm, seq_len=seq_len),
        grid=(n // tm,),
        in_specs=[pl.BlockSpec((tm, D_MODEL), lambda i: (i, 0)),
                  pl.BlockSpec((1, D_MODEL), lambda i: (0, 0)),
                  pl.BlockSpec((D_MODEL, in_w), lambda i: (0, 0))],
        out_specs=[pl.BlockSpec((tm, ATT_IN_W), lambda i: (i, 0)),
                   pl.BlockSpec((tm, RW_IN_W), lambda i: (i, 0))] + tail_specs,
        out_shape=[jax.ShapeDtypeStruct((n, ATT_IN_W), F32),
                   jax.ShapeDtypeStruct((n, RW_IN_W), F32)] + tail_shapes,
        compiler_params=_params(1),
        name="inproj_norm" if norm else "inproj_plain",
    )(x, g, w)


def _sink_softmax(score_parts, sink_col):
    m = sink_col
    for s in score_parts:
        m = jnp.maximum(m, jnp.max(s, -1, keepdims=True))
    es = [jnp.exp(s - m) for s in score_parts]
    den = jnp.exp(sink_col - m)
    for e in es:
        den = den + jnp.sum(e, -1, keepdims=True)
    return [e / den for e in es]


QG_W = GQA * HEAD_DIM


def _tile_kv_heads(x):
    rot = pltpu.roll(x, HEAD_DIM, 1)
    first = lax.broadcasted_iota(jnp.int32, x.shape, 1) < HEAD_DIM
    halves = (jnp.where(first, x, rot).astype(BF16), jnp.where(first, rot, x).astype(BF16))
    return [jnp.concatenate([h] * (QG_W // KV_W), axis=1) for h in halves]


def _attn_prompt_body(q_ref, kv_ref, bias_ref, sink_ref, o_ref, kt_prev, vT_prev, *, nblk):
    j = pl.program_id(1)

    @pl.when(j == 0)
    def _():
        kt_prev[...] = jnp.zeros_like(kt_prev)
        vT_prev[...] = jnp.zeros_like(vT_prev)

    lane = lax.broadcasted_iota(jnp.int32, (1, V7X_LANES), 1)
    lane_half = ((lane < HEAD_DIM).astype(BF16), (lane >= HEAD_DIM).astype(BF16))
    kts = [[kt_prev[kv] for kv in range(N_KV_HEADS)]]
    vTs = [vT_prev[...]]
    for i in range(nblk):
        rows = slice(i * BLOCK, (i + 1) * BLOCK)
        kts.append(_tile_kv_heads(kv_ref[0, rows, 0:KV_W]))
        vTs.append(kv_ref[0, rows, KV_W:2 * KV_W].T.astype(BF16))
    for kv in range(N_KV_HEADS):
        kt_prev[kv] = kts[nblk][kv]
    vT_prev[...] = vTs[nblk]

    units = []
    for i in range(nblk):
        qb = (q_ref[0, i * BLOCK:(i + 1) * BLOCK, :] * SCALE).astype(BF16)
        first = jnp.where(j == 0, 0, 1) if i == 0 else 1
        for kv in range(N_KV_HEADS):
            qe = _expand(qb[:, kv * QG_W:(kv + 1) * QG_W], lane_half)
            kcat = jnp.concatenate([kts[i][kv], kts[i + 1][kv]], axis=0)
            units.append(dict(i=i, kv=kv, s=_mm_nt(kcat, qe), first=first))
    for u in units:
        b = bias_ref[u["first"], u["kv"]]
        s = jnp.where(b > 0.5 * NEG, u["s"] + b, NEG)
        sink_row = sink_ref[u["kv"]]
        m = jnp.maximum(jnp.max(s, 0, keepdims=True), sink_row)
        e = jnp.exp(s - m)
        den = jnp.sum(e, 0, keepdims=True) + jnp.exp(sink_row - m)
        u["p"] = (e * (1.0 / den)).astype(BF16)
    pieces = [[None] * N_Q_HEADS for _ in range(nblk)]
    for u in units:
        i, kv = u["i"], u["kv"]
        hd = slice(kv * HEAD_DIM, (kv + 1) * HEAD_DIM)
        vcatT = jnp.concatenate([vTs[i][hd], vTs[i + 1][hd]], axis=1)
        oT = jnp.dot(vcatT, u["p"], preferred_element_type=F32)
        for g in range(GQA):
            pieces[i][kv * GQA + g] = oT[:, g * BLOCK:(g + 1) * BLOCK]
    for i in range(nblk):
        o_ref[0, i * BLOCK:(i + 1) * BLOCK, :] = jnp.concatenate(pieces[i], axis=0).T


def _attn_prompt(za, bias, sink_rows, *, nblk):
    b, t, _ = za.shape
    rows = nblk * BLOCK
    full = lambda a: pl.BlockSpec(a.shape, lambda i, j: (0,) * a.ndim)
    return pl.pallas_call(
        functools.partial(_attn_prompt_body, nblk=nblk),
        grid=(b, t // rows),
        in_specs=[pl.BlockSpec((1, rows, ATT_W), lambda i, j: (i, j, 0)),
                  pl.BlockSpec((1, rows, 2 * KV_W), lambda i, j: (i, j, ATT_W // (2 * KV_W))),
                  full(bias), full(sink_rows)],
        out_specs=pl.BlockSpec((1, rows, ATT_W), lambda i, j: (i, j, 0)),
        out_shape=jax.ShapeDtypeStruct((b, t, ATT_W), F32),
        scratch_shapes=[pltpu.VMEM((N_KV_HEADS, BLOCK, QG_W), BF16),
                        pltpu.VMEM((KV_W, BLOCK), BF16)],
        compiler_params=_params(2),
        name="attn_prompt",
    )(za, za, bias, sink_rows)


def _attn_sample_body(z_ref, ck_ref, cv_ref, bias_c_ref, bias_n_ref, sink_ref,
                      o_ref, kb_ref, vb_ref, *, n_new, ns):
    units = []
    for s in range(ns):
        z = jnp.concatenate([z_ref[s], jnp.zeros((SAMPLE_PAD - n_new, ATT_IN_W), F32)], axis=0)
        q = z[:, :ATT_W]
        kn = z[:, ATT_W:ATT_W + KV_W]
        vn = z[:, ATT_W + KV_W:]
        ck, cv = ck_ref[s], cv_ref[s]
        keep = WINDOW - n_new
        kb_ref[s, 0:keep, :] = ck_ref[s, n_new:WINDOW, :]
        kb_ref[s, keep:WINDOW, :] = kn[0:n_new]
        vb_ref[s, 0:keep, :] = cv_ref[s, n_new:WINDOW, :]
        vb_ref[s, keep:WINDOW, :] = vn[0:n_new]
        for kv in range(N_KV_HEADS):
            sl = slice(kv * HEAD_DIM, (kv + 1) * HEAD_DIM)
            qg = jnp.concatenate(
                [q[:, (kv * GQA + g) * HEAD_DIM:(kv * GQA + g + 1) * HEAD_DIM] for g in range(GQA)], axis=0)
            units.append(dict(s=s, kv=kv, qg=qg, ck=ck[:, sl], kn=kn[:, sl], cv=cv[:, sl], vn=vn[:, sl]))
    for u in units:
        u["sc"] = _mm_nt(u["qg"], u["ck"]) * SCALE
        u["sn"] = _mm_nt(u["qg"], u["kn"]) * SCALE
    for u in units:
        bc, bn = bias_c_ref[u["kv"]], bias_n_ref[u["kv"]]
        sc = jnp.where(bc > 0.5 * NEG, u["sc"] + bc, NEG)
        sn = jnp.where(bn > 0.5 * NEG, u["sn"] + bn, NEG)
        u["pc"], u["pn"] = _sink_softmax([sc, sn], sink_ref[u["kv"]])
    outs = [[] for _ in range(ns)]
    for u in units:
        o = _mm(u["pc"], u["cv"]) + _mm(u["pn"], u["vn"])
        outs[u["s"]] += [o[g * SAMPLE_PAD:(g + 1) * SAMPLE_PAD] for g in range(GQA)]
    for s in range(ns):
        o_ref[s] = jnp.concatenate(outs[s], axis=1)[:n_new]


def _attn_sample(za, ck, cv, bias_c, bias_n, sink_rows, *, layer, ns):
    db, n_new, _ = za.shape
    full = lambda a: pl.BlockSpec(a.shape, lambda i: (0,) * a.ndim)
    return pl.pallas_call(
        functools.partial(_attn_sample_body, n_new=n_new, ns=ns),
        grid=(db // ns,),
        in_specs=[pl.BlockSpec((ns, n_new, ATT_IN_W), lambda i: (i, 0, 0)),
                  pl.BlockSpec((None, ns, WINDOW, KV_W), lambda i: (layer, i, 0, 0)),
                  pl.BlockSpec((None, ns, WINDOW, KV_W), lambda i: (layer, i, 0, 0)),
                  full(bias_c), full(bias_n), full(sink_rows)],
        out_specs=[pl.BlockSpec((ns, n_new, ATT_W), lambda i: (i, 0, 0)),
                   pl.BlockSpec((ns, WINDOW, KV_W), lambda i: (i, 0, 0)),
                   pl.BlockSpec((ns, WINDOW, KV_W), lambda i: (i, 0, 0))],
        out_shape=[jax.ShapeDtypeStruct((db, n_new, ATT_W), F32),
                   jax.ShapeDtypeStruct((db, WINDOW, KV_W), F32),
                   jax.ShapeDtypeStruct((db, WINDOW, KV_W), F32)],
        compiler_params=_params(1),
        name="attn_sample",
    )(za, ck, cv, bias_c, bias_n, sink_rows)


GROUP_HEADS = 4
GROUP_W = GROUP_HEADS * HEAD_DIM
N_GROUPS = RW_HEADS // GROUP_HEADS


def _head_sums(x, ones_blk):
    m = x.shape[0]
    hi = x.astype(BF16)
    lo = (x - hi.astype(F32)).astype(BF16)
    hl = jnp.concatenate([hi, lo], axis=0)
    y = jnp.concatenate(
        [jnp.dot(hl[:, g * GROUP_W:(g + 1) * GROUP_W], ones_blk, preferred_element_type=F32)
         for g in range(N_GROUPS)], axis=1)
    return y[:m] + y[m:]


def _expand(x, lane_half):
    xb = x.astype(BF16)
    c = xb.shape[0]
    zero = jnp.zeros((c, V7X_LANES), BF16)
    blocks = []
    for h in range(GROUP_HEADS):
        col = h // 2
        piece = xb[:, col * V7X_LANES:(col + 1) * V7X_LANES] * lane_half[h % 2]
        blocks.append(jnp.concatenate([piece, zero] if col == 0 else [zero, piece], axis=1))
    return jnp.concatenate(blocks, axis=0)


def _pack_state(s, g):
    return jnp.concatenate([s[g * GROUP_HEADS + h] for h in range(GROUP_HEADS)], axis=1)


def _heads_to_lanes(x, c):
    heads = x.shape[0] // HEAD_DIM
    return jnp.concatenate([x[h * HEAD_DIM:(h + 1) * HEAD_DIM, :c] for h in range(heads)], axis=1)


def _rwkv_body(z_ref, zpf_ref, s0_ref, mu_ref, vec_ref, w2_ref, a2_ref, g2_ref,
               blkp_ref, strict_ref, incl_ref, ones_ref,
               o_ref, sout_ref, state, zbuf, ar_s, bk_s, vt_s, gl_s, bonus_s, gate_s,
               *, chunk, n_valid, solve_steps, nb, pipelined):
    C = chunk
    R = GROUP_HEADS * C
    j = pl.program_id(1)
    last = pl.num_programs(1) - 1
    wslot = j % 2 if pipelined else 0
    rslot = 1 - wslot if pipelined else 0

    @pl.when(j == 0)
    def _():
        for b in range(nb):
            s0 = s0_ref[b]
            for g in range(N_GROUPS):
                state[b, g] = _pack_state(s0, g)
            zbuf[b, 7:8, :] = zpf_ref[b]
        if pipelined:
            ar_s[1] = jnp.zeros(ar_s.shape[1:], BF16)
            bk_s[1] = jnp.zeros(bk_s.shape[1:], BF16)
            vt_s[1] = jnp.zeros(vt_s.shape[1:], BF16)
            gl_s[1] = jnp.ones(gl_s.shape[1:], F32)
            bonus_s[1] = jnp.zeros(bonus_s.shape[1:], F32)
            gate_s[1] = jnp.zeros(gate_s.shape[1:], F32)

    @pl.when(j > 0)
    def _():
        for b in range(nb):
            zbuf[b, 7:8, :] = zbuf[b, C + 7:C + 8, :]

    w0, a0, k_k, k_a = vec_ref[0:1, :], vec_ref[1:2, :], vec_ref[2:3, :], vec_ref[3:4, :]
    r_k, gn_w, gn_b = vec_ref[4:5, :], vec_ref[5:6, :], vec_ref[6:7, :]
    ones_blk = ones_ref[...]
    ri = lax.broadcasted_iota(jnp.int32, (C, C), 0)
    ci = lax.broadcasted_iota(jnp.int32, (C, C), 1)
    tri = (ri >= ci).astype(BF16)
    lane = lax.broadcasted_iota(jnp.int32, (1, V7X_LANES), 1)
    blk = ((lane < HEAD_DIM).astype(BF16), (lane >= HEAD_DIM).astype(BF16))

    def prep_stages(b):
        p = {}

        def s1():
            zc = z_ref[b]
            if zc.shape[0] < C:
                zc = jnp.concatenate([zc, jnp.zeros((C - zc.shape[0], RW_IN_W), F32)], axis=0)
            zbuf[b, 8:8 + C, :] = zc
            zp = zbuf[b, 7:7 + C, :]
            zm = zc + (zp - zc) * mu_ref[...]
            p["r"] = zm[:, 0:RW_W]
            p["k"] = zm[:, RW_W:2 * RW_W]
            p["v"] = zm[:, 2 * RW_W:3 * RW_W]
            o1 = 3 * RW_W
            p["zw"] = zm[:, o1:o1 + DECAY_LORA]
            p["za"] = zm[:, o1 + DECAY_LORA:o1 + DECAY_LORA + ICLR_LORA]
            p["zg"] = zm[:, o1 + DECAY_LORA + ICLR_LORA:]

        def s2():
            y = -(w0 + _mm(jnp.tanh(p["zw"]), w2_ref[...]))
            softplus = jnp.maximum(y, 0.0) + jnp.log(1.0 + jnp.exp(-jnp.abs(y)))
            p["logdecay"] = -jnp.exp(-softplus - 0.5)
            p["a"] = jax.nn.sigmoid(a0 + _mm(p["za"], a2_ref[...]))
            gate_s[wslot, b] = _mm(jax.nn.sigmoid(p["zg"]), g2_ref[...])

        def s3():
            k, v, a = p["k"], p["v"], p["a"]
            kk_raw = k * k_k
            kmod = k * (1.0 + (a - 1.0) * k_a)
            vv = v
            if n_valid < C:
                rows = lax.broadcasted_iota(jnp.int32, (C, 1), 0) < n_valid
                p["logdecay"] = jnp.where(rows, p["logdecay"], 0.0)
                kk_raw = jnp.where(rows, kk_raw, 0.0)
                kmod = jnp.where(rows, kmod, 0.0)
                vv = jnp.where(rows, v, 0.0)
            vpad = vv
            if C < V7X_LANES:
                vpad = jnp.concatenate([vv, jnp.zeros((V7X_LANES - C, RW_W), F32)], axis=0)
            vt_s[wslot, b] = _heads_to_lanes(vpad.T, C).astype(BF16)
            sums = _head_sums(jnp.concatenate([kk_raw * kk_raw, p["r"] * kmod * r_k], axis=0), ones_blk)
            p["kk"] = kk_raw * lax.rsqrt(jnp.maximum(sums[:C], 1e-24))
            bonus_s[wslot, b] = sums[C:] * v
            p["kmod"] = kmod

        def s4():
            logdecay = p["logdecay"]
            hi = logdecay.astype(BF16)
            lo = (logdecay - hi.astype(F32)).astype(BF16)
            cum = jnp.dot(tri, jnp.concatenate([hi, lo], axis=1), preferred_element_type=F32)
            cum = cum[:, :RW_W] + cum[:, RW_W:]
            p["cum"] = cum
            gam = jnp.exp(cum)
            p["gam"] = gam
            gl_s[wslot, b] = jnp.broadcast_to(gam[C - 1:C, :], (8, RW_W))

        def s5():
            cum, gam, kk = p["cum"], p["gam"], p["kk"]
            gam_inv = jnp.exp(-cum)
            At = -kk * jnp.exp(cum - p["logdecay"])
            Rt = p["r"] * gam
            Bt = kk * p["a"] * gam_inv
            Kt = p["kmod"] * gam_inv
            ar_s[wslot, b] = jnp.concatenate([At, Rt], axis=0).astype(BF16)
            bk_s[wslot, b] = jnp.concatenate([Bt, Kt], axis=0).astype(BF16)

        return [s1, s2, s3, s4, s5]

    strict = strict_ref[...]
    incl = incl_ref[...]
    blk_p = blkp_ref[...]
    chains = [dict(b=b, grp=grp, sl=slice(grp * GROUP_W, (grp + 1) * GROUP_W),
                   slt=slice(grp * R, (grp + 1) * R))
              for b in range(nb) for grp in range(N_GROUPS)]
    on_parts = {b: [None] * N_GROUPS for b in range(nb)}

    def expand_t(x):
        if C == HEAD_DIM:
            return _expand(x, blk)
        return jnp.concatenate([x.astype(BF16)] * GROUP_HEADS, axis=0) * blk_p

    def c_load(chs):
        for ch in chs:
            b, sl = ch["b"], ch["sl"]
            ar = ar_s[rslot, b, :, sl]
            ch["BK"] = bk_s[rslot, b, :, sl]
            ch["VT"] = vt_s[rslot, b, :, ch["slt"]]
            ch["ARe"] = jnp.concatenate([_expand(ar[:C], blk), _expand(ar[C:], blk)], axis=0)
            ch["BKe"] = jnp.concatenate([_expand(ch["BK"][:C], blk), _expand(ch["BK"][C:], blk)], axis=0)

    def c_gram(chs):
        for ch in chs:
            ch["S"] = state[ch["b"], ch["grp"]]
            lhs = jnp.concatenate([ch["BK"], ch["S"].astype(BF16)], axis=0)
            GT = _mm_nt(lhs, ch["ARe"])
            ch["Q"] = (GT[:C, :R] * strict).astype(BF16)
            ch["LakT"] = (GT[C:2 * C, :R] * strict).astype(BF16)
            ch["MT"] = jnp.concatenate([expand_t(GT[:C, R:] * incl), expand_t(GT[C:2 * C, R:] * incl)], axis=0)
            ch["P0T"] = GT[2 * C:, :]

    def c_rhs(chs):
        for ch in chs:
            ch["XT"] = ch["P0T"][:, :R] + jnp.dot(ch["VT"], expand_t(ch["LakT"]), preferred_element_type=F32)

    def c_solve(i):
        def step(chs):
            for ch in chs:
                qe = expand_t(ch["Q"])
                xb = ch["XT"].astype(BF16)
                if i + 1 < solve_steps:
                    res = jnp.dot(jnp.concatenate([xb, ch["Q"]], axis=0), qe, preferred_element_type=F32)
                    ch["XT"] = ch["XT"] + res[:HEAD_DIM]
                    ch["Q"] = res[HEAD_DIM:].astype(BF16)
                else:
                    ch["XT"] = ch["XT"] + jnp.dot(xb, qe, preferred_element_type=F32)
        return step

    def c_out(chs):
        for ch in chs:
            b, grp, sl = ch["b"], ch["grp"], ch["sl"]
            uv = jnp.concatenate([ch["XT"].astype(BF16), ch["VT"]], axis=1)
            rhs = jnp.concatenate([ch["MT"], ch["BKe"]], axis=1)
            res = jnp.dot(uv, rhs, preferred_element_type=F32)
            state[b, grp] = (ch["S"] + res[:, R:]) * gl_s[rslot, b, 0:1, sl]
            OT = ch["P0T"][:, R:] + res[:, :R]
            mean = jnp.mean(OT, 0, keepdims=True)
            d = OT - mean
            var = jnp.mean(d * d, 0, keepdims=True)
            onT = d * lax.rsqrt(var + GN_EPS)
            rows = jnp.concatenate([onT[:, h * C:(h + 1) * C] for h in range(GROUP_HEADS)], axis=0)
            if C < V7X_LANES:
                rows = jnp.concatenate([rows, jnp.zeros((GROUP_W, V7X_LANES - C), F32)], axis=1)
            on_parts[b][grp] = rows.T[:C]

    def c_tail(chs):
        for b in sorted({ch["b"] for ch in chs}):
            on = jnp.concatenate(on_parts[b], axis=1)
            out = (on * gn_w + gn_b + bonus_s[rslot, b]) * gate_s[rslot, b]
            o_ref[b] = out[:o_ref.shape[1]]

    chain = [c_load, c_gram, c_rhs] + [c_solve(i) for i in range(solve_steps)] + [c_out, c_tail]
    prep = [s for b in range(nb) for s in prep_stages(b)]
    if pipelined:
        done = 0
        for n, stage in enumerate(chain):
            stage(chains)
            upto = (len(prep) * (n + 1)) // len(chain)
            for s in prep[done:upto]:
                s()
            done = upto
    else:
        for s in prep:
            s()
        for stage in chain:
            stage(chains)

    @pl.when(j == last)
    def _():
        for b in range(nb):
            for g in range(N_GROUPS):
                s = state[b, g]
                for h in range(GROUP_HEADS):
                    sout_ref[b, g * GROUP_HEADS + h] = s[:, h * HEAD_DIM:(h + 1) * HEAD_DIM]


def _rwkv_masks(chunk):
    R = GROUP_HEADS * chunk
    row_h = np.arange(R) // chunk
    lane_h = np.arange(GROUP_W) // HEAD_DIM
    blk_p = row_h[:, None] == row_h[None, :]
    ones_blk = lane_h[:, None] == lane_h[None, :]
    s, t = np.arange(chunk)[:, None], (np.arange(R) % chunk)[None, :]
    return (jnp.asarray(blk_p, BF16), jnp.asarray(t > s, F32), jnp.asarray(t >= s, F32),
            jnp.asarray(ones_blk, BF16))


def _rwkv(zr, zp_first, s0, mu, vecs, w2, a2, g2, *, layer, chunk, n_valid, nb):
    b, t, _ = zr.shape
    rows = min(t, chunk)
    nc = t // rows
    pipelined = nc > 1
    lag = 1 if pipelined else 0
    solve_steps = max(1, (n_valid - 1).bit_length())
    masks = _rwkv_masks(chunk)
    full = lambda a: pl.BlockSpec(a.shape, lambda i, j: (0,) * a.ndim)
    return pl.pallas_call(
        functools.partial(_rwkv_body, chunk=chunk, n_valid=n_valid, solve_steps=solve_steps, nb=nb,
                          pipelined=pipelined),
        grid=(b // nb, nc + lag),
        in_specs=[pl.BlockSpec((nb, rows, RW_IN_W), lambda i, j: (i, jnp.minimum(j, nc - 1), 0)),
                  pl.BlockSpec((nb, 1, RW_IN_W), lambda i, j: (i, 0, 0)),
                  pl.BlockSpec((None, nb, RW_HEADS, HEAD_DIM, HEAD_DIM), lambda i, j: (layer, i, 0, 0, 0)),
                  full(mu), full(vecs), full(w2), full(a2), full(g2)] + [full(m) for m in masks],
        out_specs=[pl.BlockSpec((nb, rows, RW_W), lambda i, j: (i, jnp.maximum(j - lag, 0), 0)),
                   pl.BlockSpec((nb, RW_HEADS, HEAD_DIM, HEAD_DIM), lambda i, j: (i, 0, 0, 0))],
        out_shape=[jax.ShapeDtypeStruct((b, t, RW_W), F32),
                   jax.ShapeDtypeStruct((b, RW_HEADS, HEAD_DIM, HEAD_DIM), F32)],
        scratch_shapes=[pltpu.VMEM((nb, N_GROUPS, HEAD_DIM, GROUP_W), F32),
                        pltpu.VMEM((nb, chunk + 8, RW_IN_W), F32),
                        pltpu.VMEM((2, nb, 2 * chunk, RW_W), BF16),
                        pltpu.VMEM((2, nb, 2 * chunk, RW_W), BF16),
                        pltpu.VMEM((2, nb, HEAD_DIM, RW_HEADS * chunk), BF16),
                        pltpu.VMEM((2, nb, 8, RW_W), F32),
                        pltpu.VMEM((2, nb, chunk, RW_W), F32),
                        pltpu.VMEM((2, nb, chunk, RW_W), F32)],
        compiler_params=_params(2),
        name="rwkv_chunk%d" % chunk,
    )(zr, zp_first, s0, mu, vecs, w2, a2, g2, *masks)


def _outffn_body(x_ref, att_ref, rw_ref, wo_ref, n2_ref, wg_ref, wu_ref, wd_ref, nf_ref, xo_ref, *, final):
    mix = jnp.concatenate([att_ref[...], rw_ref[...]], axis=-1).astype(BF16)
    x1 = x_ref[...] + jnp.dot(mix, wo_ref[...], preferred_element_type=F32)
    hb = _rmsnorm(x1, n2_ref[...]).astype(BF16)
    d_ff = wg_ref.shape[1]
    x2 = x1
    for c0 in range(0, d_ff, FFN_CHUNK):
        cols = slice(c0, c0 + FFN_CHUNK)
        gate = jnp.dot(hb, wg_ref[:, cols], preferred_element_type=F32)
        up = jnp.dot(hb, wu_ref[:, cols], preferred_element_type=F32)
        act = (gate * jax.nn.sigmoid(gate) * up).astype(BF16)
        x2 = x2 + jnp.dot(act, wd_ref[cols, :], preferred_element_type=F32)
    xo_ref[...] = _rmsnorm(x2, nf_ref[...]) if final else x2


def _outffn(x, att, rw, wo, n2, wg, wu, wd, nf, *, final, tm):
    n = x.shape[0]
    resident = lambda a: pl.BlockSpec(a.shape, lambda i: (0,) * a.ndim, pipeline_mode=pl.Buffered(1))
    return pl.pallas_call(
        functools.partial(_outffn_body, final=final),
        grid=(n // tm,),
        in_specs=[pl.BlockSpec((tm, D_MODEL), lambda i: (i, 0)),
                  pl.BlockSpec((tm, ATT_W), lambda i: (i, 0)),
                  pl.BlockSpec((tm, RW_W), lambda i: (i, 0)),
                  resident(wo), resident(n2), resident(wg), resident(wu), resident(wd), resident(nf)],
        out_specs=pl.BlockSpec((tm, D_MODEL), lambda i: (i, 0)),
        out_shape=jax.ShapeDtypeStruct((n, D_MODEL), F32),
        compiler_params=_params(1),
        name="outffn_final" if final else "outffn",
    )(x, att, rw, wo, n2, wg, wu, wd, nf)


def _t5_bucket_np(d):
    max_exact = NUM_BUCKETS // 2
    n = np.maximum(d, 0)
    large = max_exact + (np.log(np.maximum(n, 1) / max_exact) / np.log(MAX_DISTANCE / max_exact)
                         * (NUM_BUCKETS - max_exact)).astype(np.int32)
    large = np.minimum(large, NUM_BUCKETS - 1)
    return np.where(n < max_exact, n, large).astype(np.int32)


def _bias_body(tab_ref, bucket_ref, inside_ref, o_ref, *, keys_major):
    bucket = bucket_ref[...]
    inside = inside_ref[...] > 0
    for h in range(N_Q_HEADS):
        acc = jnp.zeros(bucket.shape, F32)
        for bk in range(NUM_BUCKETS):
            acc = jnp.where(bucket == bk, tab_ref[bk, h], acc)
        acc = jnp.where(inside, acc, NEG)
        if keys_major:
            kv, g, q = h // GQA, h % GQA, bucket.shape[1]
            o_ref[kv, :, g * q:(g + 1) * q] = acc
        else:
            o_ref[h] = acc


def _masked_bias(rel_table, d, inside, rows_per_head, keys_major=False):
    q, s = d.shape
    bucket = np.zeros((rows_per_head, s), np.int32)
    bucket[:q] = _t5_bucket_np(d)
    ins = np.zeros((rows_per_head, s), np.int32)
    ins[:q] = inside
    if keys_major:
        bucket, ins = bucket.T.copy(), ins.T.copy()
        out_shape = (N_KV_HEADS, s, GQA * rows_per_head)
    else:
        out_shape = (N_Q_HEADS, rows_per_head, s)
    vm = lambda shape: pl.BlockSpec(shape, lambda: (0,) * len(shape))
    b = pl.pallas_call(
        functools.partial(_bias_body, keys_major=keys_major),
        in_specs=[pl.BlockSpec(memory_space=pltpu.SMEM), vm(bucket.shape), vm(ins.shape)],
        out_specs=vm(out_shape),
        out_shape=jax.ShapeDtypeStruct(out_shape, F32),
        name="rel_bias",
    )(rel_table.astype(F32), jnp.asarray(bucket), jnp.asarray(ins))
    return b if keys_major else b.reshape(N_KV_HEADS, GQA * rows_per_head, s)


def _sink_rows(sink_l, rows_per_head):
    return jnp.repeat(sink_l.astype(F32).reshape(N_KV_HEADS, GQA, 1), rows_per_head, axis=1).reshape(
        N_KV_HEADS, GQA * rows_per_head, 1)


def kernel(x_prompt, x_sample, cache_k, cache_v, state_shift, state_wkv, norm1, w_in, w_out, sink,
           rel_table, mu, w0, w2, a0, a2, g2, k_k, k_a, r_k, gn_w, gn_b, norm2, w_gate, w_up,
           w_down, norm_f):
    depth = w_in.shape[0]
    B, T, _ = x_prompt.shape
    DB, S, _ = x_sample.shape
    wc = cache_k.shape[2]
    assert wc == WINDOW and T >= WINDOW and S <= SAMPLE_PAD
    assert T % INPROJ_ROWS == 0 and (B * T) % OUTFFN_ROWS == 0 and T % (ATTN_BLOCKS * BLOCK) == 0
    assert T % PROMPT_CHUNK == 0 and B % RWKV_PROMPT_SEQS == 0
    assert DB % ATTN_SAMPLE_SEQS == 0 and DB % RWKV_SAMPLE_SEQS == 0

    qi = np.arange(BLOCK)[:, None]
    sj = np.arange(2 * BLOCK)[None, :]
    dp = qi + BLOCK - sj
    inside_p = (dp >= 0) & (dp <= WINDOW)
    bias_p = jnp.stack([_masked_bias(rel_table, dp, inside_p & (sj >= BLOCK), BLOCK, keys_major=True),
                        _masked_bias(rel_table, dp, inside_p, BLOCK, keys_major=True)])
    ds_full = (wc + np.arange(S))[:, None] - np.arange(wc + S)[None, :]
    inside_s = (ds_full >= 0) & (ds_full <= WINDOW)
    bias_c = _masked_bias(rel_table, ds_full[:, :wc], inside_s[:, :wc], SAMPLE_PAD)
    dn = np.full((S, SAMPLE_PAD), -1, np.int64)
    dn[:, :S] = ds_full[:, wc:]
    inn = np.zeros((S, SAMPLE_PAD), bool)
    inn[:, :S] = inside_s[:, wc:]
    bias_n = _masked_bias(rel_table, dn, inn, SAMPLE_PAD)

    xp = x_prompt.reshape(B * T, D_MODEL)
    xs = x_sample.reshape(DB * S, D_MODEL)
    zeros_shift = jnp.zeros((B, 1, RW_IN_W), F32)
    zeros_state = jnp.zeros((1, B, RW_HEADS, HEAD_DIM, HEAD_DIM), F32)
    cache_k2 = cache_k.reshape(depth, DB, wc, KV_W)
    cache_v2 = cache_v.reshape(depth, DB, wc, KV_W)
    nf = norm_f.reshape(1, D_MODEL)
    ones = jnp.ones((1, D_MODEL), F32)

    kp, vp, sp, wp, ksl, vsl, ssl, wsl = [], [], [], [], [], [], [], []
    for l in range(depth):
        w_in_b = _to_bf16(w_in, l)
        wo_b, wg_b, wu_b, wd_b = _to_bf16(w_out, l), _to_bf16(w_gate, l), _to_bf16(w_up, l), _to_bf16(w_down, l)
        n1 = norm1[l].reshape(1, D_MODEL)
        n2 = norm2[l].reshape(1, D_MODEL)
        mu_l = mu[l].reshape(1, RW_IN_W)
        vecs = jnp.stack([w0[l], a0[l], k_k[l], k_a[l], r_k[l], gn_w[l], gn_b[l], jnp.zeros_like(w0[l])])
        w2_b, a2_b, g2_b = w2[l].astype(BF16), a2[l].astype(BF16), g2[l].astype(BF16)
        final = l == depth - 1

        za, zr, h_last, kv_last = _inproj(xp, n1, w_in_b, norm=True, tm=INPROJ_ROWS, seq_len=T)
        za3 = za.reshape(B, T, ATT_IN_W)
        kp.append(kv_last[:, :, :KV_W].reshape(B, WINDOW, N_KV_HEADS, HEAD_DIM))
        vp.append(kv_last[:, :, KV_W:].reshape(B, WINDOW, N_KV_HEADS, HEAD_DIM))
        sp.append(h_last.reshape(B, D_MODEL))
        sink_lanes = jnp.swapaxes(_sink_rows(sink[l], BLOCK), 1, 2)
        att = _attn_prompt(za3, bias_p, sink_lanes, nblk=ATTN_BLOCKS)
        rw, wkv = _rwkv(zr.reshape(B, T, RW_IN_W), zeros_shift, zeros_state, mu_l, vecs, w2_b, a2_b, g2_b,
                        layer=0, chunk=PROMPT_CHUNK, n_valid=PROMPT_CHUNK, nb=RWKV_PROMPT_SEQS)
        wp.append(wkv)
        xp = _outffn(xp, att.reshape(B * T, ATT_W), rw.reshape(B * T, RW_W), wo_b, n2, wg_b, wu_b, wd_b, nf,
                     final=final, tm=OUTFFN_ROWS)

        za_s, zr_s, h_s = _inproj(xs, n1, w_in_b, norm=True, tm=DB * S)
        _, zr_prev, _ = _inproj(state_shift[l], ones, w_in_b, norm=False, tm=DB)
        ssl.append(h_s.reshape(DB, S, D_MODEL)[:, -1])
        att_s, kb, vb = _attn_sample(za_s.reshape(DB, S, ATT_IN_W), cache_k2, cache_v2,
                                     bias_c, bias_n, _sink_rows(sink[l], SAMPLE_PAD), layer=l, ns=ATTN_SAMPLE_SEQS)
        ksl.append(kb.reshape(DB, wc, N_KV_HEADS, HEAD_DIM))
        vsl.append(vb.reshape(DB, wc, N_KV_HEADS, HEAD_DIM))
        rw_s, wkv_s = _rwkv(zr_s.reshape(DB, S, RW_IN_W), zr_prev.reshape(DB, 1, RW_IN_W), state_wkv,
                            mu_l, vecs, w2_b, a2_b, g2_b, layer=l, chunk=SAMPLE_CHUNK, n_valid=S,
                            nb=RWKV_SAMPLE_SEQS)
        wsl.append(wkv_s)
        xs = _outffn(xs, att_s.reshape(DB * S, ATT_W), rw_s.reshape(DB * S, RW_W),
                     wo_b, n2, wg_b, wu_b, wd_b, nf, final=final, tm=DB * S)

    y_prompt = xp.reshape(B, T, D_MODEL)
    y_sample = xs.reshape(DB, S, D_MODEL)
    return (y_prompt, y_sample, jnp.stack(kp), jnp.stack(vp), jnp.stack(sp), jnp.stack(wp),
            jnp.stack(ksl), jnp.stack(vsl), jnp.stack(ssl), jnp.stack(wsl))
```
